```python
import jax
import jax.numpy as jnp
from jax import lax
import numpy as np

D_MODEL = 2048
BATCH = 16
SEQ = 2048
DEPTH = 4
DEC_BATCH = 8
DEC_SEQ = 64
PAST_LEN = 4096

CHUNK = 64
Q_BLOCK = 128
SUB = 16
N_SUB = CHUNK // SUB
N_EVEN = (DEPTH + 1) // 2
N_ODD = DEPTH // 2
EPS = 1e-6

D_CONV = D_MODEL // 2
CONV_W = 3

MLA_HEADS = 8
Q_LORA = 512
KV_LORA = 512
NOPE_DIM = 128
ROPE_DIM = 64
V_DIM = 128
ROPE_THETA = 10000.0
MLA_SCALE = (NOPE_DIM + ROPE_DIM) ** -0.5

HG_HEADS = 16
HG_DK = 128
HG_DV = 128

D_FF = 5632

D_IN_A = 3 * D_CONV + Q_LORA + KV_LORA + ROPE_DIM
D_CAT_A = D_CONV + MLA_HEADS * V_DIM
D_IN_C = 2 * HG_HEADS * HG_DK + 2 * HG_HEADS * HG_DV

kernel_name = 'streaming_conv_mla_hgrn2_macaron'


def _rmsnorm(x, g):
    xf = x.astype(jnp.float32)
    y = xf * lax.rsqrt(jnp.mean(xf * xf, axis=-1, keepdims=True) + EPS)
    return (y * g.astype(jnp.float32)).astype(x.dtype)


def _swiglu(h, w_gate, w_up, w_down):
    return (jax.nn.silu(h @ w_gate) * (h @ w_up)) @ w_down


def _rope(x, pos):
    half = ROPE_DIM // 2
    inv = ROPE_THETA ** (-jnp.arange(half, dtype=jnp.float32) / half)
    ang = pos.astype(jnp.float32)[:, None] * inv[None, :]
    shape = (ang.shape[0],) + (1,) * (x.ndim - 3) + (half,)
    cos = jnp.cos(ang).reshape(shape)
    sin = jnp.sin(ang).reshape(shape)
    xf = x.astype(jnp.float32)
    x1, x2 = xf[..., :half], xf[..., half:]
    return jnp.concatenate([x1 * cos - x2 * sin, x1 * sin + x2 * cos], axis=-1).astype(x.dtype)


def _causal_dwconv(v, prev, w):
    t = v.shape[1]
    vp = jnp.concatenate([prev, v], axis=1)
    y = vp[:, 0:t] * w[0]
    for j in range(1, CONV_W):
        y = y + vp[:, j:j + t] * w[j]
    return y, vp[:, -(CONV_W - 1):]


def _mla_attend(q_lat, q_pe, c_kv, k_pe, mask):
    s = jnp.einsum('bthc,bsc->bhts', q_lat, c_kv, preferred_element_type=jnp.float32)
    s = s + jnp.einsum('bthr,bsr->bhts', q_pe, k_pe, preferred_element_type=jnp.float32)
    s = s * MLA_SCALE
    if mask is not None:
        s = jnp.where(mask, s, -jnp.inf)
    p = jax.nn.softmax(s, axis=-1).astype(c_kv.dtype)
    return jnp.einsum('bhts,bsc->bthc', p, c_kv)


def _mla_prompt(q_lat, q_pe, c_kv, k_pe):
    t = q_lat.shape[1]
    outs = []
    for q0 in range(0, t, Q_BLOCK):
        q1 = min(q0 + Q_BLOCK, t)
        q_chunk = jnp.arange(q0, q1) // CHUNK
        k_chunk = jnp.arange(q1) // CHUNK
        mask = k_chunk[None, :] <= q_chunk[:, None]
        outs.append(_mla_attend(q_lat[:, q0:q1], q_pe[:, q0:q1], c_kv[:, :q1], k_pe[:, :q1], mask))
    return jnp.concatenate(outs, axis=1)


def _conv_mla_mixer(h, pos, conv_prev, ckv_prev, kpe_prev, w_in, w_conv, g_q, w_uq, g_kv, w_ukv, w_out):
    b, t, _ = h.shape
    z = h @ w_in
    o1, o2, o3 = D_CONV, 2 * D_CONV, 3 * D_CONV
    o4, o5 = o3 + Q_LORA, o3 + Q_LORA + KV_LORA
    gate_b, gate_c, u = z[..., :o1], z[..., o1:o2], z[..., o2:o3]
    c_q, c_kv, k_pe = z[..., o3:o4], z[..., o4:o5], z[..., o5:]
    if conv_prev is None:
        conv_prev = jnp.zeros((b, CONV_W - 1, D_CONV), h.dtype)
    conv, conv_state = _causal_dwconv(gate_c * u, conv_prev, w_conv)
    y_a = gate_b * conv
    q = (_rmsnorm(c_q, g_q) @ w_uq).reshape(b, t, MLA_HEADS, NOPE_DIM + ROPE_DIM)
    q_nope, q_pe = q[..., :NOPE_DIM], _rope(q[..., NOPE_DIM:], pos)
    c_kv = _rmsnorm(c_kv, g_kv)
    k_pe = _rope(k_pe, pos)
    w_ukv = w_ukv.reshape(KV_LORA, MLA_HEADS, NOPE_DIM + V_DIM)
    q_lat = jnp.einsum('bthn,chn->bthc', q_nope, w_ukv[..., :NOPE_DIM])
    if ckv_prev is None:
        o_lat = _mla_prompt(q_lat, q_pe, c_kv, k_pe)
    else:
        keys_c = jnp.concatenate([ckv_prev, c_kv], axis=1)
        keys_r = jnp.concatenate([kpe_prev, k_pe], axis=1)
        o_lat = _mla_attend(q_lat, q_pe, keys_c, keys_r, None)
    y_b = jnp.einsum('bthc,chv->bthv', o_lat, w_ukv[..., NOPE_DIM:]).reshape(b, t, MLA_HEADS * V_DIM)
    y = jnp.concatenate([y_a, y_b], axis=-1) @ w_out
    return y, conv_state, c_kv, k_pe


def _gla_chunk_step(S, inp):
    q, k, v, g = inp
    b, h, c, dk = q.shape
    A = jnp.cumsum(g, axis=2)
    Ar = A.reshape(b, h, N_SUB, SUB, dk)
    qr = q.reshape(b, h, N_SUB, SUB, dk)
    kr = k.reshape(b, h, N_SUB, SUB, dk)
    vr = v.reshape(b, h, N_SUB, SUB, v.shape[-1])
    A_end = Ar[:, :, :, -1]
    tril_sub = jnp.tril(jnp.ones((SUB, SUB), jnp.float32))
    strict = jnp.tril(jnp.ones((N_SUB, N_SUB), jnp.float32), -1)
    dec = jnp.exp(jnp.minimum(Ar[:, :, :, :, None] - Ar[:, :, :, None], 0.0))
    s_diag = jnp.einsum('bhntd,bhnsd,bhntsd->bhnts', qr, kr, dec) * tril_sub
    k_end = kr * jnp.exp(A_end[:, :, :, None] - Ar)
    q_rel = qr[:, :, :, :, None] * jnp.exp(jnp.minimum(Ar[:, :, :, :, None] - A_end[:, :, None, None], 0.0))
    s_off = jnp.einsum('bhitjd,bhjsd->bhitjs', q_rel, k_end) * strict[:, None, :, None]
    o = jnp.einsum('bhnts,bhnse->bhnte', s_diag, vr) + jnp.einsum('bhitjs,bhjse->bhite', s_off, vr)
    o = o.reshape(b, h, c, -1) + jnp.einsum('bhcd,bhde->bhce', q * jnp.exp(A), S)
    A_last = A[:, :, -1]
    S_new = jnp.exp(A_last)[..., None] * S + jnp.einsum('bhcd,bhce->bhde', k * jnp.exp(A_last[:, :, None] - A), v)
    return S_new, o


def _gla_scan(q, k, v, g, S0):
    b, h, t, _ = q.shape
    nc = t // CHUNK

    def split(a):
        return jnp.moveaxis(a.reshape(b, h, nc, CHUNK, a.shape[-1]), 2, 0)

    S, o = lax.scan(_gla_chunk_step, S0, (split(q), split(k), split(v), split(g)))
    o = jnp.moveaxis(o, 0, 2).reshape(b, h, t, -1)
    return o, S


def _hgrn2_mixer(h, S0, lb, w_in, g_o, w_out):
    b, t, _ = h.shape
    dk = HG_HEADS * HG_DK
    dv = HG_HEADS * HG_DV
    z = h @ w_in
    zq, zf = z[..., :dk], z[..., dk:2 * dk].astype(jnp.float32)
    zi, zg = z[..., 2 * dk:2 * dk + dv], z[..., 2 * dk + dv:]
    g = jnp.log(lb + (1.0 - lb) * jax.nn.sigmoid(zf))
    k = (1.0 - lb) * jax.nn.sigmoid(-zf)
    q = jax.nn.silu(zq.astype(jnp.float32))
    v = zi.astype(jnp.float32)
    pad = (-t) % CHUNK

    def heads(a):
        a = a.reshape(b, t, HG_HEADS, -1).transpose(0, 2, 1, 3)
        return jnp.pad(a, ((0, 0), (0, 0), (0, pad), (0, 0)))

    o, S = _gla_scan(heads(q), heads(k), heads(v), heads(g), S0)
    o = o[:, :, :t].transpose(0, 2, 1, 3)
    o = _rmsnorm(o, g_o) * jax.nn.silu(zg.astype(jnp.float32).reshape(b, t, HG_HEADS, HG_DV))
    y = o.reshape(b, t, dv).astype(h.dtype) @ w_out
    return y, S


def _run_group(x, pos, conv_prev, ckv_prev, kpe_prev, hgrn_prev, p):
    b = x.shape[0]
    lb_sm = jax.nn.softmax(p['lb_logits'].astype(jnp.float32), axis=0)
    lb_all = jnp.cumsum(lb_sm, axis=0) - lb_sm[0]
    conv_new, ckv_new, kpe_new, hgrn_new = [], [], [], []
    for l in range(DEPTH):
        x = x + 0.5 * _swiglu(_rmsnorm(x, p['norm_ffn1'][l]), p['w_ffn1_gate'][l], p['w_ffn1_up'][l], p['w_ffn1_down'][l])
        hn = _rmsnorm(x, p['norm_mix'][l])
        if l % 2 == 0:
            e = l // 2
            y, cs, ckv, kpe = _conv_mla_mixer(
                hn, pos,
                None if conv_prev is None else conv_prev[e],
                None if ckv_prev is None else ckv_prev[e],
                None if kpe_prev is None else kpe_prev[e],
                p['w_in_a'][e], p['w_conv'][e], p['g_q'][e], p['w_uq'][e],
                p['g_kv'][e], p['w_ukv'][e], p['w_out_a'][e])
            conv_new.append(cs)
            ckv_new.append(ckv)
            kpe_new.append(kpe)
        else:
            o = l // 2
            if hgrn_prev is None:
                S0 = jnp.zeros((b, HG_HEADS, HG_DK, HG_DV), jnp.float32)
            else:
                S0 = hgrn_prev[o].astype(jnp.float32)
            y, S = _hgrn2_mixer(hn, S0, lb_all[o], p['w_in_c'][o], p['g_o'][o], p['w_out_c'][o])
            hgrn_new.append(S.astype(x.dtype))
        x = x + y
        x = x + 0.5 * _swiglu(_rmsnorm(x, p['norm_ffn2'][l]), p['w_ffn2_gate'][l], p['w_ffn2_up'][l], p['w_ffn2_down'][l])
    y_out = _rmsnorm(x, p['norm_final'])
    return y_out, jnp.stack(conv_new), jnp.stack(ckv_new), jnp.stack(kpe_new), jnp.stack(hgrn_new)


def setup_inputs(seed: int = 0) -> dict:
    key = jax.random.key(seed)
    it = iter(jax.random.split(key, 40))
    f32 = jnp.float32

    def nrm(shape, fan_in):
        return jax.random.normal(next(it), shape, f32) * (fan_in ** -0.5)

    def gain(shape):
        return 1.0 + 0.02 * jax.random.normal(next(it), shape, f32)

    return {
        'x_prompt': jax.random.normal(next(it), (BATCH, SEQ, D_MODEL), f32),
        'x_sample': jax.random.normal(next(it), (DEC_BATCH, DEC_SEQ, D_MODEL), f32),
        'cache_conv': jax.random.normal(next(it), (N_EVEN, DEC_BATCH, CONV_W - 1, D_CONV), f32),
        'cache_ckv': jax.random.normal(next(it), (N_EVEN, DEC_BATCH, PAST_LEN, KV_LORA), f32),
        'cache_kpe': jax.random.normal(next(it), (N_EVEN, DEC_BATCH, PAST_LEN, ROPE_DIM), f32),
        'state_hgrn': 0.5 * jax.random.normal(next(it), (N_ODD, DEC_BATCH, HG_HEADS, HG_DK, HG_DV), f32),
        'norm_ffn1': gain((DEPTH, D_MODEL)),
        'w_ffn1_gate': nrm((DEPTH, D_MODEL, D_FF), D_MODEL),
        'w_ffn1_up': nrm((DEPTH, D_MODEL, D_FF), D_MODEL),
        'w_ffn1_down': nrm((DEPTH, D_FF, D_MODEL), D_FF),
        'norm_mix': gain((DEPTH, D_MODEL)),
        'w_in_a': nrm((N_EVEN, D_MODEL, D_IN_A), D_MODEL),
        'w_conv': nrm((N_EVEN, CONV_W, D_CONV), CONV_W),
        'g_q': gain((N_EVEN, Q_LORA)),
        'w_uq': nrm((N_EVEN, Q_LORA, MLA_HEADS * (NOPE_DIM + ROPE_DIM)), Q_LORA),
        'g_kv': gain((N_EVEN, KV_LORA)),
        'w_ukv': nrm((N_EVEN, KV_LORA, MLA_HEADS * (NOPE_DIM + V_DIM)), KV_LORA),
        'w_out_a': nrm((N_EVEN, D_CAT_A, D_MODEL), D_CAT_A),
        'w_in_c': nrm((N_ODD, D_MODEL, D_IN_C), D_MODEL),
        'lb_logits': 0.1 * jax.random.normal(next(it), (N_ODD, HG_HEADS * HG_DK), f32),
        'g_o': gain((N_ODD, HG_DV)),
        'w_out_c': nrm((N_ODD, HG_HEADS * HG_DV, D_MODEL), HG_HEADS * HG_DV),
        'norm_ffn2': gain((DEPTH, D_MODEL)),
        'w_ffn2_gate': nrm((DEPTH, D_MODEL, D_FF), D_MODEL),
        'w_ffn2_up': nrm((DEPTH, D_MODEL, D_FF), D_MODEL),
        'w_ffn2_down': nrm((DEPTH, D_FF, D_MODEL), D_FF),
        'norm_final': gain((D_MODEL,)),
    }


def reference(x_prompt, x_sample, cache_conv, cache_ckv, cache_kpe, state_hgrn,
              norm_ffn1, w_ffn1_gate, w_ffn1_up, w_ffn1_down, norm_mix,
              w_in_a, w_conv, g_q, w_uq, g_kv, w_ukv, w_out_a,
              w_in_c, lb_logits, g_o, w_out_c,
              norm_ffn2, w_ffn2_gate, w_ffn2_up, w_ffn2_down, norm_final):
    params = {
        'norm_ffn1': norm_ffn1, 'w_ffn1_gate': w_ffn1_gate, 'w_ffn1_up': w_ffn1_up, 'w_ffn1_down': w_ffn1_down,
        'norm_mix': norm_mix,
        'w_in_a': w_in_a, 'w_conv': w_conv, 'g_q': g_q, 'w_uq': w_uq, 'g_kv': g_kv, 'w_ukv': w_ukv, 'w_out_a': w_out_a,
        'w_in_c': w_in_c, 'lb_logits': lb_logits, 'g_o': g_o, 'w_out_c': w_out_c,
        'norm_ffn2': norm_ffn2, 'w_ffn2_gate': w_ffn2_gate, 'w_ffn2_up': w_ffn2_up, 'w_ffn2_down': w_ffn2_down,
        'norm_final': norm_final,
    }
    pos_p = jnp.arange(x_prompt.shape[1], dtype=jnp.int32)
    pos_s = cache_ckv.shape[2] + jnp.arange(x_sample.shape[1], dtype=jnp.int32)
    y_prompt, conv_p, ckv_p, kpe_p, hgrn_p = _run_group(x_prompt, pos_p, None, None, None, None, params)
    y_sample, conv_s, ckv_s, kpe_s, hgrn_s = _run_group(x_sample, pos_s, cache_conv, cache_ckv, cache_kpe, state_hgrn, params)
    return (y_prompt, y_sample, conv_p, ckv_p, kpe_p, hgrn_p, conv_s, ckv_s, kpe_s, hgrn_s)
```

```python
import functools

import numpy as np
import jax
import jax.numpy as jnp
from jax import lax
from jax.experimental import pallas as pl
from jax.experimental.pallas import tpu as pltpu

F32 = jnp.float32
BF16 = jnp.bfloat16

EPS = 1e-6
CHUNK = 64
CONV_W = 3
MLA_HEADS = 8
NOPE_DIM = 128
ROPE_DIM = 64
V_DIM = 128
ROPE_THETA = 10000.0
MLA_SCALE = (NOPE_DIM + ROPE_DIM) ** -0.5
HG_HEADS = 16
HG_DK = 128
HG_DV = 128
LANES = 128
SUBLANES = 8
NEG_BIG = -1e30
LEVELS = (32, 16, 8, 4, 2, 1)
VMEM_LIMIT_BYTES = 56 * 1024 * 1024


def _tile(n, pref):
    if n <= pref:
        return n
    for t in range(pref, 7, -8):
        if n % t == 0:
            return t
    return n


def _col_tile(n, pref):
    for t in range(min(pref, n) // LANES * LANES, 0, -LANES):
        if n % t == 0:
            return t
    return n


def _params(*sem):
    return pltpu.CompilerParams(dimension_semantics=sem, vmem_limit_bytes=VMEM_LIMIT_BYTES)


def _sigmoid(x):
    return 1.0 / (1.0 + jnp.exp(-x))


def _rms(x, g):
    return x * lax.rsqrt(jnp.mean(x * x, axis=-1, keepdims=True) + EPS) * g


def _dot(a, b):
    return jnp.dot(a, b, preferred_element_type=F32)


def _dot_nt(a, b):
    return lax.dot_general(a, b, (((1,), (1,)), ((), ())), preferred_element_type=F32)


def _dot_tn(a, b):
    return lax.dot_general(a, b, (((0,), (0,)), ((), ())), preferred_element_type=F32)


def _ffn_kernel(x_ref, g_ref, wg_ref, wu_ref, wd_ref, *rest, final_norm):
    if final_norm:
        gf_ref, o_ref, n_ref, acc_ref = rest
    else:
        o_ref, n_ref, acc_ref = rest
    j = pl.program_id(1)

    @pl.when(j == 0)
    def _():
        n_ref[...] = _rms(x_ref[...], g_ref[...]).astype(BF16)
        acc_ref[...] = jnp.zeros_like(acc_ref)

    n = n_ref[...]
    hg = _dot(n, wg_ref[...])
    hu = _dot(n, wu_ref[...])
    a = (hg * _sigmoid(hg)) * hu
    acc_ref[...] += _dot(a.astype(BF16), wd_ref[...])

    @pl.when(j == pl.num_programs(1) - 1)
    def _():
        r = x_ref[...] + 0.5 * acc_ref[...]
        if final_norm:
            r = _rms(r, gf_ref[...])
        o_ref[...] = r


def _ffn(x, g, wg, wu, wd, g_final=None):
    n, d = x.shape
    f = wg.shape[1]
    tm = _tile(n, 512)
    tf = _col_tile(f, 512)
    in_specs = [
        pl.BlockSpec((tm, d), lambda i, j: (i, 0)),
        pl.BlockSpec((1, d), lambda i, j: (0, 0)),
        pl.BlockSpec((d, tf), lambda i, j: (0, j)),
        pl.BlockSpec((d, tf), lambda i, j: (0, j)),
        pl.BlockSpec((tf, d), lambda i, j: (j, 0)),
    ]
    args = [x, g.reshape(1, d), wg, wu, wd]
    if g_final is not None:
        in_specs.append(pl.BlockSpec((1, d), lambda i, j: (0, 0)))
        args.append(g_final.reshape(1, d))
    return pl.pallas_call(
        functools.partial(_ffn_kernel, final_norm=g_final is not None),
        grid=(n // tm, f // tf),
        in_specs=in_specs,
        out_specs=pl.BlockSpec((tm, d), lambda i, j: (i, 0)),
        out_shape=jax.ShapeDtypeStruct((n, d), F32),
        scratch_shapes=[pltpu.VMEM((tm, d), BF16), pltpu.VMEM((tm, d), F32)],
        compiler_params=_params("parallel", "arbitrary"),
        name="ffn",
    )(*args)


def _nmm_kernel(x_ref, g_ref, w_ref, o_ref, n_ref):
    @pl.when(pl.program_id(1) == 0)
    def _():
        n_ref[...] = _rms(x_ref[...], g_ref[...]).astype(BF16)

    o_ref[...] = _dot(n_ref[...], w_ref[...])


def _norm_matmul(x, g, w):
    n, d = x.shape
    c = w.shape[1]
    tm = _tile(n, 512)
    tn = _col_tile(c, 1536)
    return pl.pallas_call(
        _nmm_kernel,
        grid=(n // tm, c // tn),
        in_specs=[
            pl.BlockSpec((tm, d), lambda i, j: (i, 0)),
            pl.BlockSpec((1, d), lambda i, j: (0, 0)),
            pl.BlockSpec((d, tn), lambda i, j: (0, j)),
        ],
        out_specs=pl.BlockSpec((tm, tn), lambda i, j: (i, j)),
        out_shape=jax.ShapeDtypeStruct((n, c), F32),
        scratch_shapes=[pltpu.VMEM((tm, d), BF16)],
        compiler_params=_params("parallel", "arbitrary"),
        name="norm_matmul",
    )(x, g.reshape(1, d), w)


def _mmres_kernel(*refs, n_in):
    x_ref, o_ref = refs[2 * n_in], refs[2 * n_in + 1]
    acc = x_ref[...]
    for a_ref, w_ref in zip(refs[:n_in], refs[n_in:2 * n_in]):
        acc = acc + _dot(a_ref[...], w_ref[...])
    o_ref[...] = acc


def _matmul_residual(acts, weights, x):
    n, d = x.shape
    tm = _tile(n, 512)
    in_specs = [pl.BlockSpec((tm, a.shape[1]), lambda i: (i, 0)) for a in acts]
    in_specs += [pl.BlockSpec(w.shape, lambda i: (0, 0)) for w in weights]
    in_specs.append(pl.BlockSpec((tm, d), lambda i: (i, 0)))
    return pl.pallas_call(
        functools.partial(_mmres_kernel, n_in=len(acts)),
        grid=(n // tm,),
        in_specs=in_specs,
        out_specs=pl.BlockSpec((tm, d), lambda i: (i, 0)),
        out_shape=jax.ShapeDtypeStruct((n, d), F32),
        compiler_params=_params("parallel"),
        name="matmul_residual",
    )(*acts, *weights, x)


def _evenpost_kernel(z_ref, zp_ref, cp_ref, wc_ref, gq_ref, wqn_ref, wqp_ref, wqs_ref, wuk_ref, gkv_ref,
                     c1_ref, s1_ref, c8_ref, s8_ref,
                     ya_ref, cs_ref, q_ref, ckv_ref, kpe_ref, kv_ref, *, dc, ql, kl):
    t = pl.program_id(1)
    tt = z_ref.shape[1]
    o3 = 3 * dc
    gb = z_ref[0, :, 0:dc]
    v = z_ref[0, :, dc:2 * dc] * z_ref[0, :, 2 * dc:o3]
    pv = zp_ref[0, :, dc:2 * dc] * zp_ref[0, :, 2 * dc:o3]
    cp = cp_ref[0]
    first = t == 0
    pm2 = jnp.where(first, cp[0:1], pv[SUBLANES - 2:SUBLANES - 1])
    pm1 = jnp.where(first, cp[1:2], pv[SUBLANES - 1:SUBLANES])
    rows = lax.broadcasted_iota(jnp.int32, v.shape, 0)
    v1 = jnp.where(rows == 0, pm1, pltpu.roll(v, 1, 0))
    v2 = jnp.where(rows == 0, pm2, jnp.where(rows == 1, pm1, pltpu.roll(v, 2, 0)))
    w = wc_ref[...]
    conv = v2 * w[0:1] + v1 * w[1:2] + v * w[2:3]
    ya_ref[0] = (gb * conv).astype(BF16)

    @pl.when(t == pl.num_programs(1) - 1)
    def _():
        cs_ref[0] = v[tt - (CONV_W - 1):tt]

    cqn = _rms(z_ref[0, :, o3:o3 + ql], gq_ref[...]).astype(BF16)
    qn = _dot(cqn, wqn_ref[...])
    qpr = _dot(cqn, wqp_ref[...]) * c8_ref[...] + _dot(cqn, wqs_ref[...]) * s8_ref[...]
    for h in range(MLA_HEADS):
        qlat = _dot(qn[:, h * NOPE_DIM:(h + 1) * NOPE_DIM].astype(BF16), wuk_ref[h])
        q_ref[0, h, :, 0:kl] = qlat.astype(BF16)
        q_ref[0, h, :, kl:kl + ROPE_DIM] = qpr[:, h * ROPE_DIM:(h + 1) * ROPE_DIM].astype(BF16)
    o4 = o3 + ql
    o5 = o4 + kl
    cn = _rms(z_ref[0, :, o4:o5], gkv_ref[...])
    kr = z_ref[0, :, o5:o5 + ROPE_DIM] * c1_ref[...] + z_ref[0, :, o5 + ROPE_DIM:o5 + 2 * ROPE_DIM] * s1_ref[...]
    ckv_ref[0] = cn
    kpe_ref[0] = kr
    kv_ref[0, :, 0:kl] = cn.astype(BF16)
    kv_ref[0, :, kl:kl + ROPE_DIM] = kr.astype(BF16)


def _evenpost(z, conv_prev, w_conv, g_q, wqn, wqp, wqs, wuk, g_kv, c1, s1, c8, s8):
    b, t, zw = z.shape
    dc = w_conv.shape[1]
    ql = g_q.shape[0]
    kl = g_kv.shape[0]
    tt = _tile(t, 256)
    hr = MLA_HEADS * ROPE_DIM
    full = lambda a: pl.BlockSpec(a.shape, lambda i, j: (0,) * a.ndim)
    wts = [w_conv, g_q.reshape(1, ql), wqn, wqp, wqs, wuk, g_kv.reshape(1, kl)]
    in_specs = [
        pl.BlockSpec((1, tt, zw), lambda i, j: (i, j, 0)),
        pl.BlockSpec((1, SUBLANES, zw), lambda i, j: (i, jnp.maximum(j * (tt // SUBLANES) - 1, 0), 0)),
        pl.BlockSpec((1, CONV_W - 1, dc), lambda i, j: (i, 0, 0)),
    ] + [full(a) for a in wts] + [
        pl.BlockSpec((tt, ROPE_DIM), lambda i, j: (j, 0)),
        pl.BlockSpec((tt, ROPE_DIM), lambda i, j: (j, 0)),
        pl.BlockSpec((tt, hr), lambda i, j: (j, 0)),
        pl.BlockSpec((tt, hr), lambda i, j: (j, 0)),
    ]
    out_shape = [
        jax.ShapeDtypeStruct((b, t, dc), BF16),
        jax.ShapeDtypeStruct((b, CONV_W - 1, dc), F32),
        jax.ShapeDtypeStruct((b, MLA_HEADS, t, kl + ROPE_DIM), BF16),
        jax.ShapeDtypeStruct((b, t, kl), F32),
        jax.ShapeDtypeStruct((b, t, ROPE_DIM), F32),
        jax.ShapeDtypeStruct((b, t, kl + ROPE_DIM), BF16),
    ]
    out_specs = [
        pl.BlockSpec((1, tt, dc), lambda i, j: (i, j, 0)),
        pl.BlockSpec((1, CONV_W - 1, dc), lambda i, j: (i, 0, 0)),
        pl.BlockSpec((1, MLA_HEADS, tt, kl + ROPE_DIM), lambda i, j: (i, 0, j, 0)),
        pl.BlockSpec((1, tt, kl), lambda i, j: (i, j, 0)),
        pl.BlockSpec((1, tt, ROPE_DIM), lambda i, j: (i, j, 0)),
        pl.BlockSpec((1, tt, kl + ROPE_DIM), lambda i, j: (i, j, 0)),
    ]
    return pl.pallas_call(
        functools.partial(_evenpost_kernel, dc=dc, ql=ql, kl=kl),
        grid=(b, t // tt),
        in_specs=in_specs,
        out_specs=out_specs,
        out_shape=out_shape,
        compiler_params=_params("parallel", "arbitrary"),
        name="conv_mla_prep",
    )(z, z, conv_prev, *wts, c1, s1, c8, s8)


def _attn_kernel(q_ref, kv_ref, wuv_ref, o_ref, m_ref, l_ref, acc_ref, *, tq, tk, causal, kv_len, kl):
    i = pl.program_id(1)
    rows = MLA_HEADS * tq
    q = q_ref[0].reshape(rows, q_ref.shape[3])
    m_ref[...] = jnp.full(m_ref.shape, NEG_BIG, F32)
    l_ref[...] = jnp.zeros_like(l_ref)
    acc_ref[...] = jnp.zeros_like(acc_ref)
    if causal:
        n_kv = ((i + 1) * tq + tk - 1) // tk
        tok = i * tq + lax.rem(lax.broadcasted_iota(jnp.int32, (rows, 1), 0), tq)
        limit = (tok // CHUNK + 1) * CHUNK
    else:
        n_kv = (kv_len + tk - 1) // tk
        limit = kv_len

    def body(j, carry):
        k = kv_ref[0, pl.ds(pl.multiple_of(j * tk, tk), tk), :]
        s = _dot_nt(q, k) * MLA_SCALE
        kpos = j * tk + lax.broadcasted_iota(jnp.int32, (1, tk), 1)
        s = jnp.where(kpos < limit, s, NEG_BIG)
        m_prev = m_ref[...]
        m_new = jnp.maximum(m_prev, jnp.max(s, axis=-1, keepdims=True))
        alpha = jnp.exp(m_prev - m_new)
        p = jnp.exp(s - m_new)
        l_ref[...] = alpha * l_ref[...] + jnp.sum(p, axis=-1, keepdims=True)
        acc_ref[...] = alpha * acc_ref[...] + _dot(p.astype(BF16), k[:, 0:kl])
        m_ref[...] = m_new
        return carry

    lax.fori_loop(0, n_kv, body, 0)
    o = (acc_ref[...] / l_ref[...]).astype(BF16)
    for h in range(MLA_HEADS):
        o_ref[0, :, h * V_DIM:(h + 1) * V_DIM] = _dot(o[h * tq:(h + 1) * tq], wuv_ref[h]).astype(BF16)


def _attention(q, kv, wuv, *, causal, kv_len):
    b, h, t, dq = q.shape
    tkv = kv.shape[1]
    kl = wuv.shape[1]
    tq = _tile(t, 128)
    tk = _tile(tkv, 512)
    rows = h * tq
    return pl.pallas_call(
        functools.partial(_attn_kernel, tq=tq, tk=tk, causal=causal, kv_len=kv_len, kl=kl),
        grid=(b, t // tq),
        in_specs=[
            pl.BlockSpec((1, h, tq, dq), lambda i, j: (i, 0, j, 0)),
            pl.BlockSpec((1, tkv, dq), lambda i, j: (i, 0, 0)),
            pl.BlockSpec(wuv.shape, lambda i, j: (0, 0, 0)),
        ],
        out_specs=pl.BlockSpec((1, tq, h * V_DIM), lambda i, j: (i, j, 0)),
        out_shape=jax.ShapeDtypeStruct((b, t, h * V_DIM), BF16),
        scratch_shapes=[pltpu.VMEM((rows, 1), F32), pltpu.VMEM((rows, 1), F32), pltpu.VMEM((rows, kl), F32)],
        compiler_params=_params("parallel", "arbitrary"),
        name="mla_attention",
    )(q, kv, wuv)


def _level_tables():
    c = CHUNK
    t = np.arange(c)[:, None]
    s = np.arange(c)[None, :]
    blocks = [(s <= t), (s > t)]
    masks = []
    for m in LEVELS:
        r = (t // (2 * m)) * (2 * m) + m - 1
        second = (t & m) != 0
        blocks.append(np.where(second, (s > r) & (s <= t), (s > t) & (s <= r)))
        masks.append((t // (2 * m) == s // (2 * m)) & second & ((s & m) == 0))
    d = np.concatenate(blocks, axis=0).astype(np.float32)
    return d, np.stack(masks).astype(np.float32)


def _gla_kernel(zq_ref, zf_ref, zi_ref, zg_ref, lbl_ref, go_ref, s0_ref, d_ref, mk_ref,
                o_ref, st_ref, lb_ref, *, layer):
    t = pl.program_id(1)
    tc = zq_ref.shape[1]

    @pl.when(t == 0)
    def _():
        st_ref[...] = s0_ref[...]

    ll = lbl_ref[...]
    e = jnp.exp(ll - jnp.max(ll, axis=0, keepdims=True))
    sm = e / jnp.sum(e, axis=0, keepdims=True)
    lb = jnp.zeros((1, ll.shape[1]), F32)
    for r in range(1, layer + 1):
        lb = lb + sm[r:r + 1]
    lb_ref[...] = lb

    go = go_ref[...]
    dmat = d_ref[...]
    row = lax.broadcasted_iota(jnp.int32, (CHUNK, 1), 0)

    def head_body(h, carry):
        off = pl.multiple_of(h * HG_DK, HG_DK)
        lbh = lb_ref[:, pl.ds(off, HG_DK)]

        def chunk_body(c, carry2):
            r0 = pl.multiple_of(c * CHUNK, CHUNK)
            zq = zq_ref[0, pl.ds(r0, CHUNK), pl.ds(off, HG_DK)]
            zf = zf_ref[0, pl.ds(r0, CHUNK), pl.ds(off, HG_DK)]
            v = zi_ref[0, pl.ds(r0, CHUNK), pl.ds(off, HG_DV)]
            zg = zg_ref[0, pl.ds(r0, CHUNK), pl.ds(off, HG_DV)]
            g = jnp.log(lbh + (1.0 - lbh) * _sigmoid(zf))
            k = (1.0 - lbh) * _sigmoid(-zf)
            q = zq * _sigmoid(zq)
            g1 = g.astype(BF16)
            r1 = g - g1.astype(F32)
            g2 = r1.astype(BF16)
            g3 = (r1 - g2.astype(F32)).astype(BF16)
            ex = jnp.exp(_dot(dmat, g1) + _dot(dmat, g2) + _dot(dmat, g3))
            e_q = ex[0:CHUNK]
            e_k = ex[CHUNK:2 * CHUNK]
            st = st_ref[0, h]
            vb = v.astype(BF16)
            o = _dot_nt((q * e_q).astype(BF16), st.astype(BF16))
            o = o + jnp.sum(q * k, axis=-1, keepdims=True) * v
            p = jnp.zeros((CHUNK, CHUNK), F32)
            for li, m in enumerate(LEVELS):
                el = ex[(2 + li) * CHUNK:(3 + li) * CHUNK]
                second = (row & m) != 0
                qm = jnp.where(second, q * el, 0.0).astype(BF16)
                km = jnp.where(second, 0.0, k * el).astype(BF16)
                p = p + _dot_nt(qm, km) * mk_ref[li]
            o = o + _dot(p.astype(BF16), vb)
            st_ref[0, h] = st * e_q[CHUNK - 1:CHUNK] + _dot_tn(vb, (k * e_k).astype(BF16))
            og = _rms(o, go) * (zg * _sigmoid(zg))
            o_ref[0, pl.ds(r0, CHUNK), pl.ds(off, HG_DV)] = og.astype(BF16)
            return carry2

        return lax.fori_loop(0, tc // CHUNK, chunk_body, carry)

    lax.fori_loop(0, HG_HEADS, head_body, 0)


def _gla(z, lb_logits, layer, g_o, s0t):
    b, t, zw = z.shape
    dh = zw // 4
    tc = _tile(t, 256)
    dmat, masks = _level_tables()
    dmat = jnp.asarray(dmat, BF16)
    masks = jnp.asarray(masks, F32)
    zspec = lambda c: pl.BlockSpec((1, tc, dh), lambda i, j, c=c: (i, j, c))
    sspec = pl.BlockSpec((1,) + s0t.shape[1:], lambda i, j: (i, 0, 0, 0))
    return pl.pallas_call(
        functools.partial(_gla_kernel, layer=layer),
        grid=(b, t // tc),
        in_specs=[zspec(0), zspec(1), zspec(2), zspec(3),
                  pl.BlockSpec(lb_logits.shape, lambda i, j: (0, 0)),
                  pl.BlockSpec((1, HG_DV), lambda i, j: (0, 0)),
                  sspec,
                  pl.BlockSpec(dmat.shape, lambda i, j: (0, 0)),
                  pl.BlockSpec(masks.shape, lambda i, j: (0, 0, 0))],
        out_specs=[pl.BlockSpec((1, tc, dh), lambda i, j: (i, j, 0)), sspec],
        out_shape=[jax.ShapeDtypeStruct((b, t, dh), BF16), jax.ShapeDtypeStruct(s0t.shape, F32)],
        scratch_shapes=[pltpu.VMEM((1, dh), F32)],
        compiler_params=_params("parallel", "arbitrary"),
        name="hgrn2_recurrence",
    )(z, z, z, z, lb_logits, g_o.reshape(1, HG_DV), s0t, dmat, masks)


def _rope_tables(pos):
    half = ROPE_DIM // 2
    inv = ROPE_THETA ** (-jnp.arange(half, dtype=F32) / half)
    ang = pos.astype(F32)[:, None] * inv[None, :]
    cos, sin = jnp.cos(ang), jnp.sin(ang)
    c1 = jnp.concatenate([cos, cos], axis=1)
    s1 = jnp.concatenate([-sin, sin], axis=1)
    return c1, s1, jnp.tile(c1, (1, MLA_HEADS)), jnp.tile(s1, (1, MLA_HEADS))


def _swap_halves(w):
    k, c = w.shape
    w = w.reshape(k, c // ROPE_DIM, 2, ROPE_DIM // 2)
    return w[:, :, ::-1, :].reshape(k, c)


def _prep_even(w_in_a, w_uq, w_ukv, w_out_a, dc):
    d = w_in_a.shape[0]
    kpe = w_in_a[:, -ROPE_DIM:]
    w_in = jnp.concatenate([w_in_a, _swap_halves(kpe)], axis=1).astype(BF16)
    ql = w_uq.shape[0]
    wq = w_uq.reshape(ql, MLA_HEADS, NOPE_DIM + ROPE_DIM)
    wqn = wq[:, :, :NOPE_DIM].reshape(ql, MLA_HEADS * NOPE_DIM).astype(BF16)
    wqp = wq[:, :, NOPE_DIM:].reshape(ql, MLA_HEADS * ROPE_DIM)
    wqs = _swap_halves(wqp).astype(BF16)
    kl = w_ukv.shape[0]
    wkv = w_ukv.reshape(kl, MLA_HEADS, NOPE_DIM + V_DIM)
    wuk = jnp.transpose(wkv[:, :, :NOPE_DIM], (1, 2, 0)).astype(BF16)
    wuv = jnp.transpose(wkv[:, :, NOPE_DIM:], (1, 0, 2)).astype(BF16)
    wo = w_out_a.astype(BF16)
    return dict(w_in=w_in, wqn=wqn, wqp=wqp.astype(BF16), wqs=wqs, wuk=wuk, wuv=wuv, wo_a=wo[:dc], wo_b=wo[dc:])


def _run_group(x3, pos, conv_prev, ckv_prev, kpe_prev, hgrn_prev, p):
    b, t, d = x3.shape
    n = b * t
    depth = p["norm_ffn1"].shape[0]
    x = x3.reshape(n, d)
    c1, s1, c8, s8 = _rope_tables(pos)
    conv_new, ckv_new, kpe_new, hgrn_new = [], [], [], []
    for l in range(depth):
        x = _ffn(x, p["norm_ffn1"][l], p["wg1"][l], p["wu1"][l], p["wd1"][l])
        if l % 2 == 0:
            e = l // 2
            pe = p["even"][e]
            dc = p["w_conv"].shape[2]
            z = _norm_matmul(x, p["norm_mix"][l], pe["w_in"]).reshape(b, t, -1)
            cprev = jnp.zeros((b, CONV_W - 1, dc), F32) if conv_prev is None else conv_prev[e]
            ya, cs, q, ckv, kpe, kv = _evenpost(z, cprev, p["w_conv"][e], p["g_q"][e], pe["wqn"], pe["wqp"],
                                                pe["wqs"], pe["wuk"], p["g_kv"][e], c1, s1, c8, s8)
            if ckv_prev is None:
                yb = _attention(q, kv, pe["wuv"], causal=True, kv_len=t)
            else:
                past = ckv_prev.shape[2]
                kv_len = past + t
                tk = 512
                pad = (-kv_len) % tk
                cache = jnp.concatenate([ckv_prev[e], kpe_prev[e]], axis=-1).astype(BF16)
                kv_all = jnp.concatenate([cache, kv, jnp.zeros((b, pad, kv.shape[2]), BF16)], axis=1)
                yb = _attention(q, kv_all, pe["wuv"], causal=False, kv_len=kv_len)
            x = _matmul_residual([ya.reshape(n, -1), yb.reshape(n, -1)], [pe["wo_a"], pe["wo_b"]], x)
            conv_new.append(cs)
            ckv_new.append(ckv)
            kpe_new.append(kpe)
        else:
            o = l // 2
            z = _norm_matmul(x, p["norm_mix"][l], p["w_in_c"][o]).reshape(b, t, -1)
            if hgrn_prev is None:
                s0t = jnp.zeros((b, HG_HEADS, HG_DV, HG_DK), F32)
            else:
                s0t = jnp.swapaxes(hgrn_prev[o], -1, -2)
            og, st = _gla(z, p["lb_logits"], o, p["g_o"][o], s0t)
            x = _matmul_residual([og.reshape(n, -1)], [p["w_out_c"][o]], x)
            hgrn_new.append(jnp.swapaxes(st, -1, -2))
        g_final = p["norm_final"] if l == depth - 1 else None
        x = _ffn(x, p["norm_ffn2"][l], p["wg2"][l], p["wu2"][l], p["wd2"][l], g_final)
    return x.reshape(b, t, d), jnp.stack(conv_new), jnp.stack(ckv_new), jnp.stack(kpe_new), jnp.stack(hgrn_new)


def kernel(x_prompt, x_sample, cache_conv, cache_ckv, cache_kpe, state_hgrn, norm_ffn1, w_ffn1_gate, w_ffn1_up, w_ffn1_down, norm_mix, w_in_a, w_conv, g_q, w_uq, g_kv, w_ukv, w_out_a, w_in_c, lb_logits, g_o, w_out_c, norm_ffn2, w_ffn2_gate, w_ffn2_up, w_ffn2_down, norm_final):
    dc = w_conv.shape[2]
    p = dict(
        norm_ffn1=norm_ffn1, wg1=w_ffn1_gate.astype(BF16), wu1=w_ffn1_up.astype(BF16), wd1=w_ffn1_down.astype(BF16),
        norm_ffn2=norm_ffn2, wg2=w_ffn2_gate.astype(BF16), wu2=w_ffn2_up.astype(BF16), wd2=w_ffn2_down.astype(BF16),
        norm_mix=norm_mix, w_conv=w_conv, g_q=g_q, g_kv=g_kv,
        even=[_prep_even(w_in_a[e], w_uq[e], w_ukv[e], w_out_a[e], dc) for e in range(w_in_a.shape[0])],
        w_in_c=w_in_c.astype(BF16), lb_logits=lb_logits, g_o=g_o, w_out_c=w_out_c.astype(BF16),
        norm_final=norm_final,
    )
    pos_p = jnp.arange(x_prompt.shape[1], dtype=jnp.int32)
    pos_s = cache_ckv.shape[2] + jnp.arange(x_sample.shape[1], dtype=jnp.int32)
    y_p, conv_p, ckv_p, kpe_p, hgrn_p = _run_group(x_prompt, pos_p, None, None, None, None, p)
    y_s, conv_s, ckv_s, kpe_s, hgrn_s = _run_group(x_sample, pos_s, cache_conv, cache_ckv, cache_kpe, state_hgrn, p)
    return (y_p, y_s, conv_p, ckv_p, kpe_p, hgrn_p, conv_s, ckv_s, kpe_s, hgrn_s)
```

```python
import functools

import numpy as np
import jax
import jax.numpy as jnp
from jax import lax
from jax.experimental import pallas as pl
from jax.experimental.pallas import tpu as pltpu

F32 = jnp.float32
BF16 = jnp.bfloat16

EPS = 1e-6
CHUNK = 64
CONV_W = 3
MLA_HEADS = 8
NOPE_DIM = 128
ROPE_DIM = 64
V_DIM = 128
ROPE_THETA = 10000.0
MLA_SCALE = (NOPE_DIM + ROPE_DIM) ** -0.5
HG_HEADS = 16
HG_DK = 128
HG_DV = 128
LANES = 128
SUBLANES = 8
NEG_BIG = -1e30
LOG2E = 1.4426950408889634
LEVELS = (32, 16, 8, 4, 2, 1)
VMEM_LIMIT_BYTES = 60 * 1024 * 1024


def _tile(n, pref):
    if n <= pref:
        return n
    for t in range(pref, 7, -8):
        if n % t == 0:
            return t
    return n


def _col_tile(n, pref):
    for t in range(min(pref, n) // LANES * LANES, 0, -LANES):
        if n % t == 0:
            return t
    return n


def _params(*sem):
    return pltpu.CompilerParams(dimension_semantics=sem, vmem_limit_bytes=VMEM_LIMIT_BYTES)


def _sigmoid(x):
    return 1.0 / (1.0 + jnp.exp(-x))


def _silu(x):
    h = 0.5 * x
    return h + h * jnp.tanh(h)


def _rms(x, g):
    return x * lax.rsqrt(jnp.mean(x * x, axis=-1, keepdims=True) + EPS) * g


def _dot(a, b):
    return jnp.dot(a, b, preferred_element_type=F32)


def _dot_nt(a, b):
    return lax.dot_general(a, b, (((1,), (1,)), ((), ())), preferred_element_type=F32)


def _dot_tn(a, b):
    return lax.dot_general(a, b, (((0,), (0,)), ((), ())), preferred_element_type=F32)


def _ffn_kernel(x_ref, g_ref, wg_ref, wu_ref, wd_ref, *rest, final_norm):
    if final_norm:
        gf_ref, o_ref, n_ref = rest
    else:
        o_ref, n_ref = rest
    j = pl.program_id(1)
    tm = x_ref.shape[0]
    rc = min(tm, 256)

    @pl.when(j == 0)
    def _():
        for r in range(0, tm, rc):
            n_ref[r:r + rc, :] = _rms(x_ref[r:r + rc, :], g_ref[...]).astype(BF16)
        o_ref[...] = jnp.zeros_like(o_ref)

    n = n_ref[...]
    hg = _dot(n, wg_ref[...])
    hu = _dot(n, wu_ref[...])
    a = _silu(hg) * hu
    o_ref[...] += _dot(a.astype(BF16), wd_ref[...])

    @pl.when(j == pl.num_programs(1) - 1)
    def _():
        for r in range(0, tm, rc):
            y = x_ref[r:r + rc, :] + 0.5 * o_ref[r:r + rc, :]
            if final_norm:
                y = _rms(y, gf_ref[...])
            o_ref[r:r + rc, :] = y


def _ffn(x, g, wg, wu, wd, g_final=None):
    n, d = x.shape
    f = wg.shape[1]
    tm = _tile(n, 1024)
    tf = _col_tile(f, 256)
    in_specs = [
        pl.BlockSpec((tm, d), lambda i, j: (i, 0)),
        pl.BlockSpec((1, d), lambda i, j: (0, 0)),
        pl.BlockSpec((d, tf), lambda i, j: (0, j)),
        pl.BlockSpec((d, tf), lambda i, j: (0, j)),
        pl.BlockSpec((tf, d), lambda i, j: (j, 0)),
    ]
    args = [x, g.reshape(1, d), wg, wu, wd]
    if g_final is not None:
        in_specs.append(pl.BlockSpec((1, d), lambda i, j: (0, 0)))
        args.append(g_final.reshape(1, d))
    return pl.pallas_call(
        functools.partial(_ffn_kernel, final_norm=g_final is not None),
        grid=(n // tm, f // tf),
        in_specs=in_specs,
        out_specs=pl.BlockSpec((tm, d), lambda i, j: (i, 0)),
        out_shape=jax.ShapeDtypeStruct((n, d), F32),
        scratch_shapes=[pltpu.VMEM((tm, d), BF16)],
        compiler_params=_params("parallel", "arbitrary"),
        name="ffn",
    )(*args)


def _nmm_kernel(x_ref, g_ref, w_ref, o_ref, n_ref):
    tm = x_ref.shape[0]
    rc = min(tm, 256)

    @pl.when(pl.program_id(1) == 0)
    def _():
        for r in range(0, tm, rc):
            n_ref[r:r + rc, :] = _rms(x_ref[r:r + rc, :], g_ref[...]).astype(BF16)

    o_ref[...] = _dot(n_ref[...], w_ref[...])


def _norm_matmul(x, g, w):
    n, d = x.shape
    c = w.shape[1]
    tm = _tile(n, 1024)
    tn = _col_tile(c, 1536)
    return pl.pallas_call(
        _nmm_kernel,
        grid=(n // tm, c // tn),
        in_specs=[
            pl.BlockSpec((tm, d), lambda i, j: (i, 0)),
            pl.BlockSpec((1, d), lambda i, j: (0, 0)),
            pl.BlockSpec((d, tn), lambda i, j: (0, j)),
        ],
        out_specs=pl.BlockSpec((tm, tn), lambda i, j: (i, j)),
        out_shape=jax.ShapeDtypeStruct((n, c), F32),
        scratch_shapes=[pltpu.VMEM((tm, d), BF16)],
        compiler_params=_params("parallel", "arbitrary"),
        name="norm_matmul",
    )(x, g.reshape(1, d), w)


def _mmres_kernel(*refs, n_in):
    x_ref, o_ref = refs[2 * n_in], refs[2 * n_in + 1]
    acc = x_ref[...]
    for a_ref, w_ref in zip(refs[:n_in], refs[n_in:2 * n_in]):
        acc = acc + _dot(a_ref[...], w_ref[...])
    o_ref[...] = acc


def _matmul_residual(acts, weights, x):
    n, d = x.shape
    tm = _tile(n, 512)
    in_specs = [pl.BlockSpec((tm, a.shape[1]), lambda i: (i, 0)) for a in acts]
    in_specs += [pl.BlockSpec(w.shape, lambda i: (0, 0)) for w in weights]
    in_specs.append(pl.BlockSpec((tm, d), lambda i: (i, 0)))
    return pl.pallas_call(
        functools.partial(_mmres_kernel, n_in=len(acts)),
        grid=(n // tm,),
        in_specs=in_specs,
        out_specs=pl.BlockSpec((tm, d), lambda i: (i, 0)),
        out_shape=jax.ShapeDtypeStruct((n, d), F32),
        compiler_params=_params("parallel"),
        name="matmul_residual",
    )(*acts, *weights, x)


def _evenpost_kernel(z_ref, zp_ref, cp_ref, wc_ref, gq_ref, wqn_ref, wqp_ref, wqs_ref, wuk_ref, gkv_ref,
                     c1_ref, s1_ref, c8_ref, s8_ref,
                     ya_ref, cs_ref, q_ref, ckv_ref, kpe_ref, kv_ref, *, dc, ql, kl):
    t = pl.program_id(1)
    tt = z_ref.shape[1]
    o3 = 3 * dc
    gb = z_ref[0, :, 0:dc]
    v = z_ref[0, :, dc:2 * dc] * z_ref[0, :, 2 * dc:o3]
    pv = zp_ref[0, :, dc:2 * dc] * zp_ref[0, :, 2 * dc:o3]
    cp = cp_ref[0]
    first = t == 0
    pm2 = jnp.where(first, cp[0:1], pv[SUBLANES - 2:SUBLANES - 1])
    pm1 = jnp.where(first, cp[1:2], pv[SUBLANES - 1:SUBLANES])
    rows = lax.broadcasted_iota(jnp.int32, v.shape, 0)
    v1 = jnp.where(rows == 0, pm1, pltpu.roll(v, 1, 0))
    v2 = jnp.where(rows == 0, pm2, jnp.where(rows == 1, pm1, pltpu.roll(v, 2, 0)))
    w = wc_ref[...]
    conv = v2 * w[0:1] + v1 * w[1:2] + v * w[2:3]
    ya_ref[0] = (gb * conv).astype(BF16)

    @pl.when(t == pl.num_programs(1) - 1)
    def _():
        cs_ref[0] = v[tt - (CONV_W - 1):tt]

    cqn = _rms(z_ref[0, :, o3:o3 + ql], gq_ref[...]).astype(BF16)
    qn = _dot(cqn, wqn_ref[...])
    qpr = _dot(cqn, wqp_ref[...]) * c8_ref[...] + _dot(cqn, wqs_ref[...]) * s8_ref[...]
    for h in range(MLA_HEADS):
        qlat = _dot(qn[:, h * NOPE_DIM:(h + 1) * NOPE_DIM].astype(BF16), wuk_ref[h])
        q_ref[0, h, :, 0:kl] = qlat.astype(BF16)
        q_ref[0, h, :, kl:kl + ROPE_DIM] = qpr[:, h * ROPE_DIM:(h + 1) * ROPE_DIM].astype(BF16)
    o4 = o3 + ql
    o5 = o4 + kl
    cn = _rms(z_ref[0, :, o4:o5], gkv_ref[...])
    kr = z_ref[0, :, o5:o5 + ROPE_DIM] * c1_ref[...] + z_ref[0, :, o5 + ROPE_DIM:o5 + 2 * ROPE_DIM] * s1_ref[...]
    ckv_ref[0] = cn
    kpe_ref[0] = kr
    kv_ref[0, :, 0:kl] = cn.astype(BF16)
    kv_ref[0, :, kl:kl + ROPE_DIM] = kr.astype(BF16)


def _evenpost(z, conv_prev, w_conv, g_q, wqn, wqp, wqs, wuk, g_kv, c1, s1, c8, s8):
    b, t, zw = z.shape
    dc = w_conv.shape[1]
    ql = g_q.shape[0]
    kl = g_kv.shape[0]
    tt = _tile(t, 256)
    hr = MLA_HEADS * ROPE_DIM
    full = lambda a: pl.BlockSpec(a.shape, lambda i, j: (0,) * a.ndim)
    wts = [w_conv, g_q.reshape(1, ql), wqn, wqp, wqs, wuk, g_kv.reshape(1, kl)]
    in_specs = [
        pl.BlockSpec((1, tt, zw), lambda i, j: (i, j, 0)),
        pl.BlockSpec((1, SUBLANES, zw), lambda i, j: (i, jnp.maximum(j * (tt // SUBLANES) - 1, 0), 0)),
        pl.BlockSpec((1, CONV_W - 1, dc), lambda i, j: (i, 0, 0)),
    ] + [full(a) for a in wts] + [
        pl.BlockSpec((tt, ROPE_DIM), lambda i, j: (j, 0)),
        pl.BlockSpec((tt, ROPE_DIM), lambda i, j: (j, 0)),
        pl.BlockSpec((tt, hr), lambda i, j: (j, 0)),
        pl.BlockSpec((tt, hr), lambda i, j: (j, 0)),
    ]
    out_shape = [
        jax.ShapeDtypeStruct((b, t, dc), BF16),
        jax.ShapeDtypeStruct((b, CONV_W - 1, dc), F32),
        jax.ShapeDtypeStruct((b, MLA_HEADS, t, kl + ROPE_DIM), BF16),
        jax.ShapeDtypeStruct((b, t, kl), F32),
        jax.ShapeDtypeStruct((b, t, ROPE_DIM), F32),
        jax.ShapeDtypeStruct((b, t, kl + ROPE_DIM), BF16),
    ]
    out_specs = [
        pl.BlockSpec((1, tt, dc), lambda i, j: (i, j, 0)),
        pl.BlockSpec((1, CONV_W - 1, dc), lambda i, j: (i, 0, 0)),
        pl.BlockSpec((1, MLA_HEADS, tt, kl + ROPE_DIM), lambda i, j: (i, 0, j, 0)),
        pl.BlockSpec((1, tt, kl), lambda i, j: (i, j, 0)),
        pl.BlockSpec((1, tt, ROPE_DIM), lambda i, j: (i, j, 0)),
        pl.BlockSpec((1, tt, kl + ROPE_DIM), lambda i, j: (i, j, 0)),
    ]
    return pl.pallas_call(
        functools.partial(_evenpost_kernel, dc=dc, ql=ql, kl=kl),
        grid=(b, t // tt),
        in_specs=in_specs,
        out_specs=out_specs,
        out_shape=out_shape,
        compiler_params=_params("parallel", "arbitrary"),
        name="conv_mla_prep",
    )(z, z, conv_prev, *wts, c1, s1, c8, s8)


def _attn_kernel(q_ref, kv_ref, wuv_ref, o_ref, m_ref, l_ref, acc_ref, *, tq, tk, causal, kv_len, kl):
    i = pl.program_id(1)
    rows = MLA_HEADS * tq
    q = q_ref[0].reshape(rows, q_ref.shape[3])
    m_ref[...] = jnp.full(m_ref.shape, NEG_BIG, F32)
    l_ref[...] = jnp.zeros_like(l_ref)
    acc_ref[...] = jnp.zeros_like(acc_ref)
    if causal:
        n_kv = ((i + 1) * tq + tk - 1) // tk
        n_full = (i * tq + CHUNK) // tk
        tok = i * tq + lax.rem(lax.broadcasted_iota(jnp.int32, (rows, 1), 0), tq)
        limit = (tok // CHUNK + 1) * CHUNK
    else:
        n_kv = (kv_len + tk - 1) // tk
        n_full = kv_len // tk
        limit = kv_len
    c = MLA_SCALE * LOG2E

    def step(j, masked):
        k = kv_ref[0, pl.ds(pl.multiple_of(j * tk, tk), tk), :]
        s = _dot_nt(q, k)
        if masked:
            kpos = j * tk + lax.broadcasted_iota(jnp.int32, (1, tk), 1)
            s = jnp.where(kpos < limit, s, NEG_BIG)
        m_prev = m_ref[...]
        m_new = jnp.maximum(m_prev, jnp.max(s, axis=-1, keepdims=True))
        alpha = jnp.exp2((m_prev - m_new) * c)
        p = jnp.exp2((s - m_new) * c)
        l_ref[...] = alpha * l_ref[...] + jnp.sum(p, axis=-1, keepdims=True)
        acc_ref[...] = alpha * acc_ref[...] + _dot(p.astype(BF16), k[:, 0:kl])
        m_ref[...] = m_new

    def full_body(j, carry):
        step(j, False)
        return carry

    def masked_body(j, carry):
        step(j, True)
        return carry

    lax.fori_loop(0, n_full, full_body, 0)
    lax.fori_loop(n_full, n_kv, masked_body, 0)
    o = (acc_ref[...] / l_ref[...]).astype(BF16)
    for h in range(MLA_HEADS):
        o_ref[0, :, h * V_DIM:(h + 1) * V_DIM] = _dot(o[h * tq:(h + 1) * tq], wuv_ref[h]).astype(BF16)


def _attention(q, kv, wuv, *, causal, kv_len):
    b, h, t, dq = q.shape
    tkv = kv.shape[1]
    kl = wuv.shape[1]
    tq = _tile(t, 128)
    tk = _tile(tkv, 512)
    rows = h * tq
    return pl.pallas_call(
        functools.partial(_attn_kernel, tq=tq, tk=tk, causal=causal, kv_len=kv_len, kl=kl),
        grid=(b, t // tq),
        in_specs=[
            pl.BlockSpec((1, h, tq, dq), lambda i, j: (i, 0, j, 0)),
            pl.BlockSpec((1, tkv, dq), lambda i, j: (i, 0, 0)),
            pl.BlockSpec(wuv.shape, lambda i, j: (0, 0, 0)),
        ],
        out_specs=pl.BlockSpec((1, tq, h * V_DIM), lambda i, j: (i, j, 0)),
        out_shape=jax.ShapeDtypeStruct((b, t, h * V_DIM), BF16),
        scratch_shapes=[pltpu.VMEM((rows, 1), F32), pltpu.VMEM((rows, 1), F32), pltpu.VMEM((rows, kl), F32)],
        compiler_params=_params("parallel", "arbitrary"),
        name="mla_attention",
    )(q, kv, wuv)


def _level_masks():
    t = np.arange(CHUNK)[:, None]
    s = np.arange(CHUNK)[None, :]
    masks = [(t // (2 * m) == s // (2 * m)) & ((t & m) != 0) & ((s & m) == 0) for m in LEVELS]
    masks.append(t == s)
    return np.stack(masks).astype(np.float32)


def _boundary_rows(a, m):
    w = a.shape[1]
    if m >= SUBLANES:
        parts = [jnp.broadcast_to(a[b0 + m - 1:b0 + m], (2 * m, w)) for b0 in range(0, CHUNK, 2 * m)]
        return parts[0] if len(parts) == 1 else jnp.concatenate(parts, axis=0)
    a3 = a.reshape(CHUNK // SUBLANES, SUBLANES, w)
    sub = lax.broadcasted_iota(jnp.int32, a3.shape, 1)
    out = None
    for b0 in range(SUBLANES - 2 * m, -1, -2 * m):
        piece = jnp.broadcast_to(a3[:, b0 + m - 1:b0 + m, :], a3.shape)
        out = piece if out is None else jnp.where(sub < b0 + 2 * m, piece, out)
    return out.reshape(CHUNK, w)


def _pick_halves(first, second, m, is_second):
    if m < SUBLANES:
        return jnp.where(is_second, second, first)
    parts = []
    for b0 in range(0, CHUNK, 2 * m):
        parts += [first[b0:b0 + m], second[b0 + m:b0 + 2 * m]]
    return jnp.concatenate(parts, axis=0)


def _gla_kernel(zq_ref, zf_ref, zi_ref, zg_ref, lbl_ref, go_ref, s0_ref, tri_ref, mk_ref, o_ref, st_ref, *, layer):
    t = pl.program_id(1)
    tc = zq_ref.shape[1]
    n_lv = len(LEVELS)

    @pl.when(t == 0)
    def _():
        st_ref[...] = s0_ref[...]

    ll = lbl_ref[...]
    e = jnp.exp(ll - jnp.max(ll, axis=0, keepdims=True))
    sm = e / jnp.sum(e, axis=0, keepdims=True)
    lb = jnp.zeros((1, ll.shape[1]), F32)
    for r in range(1, layer + 1):
        lb = lb + sm[r:r + 1]
    one_m_lb = 1.0 - lb
    go = go_ref[...]
    tri = tri_ref[...]
    second = [(lax.broadcasted_iota(jnp.int32, (CHUNK, 1), 0) & m) != 0 for m in LEVELS]

    def chunk_body(c, carry):
        r0 = pl.multiple_of(c * CHUNK, CHUNK)
        zq = zq_ref[0, pl.ds(r0, CHUNK), :]
        zf = zf_ref[0, pl.ds(r0, CHUNK), :]
        v = zi_ref[0, pl.ds(r0, CHUNK), :]
        zg = zg_ref[0, pl.ds(r0, CHUNK), :]
        sig = _sigmoid(zf)
        g = jnp.log(lb + one_m_lb * sig) * LOG2E
        k = one_m_lb * (1.0 - sig)
        q = _silu(zq)
        gate = _silu(zg)
        g1 = g.astype(BF16)
        r1 = g - g1.astype(F32)
        g2 = r1.astype(BF16)
        g3 = (r1 - g2.astype(F32)).astype(BF16)
        a = _dot(tri, g1) + _dot(tri, g2) + _dot(tri, g3)
        e_q = jnp.exp2(a)
        e_k = jnp.exp2(a[CHUNK - 1:CHUNK] - a)
        e_last = e_q[CHUNK - 1:CHUNK]
        qt = (q * e_q).astype(BF16)
        kt = (k * e_k).astype(BF16)
        qb = q.astype(BF16)
        kb = k.astype(BF16)
        vb = v.astype(BF16)
        ops = []
        for li, m in enumerate(LEVELS):
            ex = jnp.exp2(-jnp.abs(a - _boundary_rows(a, m)))
            ops.append((_pick_halves(k, q, m, second[li]) * ex).astype(BF16))
        for h in range(HG_HEADS):
            sl = slice(h * HG_DK, (h + 1) * HG_DK)
            p = _dot_nt(qb[:, sl], kb[:, sl]) * mk_ref[n_lv]
            for li in range(n_lv):
                p = p + _dot_nt(ops[li][:, sl], ops[li][:, sl]) * mk_ref[li]
            st = st_ref[0, h]
            o = _dot_nt(qt[:, sl], st.astype(BF16)) + _dot(p.astype(BF16), vb[:, sl])
            st_ref[0, h] = st * e_last[:, sl] + _dot_tn(vb[:, sl], kt[:, sl])
            og = _rms(o, go) * gate[:, sl]
            o_ref[0, pl.ds(r0, CHUNK), sl] = og.astype(BF16)
        return carry

    lax.fori_loop(0, tc // CHUNK, chunk_body, 0)


def _gla(z, lb_logits, layer, g_o, s0t):
    b, t, zw = z.shape
    dh = zw // 4
    tc = _tile(t, 256)
    tri = jnp.asarray(np.tril(np.ones((CHUNK, CHUNK), np.float32)), BF16)
    masks = jnp.asarray(_level_masks(), F32)
    zspec = lambda c: pl.BlockSpec((1, tc, dh), lambda i, j, c=c: (i, j, c))
    sspec = pl.BlockSpec((1,) + s0t.shape[1:], lambda i, j: (i, 0, 0, 0))
    return pl.pallas_call(
        functools.partial(_gla_kernel, layer=layer),
        grid=(b, t // tc),
        in_specs=[zspec(0), zspec(1), zspec(2), zspec(3),
                  pl.BlockSpec(lb_logits.shape, lambda i, j: (0, 0)),
                  pl.BlockSpec((1, HG_DV), lambda i, j: (0, 0)),
                  sspec,
                  pl.BlockSpec(tri.shape, lambda i, j: (0, 0)),
                  pl.BlockSpec(masks.shape, lambda i, j: (0, 0, 0))],
        out_specs=[pl.BlockSpec((1, tc, dh), lambda i, j: (i, j, 0)), sspec],
        out_shape=[jax.ShapeDtypeStruct((b, t, dh), BF16), jax.ShapeDtypeStruct(s0t.shape, F32)],
        compiler_params=_params("parallel", "arbitrary"),
        name="hgrn2_recurrence",
    )(z, z, z, z, lb_logits, g_o.reshape(1, HG_DV), s0t, tri, masks)


def _rope_tables(pos):
    half = ROPE_DIM // 2
    inv = ROPE_THETA ** (-jnp.arange(half, dtype=F32) / half)
    ang = pos.astype(F32)[:, None] * inv[None, :]
    cos, sin = jnp.cos(ang), jnp.sin(ang)
    c1 = jnp.concatenate([cos, cos], axis=1)
    s1 = jnp.concatenate([-sin, sin], axis=1)
    return c1, s1, jnp.tile(c1, (1, MLA_HEADS)), jnp.tile(s1, (1, MLA_HEADS))


def _swap_halves(w):
    k, c = w.shape
    w = w.reshape(k, c // ROPE_DIM, 2, ROPE_DIM // 2)
    return w[:, :, ::-1, :].reshape(k, c)


def _prep_even(w_in_a, w_uq, w_ukv, w_out_a, dc):
    d = w_in_a.shape[0]
    kpe = w_in_a[:, -ROPE_DIM:]
    w_in = jnp.concatenate([w_in_a, _swap_halves(kpe)], axis=1).astype(BF16)
    ql = w_uq.shape[0]
    wq = w_uq.reshape(ql, MLA_HEADS, NOPE_DIM + ROPE_DIM)
    wqn = wq[:, :, :NOPE_DIM].reshape(ql, MLA_HEADS * NOPE_DIM).astype(BF16)
    wqp = wq[:, :, NOPE_DIM:].reshape(ql, MLA_HEADS * ROPE_DIM)
    wqs = _swap_halves(wqp).astype(BF16)
    kl = w_ukv.shape[0]
    wkv = w_ukv.reshape(kl, MLA_HEADS, NOPE_DIM + V_DIM)
    wuk = jnp.transpose(wkv[:, :, :NOPE_DIM], (1, 2, 0)).astype(BF16)
    wuv = jnp.transpose(wkv[:, :, NOPE_DIM:], (1, 0, 2)).astype(BF16)
    wo = w_out_a.astype(BF16)
    return dict(w_in=w_in, wqn=wqn, wqp=wqp.astype(BF16), wqs=wqs, wuk=wuk, wuv=wuv, wo_a=wo[:dc], wo_b=wo[dc:])


def _run_group(x3, pos, conv_prev, ckv_prev, kpe_prev, hgrn_prev, p):
    b, t, d = x3.shape
    n = b * t
    depth = p["norm_ffn1"].shape[0]
    x = x3.reshape(n, d)
    c1, s1, c8, s8 = _rope_tables(pos)
    conv_new, ckv_new, kpe_new, hgrn_new = [], [], [], []
    for l in range(depth):
        x = _ffn(x, p["norm_ffn1"][l], p["wg1"][l], p["wu1"][l], p["wd1"][l])
        if l % 2 == 0:
            e = l // 2
            pe = p["even"][e]
            dc = p["w_conv"].shape[2]
            z = _norm_matmul(x, p["norm_mix"][l], pe["w_in"]).reshape(b, t, -1)
            cprev = jnp.zeros((b, CONV_W - 1, dc), F32) if conv_prev is None else conv_prev[e]
            ya, cs, q, ckv, kpe, kv = _evenpost(z, cprev, p["w_conv"][e], p["g_q"][e], pe["wqn"], pe["wqp"],
                                                pe["wqs"], pe["wuk"], p["g_kv"][e], c1, s1, c8, s8)
            if ckv_prev is None:
                yb = _attention(q, kv, pe["wuv"], causal=True, kv_len=t)
            else:
                past = ckv_prev.shape[2]
                kv_len = past + t
                tk = 512
                pad = (-kv_len) % tk
                cache = jnp.concatenate([ckv_prev[e], kpe_prev[e]], axis=-1).astype(BF16)
                kv_all = jnp.concatenate([cache, kv, jnp.zeros((b, pad, kv.shape[2]), BF16)], axis=1)
                yb = _attention(q, kv_all, pe["wuv"], causal=False, kv_len=kv_len)
            x = _matmul_residual([ya.reshape(n, -1), yb.reshape(n, -1)], [pe["wo_a"], pe["wo_b"]], x)
            conv_new.append(cs)
            ckv_new.append(ckv)
            kpe_new.append(kpe)
        else:
            o = l // 2
            z = _norm_matmul(x, p["norm_mix"][l], p["w_in_c"][o]).reshape(b, t, -1)
            if hgrn_prev is None:
                s0t = jnp.zeros((b, HG_HEADS, HG_DV, HG_DK), F32)
            else:
                s0t = jnp.swapaxes(hgrn_prev[o], -1, -2)
            og, st = _gla(z, p["lb_logits"], o, p["g_o"][o], s0t)
            x = _matmul_residual([og.reshape(n, -1)], [p["w_out_c"][o]], x)
            hgrn_new.append(jnp.swapaxes(st, -1, -2))
        g_final = p["norm_final"] if l == depth - 1 else None
        x = _ffn(x, p["norm_ffn2"][l], p["wg2"][l], p["wu2"][l], p["wd2"][l], g_final)
    return x.reshape(b, t, d), jnp.stack(conv_new), jnp.stack(ckv_new), jnp.stack(kpe_new), jnp.stack(hgrn_new)


def kernel(x_prompt, x_sample, cache_conv, cache_ckv, cache_kpe, state_hgrn, norm_ffn1, w_ffn1_gate, w_ffn1_up, w_ffn1_down, norm_mix, w_in_a, w_conv, g_q, w_uq, g_kv, w_ukv, w_out_a, w_in_c, lb_logits, g_o, w_out_c, norm_ffn2, w_ffn2_gate, w_ffn2_up, w_ffn2_down, norm_final):
    dc = w_conv.shape[2]
    p = dict(
        norm_ffn1=norm_ffn1, wg1=w_ffn1_gate.astype(BF16), wu1=w_ffn1_up.astype(BF16), wd1=w_ffn1_down.astype(BF16),
        norm_ffn2=norm_ffn2, wg2=w_ffn2_gate.astype(BF16), wu2=w_ffn2_up.astype(BF16), wd2=w_ffn2_down.astype(BF16),
        norm_mix=norm_mix, w_conv=w_conv, g_q=g_q, g_kv=g_kv,
        even=[_prep_even(w_in_a[e], w_uq[e], w_ukv[e], w_out_a[e], dc) for e in range(w_in_a.shape[0])],
        w_in_c=w_in_c.astype(BF16), lb_logits=lb_logits, g_o=g_o, w_out_c=w_out_c.astype(BF16),
        norm_final=norm_final,
    )
    pos_p = jnp.arange(x_prompt.shape[1], dtype=jnp.int32)
    pos_s = cache_ckv.shape[2] + jnp.arange(x_sample.shape[1], dtype=jnp.int32)
    y_p, conv_p, ckv_p, kpe_p, hgrn_p = _run_group(x_prompt, pos_p, None, None, None, None, p)
    y_s, conv_s, ckv_s, kpe_s, hgrn_s = _run_group(x_sample, pos_s, cache_conv, cache_ckv, cache_kpe, state_hgrn, p)
    return (y_p, y_s, conv_p, ckv_p, kpe_p, hgrn_p, conv_s, ckv_s, kpe_s, hgrn_s)
```

```python
import functools

import numpy as np
import jax
import jax.numpy as jnp
from jax import lax
from jax.experimental import pallas as pl
from jax.experimental.pallas import tpu as pltpu

F32 = jnp.float32
BF16 = jnp.bfloat16

EPS = 1e-6
CHUNK = 64
CONV_W = 3
MLA_HEADS = 8
NOPE_DIM = 128
ROPE_DIM = 64
V_DIM = 128
ROPE_THETA = 10000.0
MLA_SCALE = (NOPE_DIM + ROPE_DIM) ** -0.5
HG_HEADS = 16
HG_DK = 128
HG_DV = 128
LANES = 128
SUBLANES = 8
NEG_BIG = -1e30
LOG2E = 1.4426950408889634
LEVELS = (32, 16, 8, 4, 2, 1)
VMEM_LIMIT_BYTES = 60 * 1024 * 1024


def _tile(n, pref):
    if n <= pref:
        return n
    for t in range(pref, 7, -8):
        if n % t == 0:
            return t
    return n


def _col_tile(n, pref):
    for t in range(min(pref, n) // LANES * LANES, 0, -LANES):
        if n % t == 0:
            return t
    return n


def _params(*sem, **kw):
    return pltpu.CompilerParams(dimension_semantics=sem, vmem_limit_bytes=VMEM_LIMIT_BYTES, **kw)


def _sigmoid(x):
    return 1.0 / (1.0 + jnp.exp(-x))


def _silu(x):
    h = 0.5 * x
    return h + h * jnp.tanh(h)


def _rms(x, g):
    return x * lax.rsqrt(jnp.mean(x * x, axis=-1, keepdims=True) + EPS) * g


def _dot(a, b):
    return jnp.dot(a, b, preferred_element_type=F32)


def _dot_nt(a, b):
    return lax.dot_general(a, b, (((1,), (1,)), ((), ())), preferred_element_type=F32)


def _dot_tn(a, b):
    return lax.dot_general(a, b, (((0,), (0,)), ((), ())), preferred_element_type=F32)


def _ffn_kernel(x_ref, g_ref, wg_ref, wu_ref, wd_ref, *rest, final_norm):
    if final_norm:
        gf_ref, o_ref, n_ref = rest
    else:
        o_ref, n_ref = rest
    j = pl.program_id(1)
    tm = x_ref.shape[0]
    rc = min(tm, 256)

    @pl.when(j == 0)
    def _():
        for r in range(0, tm, rc):
            n_ref[r:r + rc, :] = _rms(x_ref[r:r + rc, :], g_ref[...]).astype(BF16)
        o_ref[...] = jnp.zeros_like(o_ref)

    n = n_ref[...]
    hg = _dot(n, wg_ref[...])
    hu = _dot(n, wu_ref[...])
    a = _silu(hg) * hu
    o_ref[...] += _dot(a.astype(BF16), wd_ref[...])

    @pl.when(j == pl.num_programs(1) - 1)
    def _():
        for r in range(0, tm, rc):
            y = x_ref[r:r + rc, :] + 0.5 * o_ref[r:r + rc, :]
            if final_norm:
                y = _rms(y, gf_ref[...])
            o_ref[r:r + rc, :] = y


def _ffn(x, g, wg, wu, wd, g_final=None):
    n, d = x.shape
    f = wg.shape[1]
    tm = _tile(n, 1024)
    tf = _col_tile(f, 256)
    in_specs = [
        pl.BlockSpec((tm, d), lambda i, j: (i, 0)),
        pl.BlockSpec((1, d), lambda i, j: (0, 0)),
        pl.BlockSpec((d, tf), lambda i, j: (0, j)),
        pl.BlockSpec((d, tf), lambda i, j: (0, j)),
        pl.BlockSpec((tf, d), lambda i, j: (j, 0)),
    ]
    args = [x, g.reshape(1, d), wg, wu, wd]
    if g_final is not None:
        in_specs.append(pl.BlockSpec((1, d), lambda i, j: (0, 0)))
        args.append(g_final.reshape(1, d))
    return pl.pallas_call(
        functools.partial(_ffn_kernel, final_norm=g_final is not None),
        grid=(n // tm, f // tf),
        in_specs=in_specs,
        out_specs=pl.BlockSpec((tm, d), lambda i, j: (i, 0)),
        out_shape=jax.ShapeDtypeStruct((n, d), F32),
        scratch_shapes=[pltpu.VMEM((tm, d), BF16)],
        compiler_params=_params("parallel", "arbitrary"),
        name="ffn",
    )(*args)


def _nmm_kernel(x_ref, g_ref, w_ref, o_ref, n_ref):
    tm = x_ref.shape[0]
    rc = min(tm, 256)

    @pl.when(pl.program_id(1) == 0)
    def _():
        for r in range(0, tm, rc):
            n_ref[r:r + rc, :] = _rms(x_ref[r:r + rc, :], g_ref[...]).astype(BF16)

    o_ref[...] = _dot(n_ref[...], w_ref[...])


def _norm_matmul(x, g, w):
    n, d = x.shape
    c = w.shape[1]
    tm = _tile(n, 1024)
    tn = _col_tile(c, 1536)
    return pl.pallas_call(
        _nmm_kernel,
        grid=(n // tm, c // tn),
        in_specs=[
            pl.BlockSpec((tm, d), lambda i, j: (i, 0)),
            pl.BlockSpec((1, d), lambda i, j: (0, 0)),
            pl.BlockSpec((d, tn), lambda i, j: (0, j)),
        ],
        out_specs=pl.BlockSpec((tm, tn), lambda i, j: (i, j)),
        out_shape=jax.ShapeDtypeStruct((n, c), F32),
        scratch_shapes=[pltpu.VMEM((tm, d), BF16)],
        compiler_params=_params("parallel", "arbitrary"),
        name="norm_matmul",
    )(x, g.reshape(1, d), w)


def _mmres_kernel(*refs, n_in):
    x_ref, o_ref = refs[2 * n_in], refs[2 * n_in + 1]
    acc = x_ref[...]
    for a_ref, w_ref in zip(refs[:n_in], refs[n_in:2 * n_in]):
        acc = acc + _dot(a_ref[...], w_ref[...])
    o_ref[...] = acc


def _matmul_residual(acts, weights, x):
    n, d = x.shape
    tm = _tile(n, 512)
    in_specs = [pl.BlockSpec((tm, a.shape[1]), lambda i: (i, 0)) for a in acts]
    in_specs += [pl.BlockSpec(w.shape, lambda i: (0, 0)) for w in weights]
    in_specs.append(pl.BlockSpec((tm, d), lambda i: (i, 0)))
    return pl.pallas_call(
        functools.partial(_mmres_kernel, n_in=len(acts)),
        grid=(n // tm,),
        in_specs=in_specs,
        out_specs=pl.BlockSpec((tm, d), lambda i: (i, 0)),
        out_shape=jax.ShapeDtypeStruct((n, d), F32),
        compiler_params=_params("parallel"),
        name="matmul_residual",
    )(*acts, *weights, x)


def _evenpost_kernel(z_ref, zp_ref, cp_ref, wc_ref, gq_ref, wqn_ref, wqp_ref, wqs_ref, wuk_ref, gkv_ref,
                     c1_ref, s1_ref, c8_ref, s8_ref,
                     ya_ref, cs_ref, q_ref, ckv_ref, kpe_ref, kv_ref, *, dc, ql, kl):
    t = pl.program_id(1)
    tt = z_ref.shape[1]
    o3 = 3 * dc
    gb = z_ref[0, :, 0:dc]
    v = z_ref[0, :, dc:2 * dc] * z_ref[0, :, 2 * dc:o3]
    pv = zp_ref[0, :, dc:2 * dc] * zp_ref[0, :, 2 * dc:o3]
    cp = cp_ref[0]
    first = t == 0
    pm2 = jnp.where(first, cp[0:1], pv[SUBLANES - 2:SUBLANES - 1])
    pm1 = jnp.where(first, cp[1:2], pv[SUBLANES - 1:SUBLANES])
    rows = lax.broadcasted_iota(jnp.int32, v.shape, 0)
    v1 = jnp.where(rows == 0, pm1, pltpu.roll(v, 1, 0))
    v2 = jnp.where(rows == 0, pm2, jnp.where(rows == 1, pm1, pltpu.roll(v, 2, 0)))
    w = wc_ref[...]
    conv = v2 * w[0:1] + v1 * w[1:2] + v * w[2:3]
    ya_ref[0] = (gb * conv).astype(BF16)

    @pl.when(t == pl.num_programs(1) - 1)
    def _():
        cs_ref[0] = v[tt - (CONV_W - 1):tt]

    cqn = _rms(z_ref[0, :, o3:o3 + ql], gq_ref[...]).astype(BF16)
    qn = _dot(cqn, wqn_ref[...])
    qpr = _dot(cqn, wqp_ref[...]) * c8_ref[...] + _dot(cqn, wqs_ref[...]) * s8_ref[...]
    for h in range(MLA_HEADS):
        qlat = _dot(qn[:, h * NOPE_DIM:(h + 1) * NOPE_DIM].astype(BF16), wuk_ref[h])
        q_ref[0, h, :, 0:kl] = qlat.astype(BF16)
        q_ref[0, h, :, kl:kl + ROPE_DIM] = qpr[:, h * ROPE_DIM:(h + 1) * ROPE_DIM].astype(BF16)
    o4 = o3 + ql
    o5 = o4 + kl
    cn = _rms(z_ref[0, :, o4:o5], gkv_ref[...])
    kr = z_ref[0, :, o5:o5 + ROPE_DIM] * c1_ref[...] + z_ref[0, :, o5 + ROPE_DIM:o5 + 2 * ROPE_DIM] * s1_ref[...]
    ckv_ref[0] = cn
    kpe_ref[0] = kr
    kv_ref[0, :, 0:kl] = cn.astype(BF16)
    kv_ref[0, :, kl:kl + ROPE_DIM] = kr.astype(BF16)


def _evenpost(z, conv_prev, w_conv, g_q, wqn, wqp, wqs, wuk, g_kv, c1, s1, c8, s8):
    b, t, zw = z.shape
    dc = w_conv.shape[1]
    ql = g_q.shape[0]
    kl = g_kv.shape[0]
    tt = _tile(t, 256)
    hr = MLA_HEADS * ROPE_DIM
    full = lambda a: pl.BlockSpec(a.shape, lambda i, j: (0,) * a.ndim)
    wts = [w_conv, g_q.reshape(1, ql), wqn, wqp, wqs, wuk, g_kv.reshape(1, kl)]
    in_specs = [
        pl.BlockSpec((1, tt, zw), lambda i, j: (i, j, 0)),
        pl.BlockSpec((1, SUBLANES, zw), lambda i, j: (i, jnp.maximum(j * (tt // SUBLANES) - 1, 0), 0)),
        pl.BlockSpec((1, CONV_W - 1, dc), lambda i, j: (i, 0, 0)),
    ] + [full(a) for a in wts] + [
        pl.BlockSpec((tt, ROPE_DIM), lambda i, j: (j, 0)),
        pl.BlockSpec((tt, ROPE_DIM), lambda i, j: (j, 0)),
        pl.BlockSpec((tt, hr), lambda i, j: (j, 0)),
        pl.BlockSpec((tt, hr), lambda i, j: (j, 0)),
    ]
    out_shape = [
        jax.ShapeDtypeStruct((b, t, dc), BF16),
        jax.ShapeDtypeStruct((b, CONV_W - 1, dc), F32),
        jax.ShapeDtypeStruct((b, MLA_HEADS, t, kl + ROPE_DIM), BF16),
        jax.ShapeDtypeStruct((b, t, kl), F32),
        jax.ShapeDtypeStruct((b, t, ROPE_DIM), F32),
        jax.ShapeDtypeStruct((b, t, kl + ROPE_DIM), BF16),
    ]
    out_specs = [
        pl.BlockSpec((1, tt, dc), lambda i, j: (i, j, 0)),
        pl.BlockSpec((1, CONV_W - 1, dc), lambda i, j: (i, 0, 0)),
        pl.BlockSpec((1, MLA_HEADS, tt, kl + ROPE_DIM), lambda i, j: (i, 0, j, 0)),
        pl.BlockSpec((1, tt, kl), lambda i, j: (i, j, 0)),
        pl.BlockSpec((1, tt, ROPE_DIM), lambda i, j: (i, j, 0)),
        pl.BlockSpec((1, tt, kl + ROPE_DIM), lambda i, j: (i, j, 0)),
    ]
    return pl.pallas_call(
        functools.partial(_evenpost_kernel, dc=dc, ql=ql, kl=kl),
        grid=(b, t // tt),
        in_specs=in_specs,
        out_specs=out_specs,
        out_shape=out_shape,
        compiler_params=_params("parallel", "arbitrary"),
        name="conv_mla_prep",
    )(z, z, conv_prev, *wts, c1, s1, c8, s8)


def _attn_kernel(q_ref, kv_ref, wuv_ref, o_ref, *, tq, tk, causal, kv_len, kl):
    i = pl.program_id(1)
    rows = MLA_HEADS * tq
    q = q_ref[0].reshape(rows, q_ref.shape[3])
    if causal:
        tok = i * tq + lax.rem(lax.broadcasted_iota(jnp.int32, (rows, 1), 0), tq)
        limit = (tok // CHUNK + 1) * CHUNK
    else:
        limit = kv_len
    c = MLA_SCALE * LOG2E

    def scores(j):
        return _dot_nt(q, kv_ref[0, j * tk:(j + 1) * tk, :])

    def run(n):
        s = scores(0)
        m = l = acc = None
        for j in range(n):
            s_next = scores(j + 1) if j + 1 < n else None
            if j == n - 1:
                kpos = j * tk + lax.broadcasted_iota(jnp.int32, (1, tk), 1)
                s = jnp.where(kpos < limit, s, NEG_BIG)
            m_new = jnp.max(s, axis=-1, keepdims=True)
            if m is not None:
                m_new = jnp.maximum(m, m_new)
            p = jnp.exp2((s - m_new) * c)
            p_sum = jnp.sum(p, axis=-1, keepdims=True)
            pv = _dot(p.astype(BF16), kv_ref[0, j * tk:(j + 1) * tk, 0:kl])
            if m is None:
                l, acc = p_sum, pv
            else:
                alpha = jnp.exp2((m - m_new) * c)
                l = alpha * l + p_sum
                acc = alpha * acc + pv
            m, s = m_new, s_next
        o = (acc * (1.0 / l)).astype(BF16)
        for h in range(MLA_HEADS):
            o_ref[0, :, h * V_DIM:(h + 1) * V_DIM] = _dot(o[h * tq:(h + 1) * tq], wuv_ref[h]).astype(BF16)

    if causal:
        n_kv = (i * tq) // tk + 1
        for n in range(1, kv_ref.shape[1] // tk + 1):
            pl.when(n_kv == n)(functools.partial(run, n))
    else:
        run((kv_len + tk - 1) // tk)


def _attention(q, kv, wuv, *, causal, kv_len):
    b, h, t, dq = q.shape
    tkv = kv.shape[1]
    kl = wuv.shape[1]
    tq = _tile(t, 128)
    tk = _tile(tkv, 512)
    assert tk % tq == 0 and tkv % tk == 0 and tq % CHUNK == 0
    return pl.pallas_call(
        functools.partial(_attn_kernel, tq=tq, tk=tk, causal=causal, kv_len=kv_len, kl=kl),
        grid=(b, t // tq),
        in_specs=[
            pl.BlockSpec((1, h, tq, dq), lambda i, j: (i, 0, j, 0)),
            pl.BlockSpec((1, tkv, dq), lambda i, j: (i, 0, 0)),
            pl.BlockSpec(wuv.shape, lambda i, j: (0, 0, 0)),
        ],
        out_specs=pl.BlockSpec((1, tq, h * V_DIM), lambda i, j: (i, j, 0)),
        out_shape=jax.ShapeDtypeStruct((b, t, h * V_DIM), BF16),
        compiler_params=_params("parallel", "arbitrary"),
        name="mla_attention",
    )(q, kv, wuv)


def _level_masks():
    t = np.arange(CHUNK)[:, None]
    s = np.arange(CHUNK)[None, :]
    masks = [(t // (2 * m) == s // (2 * m)) & ((t & m) != 0) & ((s & m) == 0) for m in LEVELS]
    return np.stack(masks).astype(np.float32)


def _boundary_rows(a, m):
    w = a.shape[1]
    if m >= SUBLANES:
        parts = [jnp.broadcast_to(a[b0 + m - 1:b0 + m], (2 * m, w)) for b0 in range(0, CHUNK, 2 * m)]
        return parts[0] if len(parts) == 1 else jnp.concatenate(parts, axis=0)
    a3 = a.reshape(CHUNK // SUBLANES, SUBLANES, w)
    sub = lax.broadcasted_iota(jnp.int32, a3.shape, 1)
    out = None
    for b0 in range(SUBLANES - 2 * m, -1, -2 * m):
        piece = jnp.broadcast_to(a3[:, b0 + m - 1:b0 + m, :], a3.shape)
        out = piece if out is None else jnp.where(sub < b0 + 2 * m, piece, out)
    return out.reshape(CHUNK, w)


def _pick_halves(first, second, m, is_second):
    if m < SUBLANES:
        return jnp.where(is_second, second, first)
    parts = []
    for b0 in range(0, CHUNK, 2 * m):
        parts += [first[b0:b0 + m], second[b0 + m:b0 + 2 * m]]
    return jnp.concatenate(parts, axis=0)


def _gla_kernel(zq_ref, zf_ref, zi_ref, zg_ref, lbl_ref, go_ref, s0_ref, mk_ref, o_ref, st_ref, *, layer):
    t = pl.program_id(1)
    tc = zq_ref.shape[1]
    n_lv = len(LEVELS)

    @pl.when(t == 0)
    def _():
        st_ref[...] = s0_ref[...]

    ll = lbl_ref[...]
    e = jnp.exp(ll - jnp.max(ll, axis=0, keepdims=True))
    sm = e / jnp.sum(e, axis=0, keepdims=True)
    lb = jnp.zeros((1, ll.shape[1]), F32)
    for r in range(1, layer + 1):
        lb = lb + sm[r:r + 1]
    one_m_lb = 1.0 - lb
    go = go_ref[...]
    row = lax.broadcasted_iota(jnp.int32, (CHUNK, 1), 0)
    second = [(row & m) != 0 for m in LEVELS]

    def wide(c):
        rs = pl.ds(pl.multiple_of(c * CHUNK, CHUNK), CHUNK)
        zq = zq_ref[0, rs, :]
        zf = zf_ref[0, rs, :]
        v = zi_ref[0, rs, :]
        zg = zg_ref[0, rs, :]
        sig = _sigmoid(zf)
        g = jnp.log(lb + one_m_lb * sig) * LOG2E
        k = one_m_lb * (1.0 - sig)
        q = _silu(zq)
        gate = _silu(zg)
        a = g
        for sh in (1, 2, 4):
            a = a + jnp.where(row >= sh, pltpu.roll(a, sh, 0), 0.0)
        for sh in (8, 16, 32):
            a = a + jnp.concatenate([jnp.zeros((sh, a.shape[1]), F32), a[:CHUNK - sh]], axis=0)
        e_q = jnp.exp2(a)
        e_k = jnp.exp2(a[CHUNK - 1:CHUNK] - a)
        e_last = e_q[CHUNK - 1:CHUNK]
        qt = (q * e_q).astype(BF16)
        kt = (k * e_k).astype(BF16)
        qk = q * k
        vb = v.astype(BF16)
        ops = []
        for li, m in enumerate(LEVELS):
            ex = jnp.exp2(-jnp.abs(a - _boundary_rows(a, m)))
            ops.append((_pick_halves(k, q, m, second[li]) * ex).astype(BF16))
        return ops, qt, kt, vb, qk, v, gate, e_last

    def heads(c, w, states):
        ops, qt, kt, vb, qk, v, gate, e_last = w
        rs = pl.ds(pl.multiple_of(c * CHUNK, CHUNK), CHUNK)
        new_states = []
        for h in range(HG_HEADS):
            sl = slice(h * HG_DK, (h + 1) * HG_DK)
            p = _dot_nt(ops[0][:, sl], ops[0][:, sl]) * mk_ref[0]
            for li in range(1, n_lv):
                p = p + _dot_nt(ops[li][:, sl], ops[li][:, sl]) * mk_ref[li]
            st = states[h]
            o = _dot_nt(qt[:, sl], st.astype(BF16)) + _dot(p.astype(BF16), vb[:, sl])
            o = o + jnp.sum(qk[:, sl], axis=-1, keepdims=True) * v[:, sl]
            new_states.append(st * e_last[:, sl] + _dot_tn(vb[:, sl], kt[:, sl]))
            og = _rms(o, go) * gate[:, sl]
            o_ref[0, rs, sl] = og.astype(BF16)
        return new_states

    def chunk_body(c, carry):
        states = heads(c, wide(c), [st_ref[0, h] for h in range(HG_HEADS)])
        for h in range(HG_HEADS):
            st_ref[0, h] = states[h]
        return carry

    lax.fori_loop(0, tc // CHUNK, chunk_body, 0)


def _gla(z, lb_logits, layer, g_o, s0t):
    b, t, zw = z.shape
    dh = zw // 4
    tc = _tile(t, 256)
    masks = jnp.asarray(_level_masks(), F32)
    zspec = lambda c: pl.BlockSpec((1, tc, dh), lambda i, j, c=c: (i, j, c))
    sspec = pl.BlockSpec((1,) + s0t.shape[1:], lambda i, j: (i, 0, 0, 0))
    return pl.pallas_call(
        functools.partial(_gla_kernel, layer=layer),
        grid=(b, t // tc),
        in_specs=[zspec(0), zspec(1), zspec(2), zspec(3),
                  pl.BlockSpec(lb_logits.shape, lambda i, j: (0, 0)),
                  pl.BlockSpec((1, HG_DV), lambda i, j: (0, 0)),
                  sspec,
                  pl.BlockSpec(masks.shape, lambda i, j: (0, 0, 0))],
        out_specs=[pl.BlockSpec((1, tc, dh), lambda i, j: (i, j, 0)), sspec],
        out_shape=[jax.ShapeDtypeStruct((b, t, dh), BF16), jax.ShapeDtypeStruct(s0t.shape, F32)],
        compiler_params=_params("parallel", "arbitrary"),
        name="hgrn2_recurrence",
    )(z, z, z, z, lb_logits, g_o.reshape(1, HG_DV), s0t, masks)


def _rope_tables(pos):
    half = ROPE_DIM // 2
    inv = ROPE_THETA ** (-jnp.arange(half, dtype=F32) / half)
    ang = pos.astype(F32)[:, None] * inv[None, :]
    cos, sin = jnp.cos(ang), jnp.sin(ang)
    c1 = jnp.concatenate([cos, cos], axis=1)
    s1 = jnp.concatenate([-sin, sin], axis=1)
    return c1, s1, jnp.tile(c1, (1, MLA_HEADS)), jnp.tile(s1, (1, MLA_HEADS))


def _swap_halves(w):
    k, c = w.shape
    w = w.reshape(k, c // ROPE_DIM, 2, ROPE_DIM // 2)
    return w[:, :, ::-1, :].reshape(k, c)


def _prep_even(w_in_a, w_uq, w_ukv, w_out_a, dc):
    d = w_in_a.shape[0]
    kpe = w_in_a[:, -ROPE_DIM:]
    w_in = jnp.concatenate([w_in_a, _swap_halves(kpe)], axis=1).astype(BF16)
    ql = w_uq.shape[0]
    wq = w_uq.reshape(ql, MLA_HEADS, NOPE_DIM + ROPE_DIM)
    wqn = wq[:, :, :NOPE_DIM].reshape(ql, MLA_HEADS * NOPE_DIM).astype(BF16)
    wqp = wq[:, :, NOPE_DIM:].reshape(ql, MLA_HEADS * ROPE_DIM)
    wqs = _swap_halves(wqp).astype(BF16)
    kl = w_ukv.shape[0]
    wkv = w_ukv.reshape(kl, MLA_HEADS, NOPE_DIM + V_DIM)
    wuk = jnp.transpose(wkv[:, :, :NOPE_DIM], (1, 2, 0)).astype(BF16)
    wuv = jnp.transpose(wkv[:, :, NOPE_DIM:], (1, 0, 2)).astype(BF16)
    wo = w_out_a.astype(BF16)
    return dict(w_in=w_in, wqn=wqn, wqp=wqp.astype(BF16), wqs=wqs, wuk=wuk, wuv=wuv, wo_a=wo[:dc], wo_b=wo[dc:])


def _run_group(x3, pos, conv_prev, ckv_prev, kpe_prev, hgrn_prev, p):
    b, t, d = x3.shape
    n = b * t
    depth = p["norm_ffn1"].shape[0]
    x = x3.reshape(n, d)
    c1, s1, c8, s8 = _rope_tables(pos)
    conv_new, ckv_new, kpe_new, hgrn_new = [], [], [], []
    for l in range(depth):
        x = _ffn(x, p["norm_ffn1"][l], p["wg1"][l], p["wu1"][l], p["wd1"][l])
        if l % 2 == 0:
            e = l // 2
            pe = p["even"][e]
            dc = p["w_conv"].shape[2]
            z = _norm_matmul(x, p["norm_mix"][l], pe["w_in"]).reshape(b, t, -1)
            cprev = jnp.zeros((b, CONV_W - 1, dc), F32) if conv_prev is None else conv_prev[e]
            ya, cs, q, ckv, kpe, kv = _evenpost(z, cprev, p["w_conv"][e], p["g_q"][e], pe["wqn"], pe["wqp"],
                                                pe["wqs"], pe["wuk"], p["g_kv"][e], c1, s1, c8, s8)
            if ckv_prev is None:
                yb = _attention(q, kv, pe["wuv"], causal=True, kv_len=t)
            else:
                past = ckv_prev.shape[2]
                kv_len = past + t
                tk = 512
                pad = (-kv_len) % tk
                cache = jnp.concatenate([ckv_prev[e], kpe_prev[e]], axis=-1).astype(BF16)
                kv_all = jnp.concatenate([cache, kv, jnp.zeros((b, pad, kv.shape[2]), BF16)], axis=1)
                yb = _attention(q, kv_all, pe["wuv"], causal=False, kv_len=kv_len)
            x = _matmul_residual([ya.reshape(n, -1), yb.reshape(n, -1)], [pe["wo_a"], pe["wo_b"]], x)
            conv_new.append(cs)
            ckv_new.append(ckv)
            kpe_new.append(kpe)
        else:
            o = l // 2
            z = _norm_matmul(x, p["norm_mix"][l], p["w_in_c"][o]).reshape(b, t, -1)
            if hgrn_prev is None:
                s0t = jnp.zeros((b, HG_HEADS, HG_DV, HG_DK), F32)
            else:
                s0t = jnp.swapaxes(hgrn_prev[o], -1, -2)
            og, st = _gla(z, p["lb_logits"], o, p["g_o"][o], s0t)
            x = _matmul_residual([og.reshape(n, -1)], [p["w_out_c"][o]], x)
            hgrn_new.append(jnp.swapaxes(st, -1, -2))
        g_final = p["norm_final"] if l == depth - 1 else None
        x = _ffn(x, p["norm_ffn2"][l], p["wg2"][l], p["wu2"][l], p["wd2"][l], g_final)
    return x.reshape(b, t, d), jnp.stack(conv_new), jnp.stack(ckv_new), jnp.stack(kpe_new), jnp.stack(hgrn_new)


def kernel(x_prompt, x_sample, cache_conv, cache_ckv, cache_kpe, state_hgrn, norm_ffn1, w_ffn1_gate, w_ffn1_up, w_ffn1_down, norm_mix, w_in_a, w_conv, g_q, w_uq, g_kv, w_ukv, w_out_a, w_in_c, lb_logits, g_o, w_out_c, norm_ffn2, w_ffn2_gate, w_ffn2_up, w_ffn2_down, norm_final):
    dc = w_conv.shape[2]
    p = dict(
        norm_ffn1=norm_ffn1, wg1=w_ffn1_gate.astype(BF16), wu1=w_ffn1_up.astype(BF16), wd1=w_ffn1_down.astype(BF16),
        norm_ffn2=norm_ffn2, wg2=w_ffn2_gate.astype(BF16), wu2=w_ffn2_up.astype(BF16), wd2=w_ffn2_down.astype(BF16),
        norm_mix=norm_mix, w_conv=w_conv, g_q=g_q, g_kv=g_kv,
        even=[_prep_even(w_in_a[e], w_uq[e], w_ukv[e], w_out_a[e], dc) for e in range(w_in_a.shape[0])],
        w_in_c=w_in_c.astype(BF16), lb_logits=lb_logits, g_o=g_o, w_out_c=w_out_c.astype(BF16),
        norm_final=norm_final,
    )
    pos_p = jnp.arange(x_prompt.shape[1], dtype=jnp.int32)
    pos_s = cache_ckv.shape[2] + jnp.arange(x_sample.shape[1], dtype=jnp.int32)
    y_p, conv_p, ckv_p, kpe_p, hgrn_p = _run_group(x_prompt, pos_p, None, None, None, None, p)
    y_s, conv_s, ckv_s, kpe_s, hgrn_s = _run_group(x_sample, pos_s, cache_conv, cache_ckv, cache_kpe, state_hgrn, p)
    return (y_p, y_s, conv_p, ckv_p, kpe_p, hgrn_p, conv_s, ckv_s, kpe_s, hgrn_s)
```

```python
import functools

import numpy as np
import jax
import jax.numpy as jnp
from jax import lax
from jax.experimental import pallas as pl
from jax.experimental.pallas import tpu as pltpu

F32 = jnp.float32
BF16 = jnp.bfloat16

EPS = 1e-6
CHUNK = 64
CONV_W = 3
MLA_HEADS = 8
NOPE_DIM = 128
ROPE_DIM = 64
V_DIM = 128
ROPE_THETA = 10000.0
MLA_SCALE = (NOPE_DIM + ROPE_DIM) ** -0.5
HG_HEADS = 16
HG_DK = 128
HG_DV = 128
LANES = 128
SUBLANES = 8
NEG_BIG = -1e30
LOG2E = 1.4426950408889634
LEVELS = (32, 16, 8, 4, 2)
VMEM_LIMIT_BYTES = 60 * 1024 * 1024


def _tile(n, pref):
    if n <= pref:
        return n
    for t in range(pref, 7, -8):
        if n % t == 0:
            return t
    return n


def _col_tile(n, pref):
    for t in range(min(pref, n) // LANES * LANES, 0, -LANES):
        if n % t == 0:
            return t
    return n


def _params(*sem, **kw):
    return pltpu.CompilerParams(dimension_semantics=sem, vmem_limit_bytes=VMEM_LIMIT_BYTES, **kw)


def _sigmoid(x):
    return 1.0 / (1.0 + jnp.exp(-x))


def _silu(x):
    h = 0.5 * x
    return h + h * jnp.tanh(h)


def _rms(x, g):
    return x * lax.rsqrt(jnp.mean(x * x, axis=-1, keepdims=True) + EPS) * g


def _dot(a, b):
    return jnp.dot(a, b, preferred_element_type=F32)


def _dot_nt(a, b):
    return lax.dot_general(a, b, (((1,), (1,)), ((), ())), preferred_element_type=F32)


def _dot_tn(a, b):
    return lax.dot_general(a, b, (((0,), (0,)), ((), ())), preferred_element_type=F32)


def _ffn_kernel(x_ref, g_ref, wg_ref, wu_ref, wd_ref, *rest, final_norm):
    if final_norm:
        gf_ref, o_ref, n_ref = rest
    else:
        o_ref, n_ref = rest
    j = pl.program_id(1)
    tm = x_ref.shape[0]
    rc = min(tm, 256)

    @pl.when(j == 0)
    def _():
        for r in range(0, tm, rc):
            n_ref[r:r + rc, :] = _rms(x_ref[r:r + rc, :], g_ref[...]).astype(BF16)
        o_ref[...] = jnp.zeros_like(o_ref)

    n = n_ref[...]
    hg = _dot(n, wg_ref[...])
    hu = _dot(n, wu_ref[...])
    a = _silu(hg) * hu
    o_ref[...] += _dot(a.astype(BF16), wd_ref[...])

    @pl.when(j == pl.num_programs(1) - 1)
    def _():
        for r in range(0, tm, rc):
            y = x_ref[r:r + rc, :] + 0.5 * o_ref[r:r + rc, :]
            if final_norm:
                y = _rms(y, gf_ref[...])
            o_ref[r:r + rc, :] = y


def _ffn(x, g, wg, wu, wd, g_final=None):
    n, d = x.shape
    f = wg.shape[1]
    tm = _tile(n, 1024)
    tf = _col_tile(f, 256)
    in_specs = [
        pl.BlockSpec((tm, d), lambda i, j: (i, 0)),
        pl.BlockSpec((1, d), lambda i, j: (0, 0)),
        pl.BlockSpec((d, tf), lambda i, j: (0, j)),
        pl.BlockSpec((d, tf), lambda i, j: (0, j)),
        pl.BlockSpec((tf, d), lambda i, j: (j, 0)),
    ]
    args = [x, g.reshape(1, d), wg, wu, wd]
    if g_final is not None:
        in_specs.append(pl.BlockSpec((1, d), lambda i, j: (0, 0)))
        args.append(g_final.reshape(1, d))
    return pl.pallas_call(
        functools.partial(_ffn_kernel, final_norm=g_final is not None),
        grid=(n // tm, f // tf),
        in_specs=in_specs,
        out_specs=pl.BlockSpec((tm, d), lambda i, j: (i, 0)),
        out_shape=jax.ShapeDtypeStruct((n, d), F32),
        scratch_shapes=[pltpu.VMEM((tm, d), BF16)],
        compiler_params=_params("parallel", "arbitrary"),
        name="ffn",
    )(*args)


def _nmm_kernel(x_ref, g_ref, w_ref, o_ref, n_ref):
    tm = x_ref.shape[0]
    rc = min(tm, 256)

    @pl.when(pl.program_id(1) == 0)
    def _():
        for r in range(0, tm, rc):
            n_ref[r:r + rc, :] = _rms(x_ref[r:r + rc, :], g_ref[...]).astype(BF16)

    o_ref[...] = _dot(n_ref[...], w_ref[...])


def _norm_matmul(x, g, w):
    n, d = x.shape
    c = w.shape[1]
    tm = _tile(n, 1024)
    tn = _col_tile(c, 1536)
    return pl.pallas_call(
        _nmm_kernel,
        grid=(n // tm, c // tn),
        in_specs=[
            pl.BlockSpec((tm, d), lambda i, j: (i, 0)),
            pl.BlockSpec((1, d), lambda i, j: (0, 0)),
            pl.BlockSpec((d, tn), lambda i, j: (0, j)),
        ],
        out_specs=pl.BlockSpec((tm, tn), lambda i, j: (i, j)),
        out_shape=jax.ShapeDtypeStruct((n, c), F32),
        scratch_shapes=[pltpu.VMEM((tm, d), BF16)],
        compiler_params=_params("parallel", "arbitrary"),
        name="norm_matmul",
    )(x, g.reshape(1, d), w)


def _mmres_kernel(*refs, n_in):
    x_ref, o_ref = refs[2 * n_in], refs[2 * n_in + 1]
    acc = x_ref[...]
    for a_ref, w_ref in zip(refs[:n_in], refs[n_in:2 * n_in]):
        acc = acc + _dot(a_ref[...], w_ref[...])
    o_ref[...] = acc


def _matmul_residual(acts, weights, x):
    n, d = x.shape
    tm = _tile(n, 512)
    in_specs = [pl.BlockSpec((tm, a.shape[1]), lambda i: (i, 0)) for a in acts]
    in_specs += [pl.BlockSpec(w.shape, lambda i: (0, 0)) for w in weights]
    in_specs.append(pl.BlockSpec((tm, d), lambda i: (i, 0)))
    return pl.pallas_call(
        functools.partial(_mmres_kernel, n_in=len(acts)),
        grid=(n // tm,),
        in_specs=in_specs,
        out_specs=pl.BlockSpec((tm, d), lambda i: (i, 0)),
        out_shape=jax.ShapeDtypeStruct((n, d), F32),
        compiler_params=_params("parallel"),
        name="matmul_residual",
    )(*acts, *weights, x)


def _even_in_kernel(x_ref, gn_ref, win_ref, cp_ref, wc_ref, gq_ref, wqn_ref, wqp_ref, wqs_ref, wuk_ref, gkv_ref,
                    c1_ref, s1_ref, c8_ref, s8_ref,
                    ya_ref, cs_ref, q_ref, ckv_ref, kpe_ref, kv_ref, vp_ref, *, dc, ql, kl):
    t = pl.program_id(1)
    tt = x_ref.shape[1]
    o3 = 3 * dc
    o4 = o3 + ql
    o5 = o4 + kl
    n = _rms(x_ref[0], gn_ref[...]).astype(BF16)

    def proj(c0, c1):
        return _dot(n, win_ref[:, c0:c1])

    v = proj(dc, 2 * dc) * proj(2 * dc, o3)
    cp = cp_ref[0]
    first = t == 0
    pm2 = jnp.where(first, cp[0:1], vp_ref[SUBLANES - 2:SUBLANES - 1, :])
    pm1 = jnp.where(first, cp[1:2], vp_ref[SUBLANES - 1:SUBLANES, :])
    rows = lax.broadcasted_iota(jnp.int32, v.shape, 0)
    v1 = jnp.where(rows == 0, pm1, pltpu.roll(v, 1, 0))
    v2 = jnp.where(rows == 0, pm2, jnp.where(rows == 1, pm1, pltpu.roll(v, 2, 0)))
    w = wc_ref[...]
    conv = v2 * w[0:1] + v1 * w[1:2] + v * w[2:3]
    ya_ref[0] = (proj(0, dc) * conv).astype(BF16)
    vp_ref[...] = v[tt - SUBLANES:tt]

    @pl.when(t == pl.num_programs(1) - 1)
    def _():
        cs_ref[0] = v[tt - (CONV_W - 1):tt]

    cqn = _rms(proj(o3, o4), gq_ref[...]).astype(BF16)
    qn = _dot(cqn, wqn_ref[...])
    qpr = _dot(cqn, wqp_ref[...]) * c8_ref[...] + _dot(cqn, wqs_ref[...]) * s8_ref[...]
    for h in range(MLA_HEADS):
        qlat = _dot(qn[:, h * NOPE_DIM:(h + 1) * NOPE_DIM].astype(BF16), wuk_ref[h])
        q_ref[0, h, :, 0:kl] = qlat.astype(BF16)
        q_ref[0, h, :, kl:kl + ROPE_DIM] = qpr[:, h * ROPE_DIM:(h + 1) * ROPE_DIM].astype(BF16)
    cn = _rms(proj(o4, o5), gkv_ref[...])
    kp = proj(o5, o5 + 2 * ROPE_DIM)
    kr = kp[:, 0:ROPE_DIM] * c1_ref[...] + kp[:, ROPE_DIM:2 * ROPE_DIM] * s1_ref[...]
    ckv_ref[0] = cn
    kpe_ref[0] = kr
    kv_ref[0, :, 0:kl] = cn.astype(BF16)
    kv_ref[0, :, kl:kl + ROPE_DIM] = kr.astype(BF16)


def _even_in(x, g_norm, w_in, conv_prev, w_conv, g_q, wqn, wqp, wqs, wuk, g_kv, c1, s1, c8, s8):
    b, t, d = x.shape
    dc = w_conv.shape[1]
    ql = g_q.shape[0]
    kl = g_kv.shape[0]
    tt = _tile(t, 256)
    hr = MLA_HEADS * ROPE_DIM
    once = lambda a: pl.BlockSpec(a.shape, lambda i, j: (0,) * a.ndim, pipeline_mode=pl.Buffered(1))
    wts = [g_norm.reshape(1, d), w_in]
    wts2 = [w_conv, g_q.reshape(1, ql), wqn, wqp, wqs, wuk, g_kv.reshape(1, kl)]
    in_specs = [pl.BlockSpec((1, tt, d), lambda i, j: (i, j, 0))] + [once(a) for a in wts] + [
        pl.BlockSpec((1, CONV_W - 1, dc), lambda i, j: (i, 0, 0)),
    ] + [once(a) for a in wts2] + [
        pl.BlockSpec((tt, ROPE_DIM), lambda i, j: (j, 0)),
        pl.BlockSpec((tt, ROPE_DIM), lambda i, j: (j, 0)),
        pl.BlockSpec((tt, hr), lambda i, j: (j, 0)),
        pl.BlockSpec((tt, hr), lambda i, j: (j, 0)),
    ]
    out_shape = [
        jax.ShapeDtypeStruct((b, t, dc), BF16),
        jax.ShapeDtypeStruct((b, CONV_W - 1, dc), F32),
        jax.ShapeDtypeStruct((b, MLA_HEADS, t, kl + ROPE_DIM), BF16),
        jax.ShapeDtypeStruct((b, t, kl), F32),
        jax.ShapeDtypeStruct((b, t, ROPE_DIM), F32),
        jax.ShapeDtypeStruct((b, t, kl + ROPE_DIM), BF16),
    ]
    out_specs = [
        pl.BlockSpec((1, tt, dc), lambda i, j: (i, j, 0)),
        pl.BlockSpec((1, CONV_W - 1, dc), lambda i, j: (i, 0, 0)),
        pl.BlockSpec((1, MLA_HEADS, tt, kl + ROPE_DIM), lambda i, j: (i, 0, j, 0)),
        pl.BlockSpec((1, tt, kl), lambda i, j: (i, j, 0)),
        pl.BlockSpec((1, tt, ROPE_DIM), lambda i, j: (i, j, 0)),
        pl.BlockSpec((1, tt, kl + ROPE_DIM), lambda i, j: (i, j, 0)),
    ]
    return pl.pallas_call(
        functools.partial(_even_in_kernel, dc=dc, ql=ql, kl=kl),
        grid=(b, t // tt),
        in_specs=in_specs,
        out_specs=out_specs,
        out_shape=out_shape,
        scratch_shapes=[pltpu.VMEM((SUBLANES, dc), F32)],
        compiler_params=_params("parallel", "arbitrary"),
        name="even_in",
    )(x, *wts, conv_prev, *wts2, c1, s1, c8, s8)


def _attn_kernel(q_ref, kv_ref, wuv_ref, o_ref, *, tq, tk, causal, kv_len, kl):
    i = pl.program_id(1)
    rows = MLA_HEADS * tq
    q = q_ref[0].reshape(rows, q_ref.shape[3])
    if causal:
        tok = i * tq + lax.rem(lax.broadcasted_iota(jnp.int32, (rows, 1), 0), tq)
        limit = (tok // CHUNK + 1) * CHUNK
    else:
        limit = kv_len
    c = MLA_SCALE * LOG2E

    def scores(j):
        return _dot_nt(q, kv_ref[0, j * tk:(j + 1) * tk, :])

    def run(n):
        s = scores(0)
        m = l = acc = None
        for j in range(n):
            s_next = scores(j + 1) if j + 1 < n else None
            if j == n - 1:
                kpos = j * tk + lax.broadcasted_iota(jnp.int32, (1, tk), 1)
                s = jnp.where(kpos < limit, s, NEG_BIG)
            m_new = jnp.max(s, axis=-1, keepdims=True)
            if m is not None:
                m_new = jnp.maximum(m, m_new)
            p = jnp.exp2((s - m_new) * c)
            p_sum = jnp.sum(p, axis=-1, keepdims=True)
            pv = _dot(p.astype(BF16), kv_ref[0, j * tk:(j + 1) * tk, 0:kl])
            if m is None:
                l, acc = p_sum, pv
            else:
                alpha = jnp.exp2((m - m_new) * c)
                l = alpha * l + p_sum
                acc = alpha * acc + pv
            m, s = m_new, s_next
        o = (acc * (1.0 / l)).astype(BF16)
        for h in range(MLA_HEADS):
            o_ref[0, :, h * V_DIM:(h + 1) * V_DIM] = _dot(o[h * tq:(h + 1) * tq], wuv_ref[h]).astype(BF16)

    if causal:
        n_kv = (i * tq) // tk + 1
        for n in range(1, kv_ref.shape[1] // tk + 1):
            pl.when(n_kv == n)(functools.partial(run, n))
    else:
        run((kv_len + tk - 1) // tk)


def _attention(q, kv, wuv, *, causal, kv_len):
    b, h, t, dq = q.shape
    tkv = kv.shape[1]
    kl = wuv.shape[1]
    tq = _tile(t, 128)
    tk = _tile(tkv, 512)
    assert tk % tq == 0 and tkv % tk == 0 and tq % CHUNK == 0
    return pl.pallas_call(
        functools.partial(_attn_kernel, tq=tq, tk=tk, causal=causal, kv_len=kv_len, kl=kl),
        grid=(b, t // tq),
        in_specs=[
            pl.BlockSpec((1, h, tq, dq), lambda i, j: (i, 0, j, 0)),
            pl.BlockSpec((1, tkv, dq), lambda i, j: (i, 0, 0)),
            pl.BlockSpec(wuv.shape, lambda i, j: (0, 0, 0)),
        ],
        out_specs=pl.BlockSpec((1, tq, h * V_DIM), lambda i, j: (i, j, 0)),
        out_shape=jax.ShapeDtypeStruct((b, t, h * V_DIM), BF16),
        compiler_params=_params("parallel", "arbitrary"),
        name="mla_attention",
    )(q, kv, wuv)


def _level_masks():
    t = np.arange(CHUNK)[:, None]
    s = np.arange(CHUNK)[None, :]
    masks = [(t // (2 * m) == s // (2 * m)) & ((t & m) != 0) & ((s & m) == 0) for m in LEVELS]
    return np.stack(masks).astype(np.float32)


def _boundary_rows(a, m):
    w = a.shape[1]
    if m >= SUBLANES:
        parts = [jnp.broadcast_to(a[b0 + m - 1:b0 + m], (2 * m, w)) for b0 in range(0, CHUNK, 2 * m)]
        return parts[0] if len(parts) == 1 else jnp.concatenate(parts, axis=0)
    a3 = a.reshape(CHUNK // SUBLANES, SUBLANES, w)
    sub = lax.broadcasted_iota(jnp.int32, a3.shape, 1)
    out = None
    for b0 in range(SUBLANES - 2 * m, -1, -2 * m):
        piece = jnp.broadcast_to(a3[:, b0 + m - 1:b0 + m, :], a3.shape)
        out = piece if out is None else jnp.where(sub < b0 + 2 * m, piece, out)
    return out.reshape(CHUNK, w)


def _pick_halves(first, second, m, is_second):
    if m < SUBLANES:
        return jnp.where(is_second, second, first)
    parts = []
    for b0 in range(0, CHUNK, 2 * m):
        parts += [first[b0:b0 + m], second[b0 + m:b0 + 2 * m]]
    return jnp.concatenate(parts, axis=0)


def _gla_kernel(zq_ref, zf_ref, zi_ref, zg_ref, lbl_ref, go_ref, s0_ref, mk_ref, o_ref, st_ref, *, layer):
    t = pl.program_id(1)
    tc = zq_ref.shape[1]
    n_lv = len(LEVELS)

    @pl.when(t == 0)
    def _():
        st_ref[...] = s0_ref[...]

    ll = lbl_ref[...]
    e = jnp.exp(ll - jnp.max(ll, axis=0, keepdims=True))
    sm = e / jnp.sum(e, axis=0, keepdims=True)
    lb = jnp.zeros((1, ll.shape[1]), F32)
    for r in range(1, layer + 1):
        lb = lb + sm[r:r + 1]
    one_m_lb = 1.0 - lb
    go = go_ref[...]
    row = lax.broadcasted_iota(jnp.int32, (CHUNK, 1), 0)
    second = [(row & m) != 0 for m in LEVELS]
    odd = (row & 1) != 0

    def wide(c):
        rs = pl.ds(pl.multiple_of(c * CHUNK, CHUNK), CHUNK)
        zq = zq_ref[0, rs, :]
        zf = zf_ref[0, rs, :]
        v = zi_ref[0, rs, :]
        zg = zg_ref[0, rs, :]
        sig = _sigmoid(zf)
        g = jnp.log(lb + one_m_lb * sig) * LOG2E
        k = one_m_lb * (1.0 - sig)
        q = _silu(zq)
        gate = _silu(zg)
        a = g
        for sh in (1, 2, 4):
            a = a + jnp.where(row >= sh, pltpu.roll(a, sh, 0), 0.0)
        for sh in (8, 16, 32):
            a = a + jnp.concatenate([jnp.zeros((sh, a.shape[1]), F32), a[:CHUNK - sh]], axis=0)
        e_q = jnp.exp2(a)
        e_k = jnp.exp2(a[CHUNK - 1:CHUNK] - a)
        e_last = e_q[CHUNK - 1:CHUNK]
        qt = (q * e_q).astype(BF16)
        kt = (k * e_k).astype(BF16)
        qk = q * k
        vb = v.astype(BF16)
        pair = q * jnp.exp2(g) * pltpu.roll(k, 1, 0)
        v_prev = pltpu.roll(v, 1, 0)
        ops = []
        for li, m in enumerate(LEVELS):
            ex = jnp.exp2(-jnp.abs(a - _boundary_rows(a, m)))
            ops.append((_pick_halves(k, q, m, second[li]) * ex).astype(BF16))
        return ops, qt, kt, vb, qk, v, gate, e_last, pair, v_prev

    def heads(c, w, states):
        ops, qt, kt, vb, qk, v, gate, e_last, pair, v_prev = w
        rs = pl.ds(pl.multiple_of(c * CHUNK, CHUNK), CHUNK)
        sls = [slice(h * HG_DK, (h + 1) * HG_DK) for h in range(HG_HEADS)]
        ps = []
        for sl in sls:
            p = _dot_nt(ops[0][:, sl], ops[0][:, sl]) * mk_ref[0]
            for li in range(1, n_lv):
                p = p + _dot_nt(ops[li][:, sl], ops[li][:, sl]) * mk_ref[li]
            ps.append(p.astype(BF16))
        new_states = [states[h] * e_last[:, sl] + _dot_tn(vb[:, sl], kt[:, sl]) for h, sl in enumerate(sls)]
        os_ = [_dot_nt(qt[:, sl], states[h].astype(BF16)) + _dot(ps[h], vb[:, sl]) for h, sl in enumerate(sls)]
        for h, sl in enumerate(sls):
            o = os_[h] + jnp.sum(qk[:, sl], axis=-1, keepdims=True) * v[:, sl]
            c1 = jnp.where(odd, jnp.sum(pair[:, sl], axis=-1, keepdims=True), 0.0)
            o = o + c1 * v_prev[:, sl]
            og = _rms(o, go) * gate[:, sl]
            o_ref[0, rs, sl] = og.astype(BF16)
        return new_states

    def chunk_body(c, carry):
        states = heads(c, wide(c), [st_ref[0, h] for h in range(HG_HEADS)])
        for h in range(HG_HEADS):
            st_ref[0, h] = states[h]
        return carry

    lax.fori_loop(0, tc // CHUNK, chunk_body, 0)


def _gla(z, lb_logits, layer, g_o, s0t):
    b, t, zw = z.shape
    dh = zw // 4
    tc = _tile(t, 256)
    masks = jnp.asarray(_level_masks(), F32)
    zspec = lambda c: pl.BlockSpec((1, tc, dh), lambda i, j, c=c: (i, j, c))
    sspec = pl.BlockSpec((1,) + s0t.shape[1:], lambda i, j: (i, 0, 0, 0))
    return pl.pallas_call(
        functools.partial(_gla_kernel, layer=layer),
        grid=(b, t // tc),
        in_specs=[zspec(0), zspec(1), zspec(2), zspec(3),
                  pl.BlockSpec(lb_logits.shape, lambda i, j: (0, 0)),
                  pl.BlockSpec((1, HG_DV), lambda i, j: (0, 0)),
                  sspec,
                  pl.BlockSpec(masks.shape, lambda i, j: (0, 0, 0))],
        out_specs=[pl.BlockSpec((1, tc, dh), lambda i, j: (i, j, 0)), sspec],
        out_shape=[jax.ShapeDtypeStruct((b, t, dh), BF16), jax.ShapeDtypeStruct(s0t.shape, F32)],
        compiler_params=_params("parallel", "arbitrary"),
        name="hgrn2_recurrence",
    )(z, z, z, z, lb_logits, g_o.reshape(1, HG_DV), s0t, masks)


def _rope_tables(pos):
    half = ROPE_DIM // 2
    inv = ROPE_THETA ** (-jnp.arange(half, dtype=F32) / half)
    ang = pos.astype(F32)[:, None] * inv[None, :]
    cos, sin = jnp.cos(ang), jnp.sin(ang)
    c1 = jnp.concatenate([cos, cos], axis=1)
    s1 = jnp.concatenate([-sin, sin], axis=1)
    return c1, s1, jnp.tile(c1, (1, MLA_HEADS)), jnp.tile(s1, (1, MLA_HEADS))


def _swap_halves(w):
    k, c = w.shape
    w = w.reshape(k, c // ROPE_DIM, 2, ROPE_DIM // 2)
    return w[:, :, ::-1, :].reshape(k, c)


def _prep_even(w_in_a, w_uq, w_ukv, w_out_a, dc):
    d = w_in_a.shape[0]
    kpe = w_in_a[:, -ROPE_DIM:]
    w_in = jnp.concatenate([w_in_a, _swap_halves(kpe)], axis=1).astype(BF16)
    ql = w_uq.shape[0]
    wq = w_uq.reshape(ql, MLA_HEADS, NOPE_DIM + ROPE_DIM)
    wqn = wq[:, :, :NOPE_DIM].reshape(ql, MLA_HEADS * NOPE_DIM).astype(BF16)
    wqp = wq[:, :, NOPE_DIM:].reshape(ql, MLA_HEADS * ROPE_DIM)
    wqs = _swap_halves(wqp).astype(BF16)
    kl = w_ukv.shape[0]
    wkv = w_ukv.reshape(kl, MLA_HEADS, NOPE_DIM + V_DIM)
    wuk = jnp.transpose(wkv[:, :, :NOPE_DIM], (1, 2, 0)).astype(BF16)
    wuv = jnp.transpose(wkv[:, :, NOPE_DIM:], (1, 0, 2)).astype(BF16)
    wo = w_out_a.astype(BF16)
    return dict(w_in=w_in, wqn=wqn, wqp=wqp.astype(BF16), wqs=wqs, wuk=wuk, wuv=wuv, wo_a=wo[:dc], wo_b=wo[dc:])


def _run_group(x3, pos, conv_prev, ckv_prev, kpe_prev, hgrn_prev, p):
    b, t, d = x3.shape
    n = b * t
    depth = p["norm_ffn1"].shape[0]
    x = x3.reshape(n, d)
    c1, s1, c8, s8 = _rope_tables(pos)
    conv_new, ckv_new, kpe_new, hgrn_new = [], [], [], []
    for l in range(depth):
        x = _ffn(x, p["norm_ffn1"][l], p["wg1"][l], p["wu1"][l], p["wd1"][l])
        if l % 2 == 0:
            e = l // 2
            pe = p["even"][e]
            dc = p["w_conv"].shape[2]
            cprev = jnp.zeros((b, CONV_W - 1, dc), F32) if conv_prev is None else conv_prev[e]
            ya, cs, q, ckv, kpe, kv = _even_in(x.reshape(b, t, d), p["norm_mix"][l], pe["w_in"], cprev,
                                               p["w_conv"][e], p["g_q"][e], pe["wqn"], pe["wqp"], pe["wqs"],
                                               pe["wuk"], p["g_kv"][e], c1, s1, c8, s8)
            if ckv_prev is None:
                yb = _attention(q, kv, pe["wuv"], causal=True, kv_len=t)
            else:
                past = ckv_prev.shape[2]
                kv_len = past + t
                tk = 512
                pad = (-kv_len) % tk
                cache = jnp.concatenate([ckv_prev[e], kpe_prev[e]], axis=-1).astype(BF16)
                kv_all = jnp.concatenate([cache, kv, jnp.zeros((b, pad, kv.shape[2]), BF16)], axis=1)
                yb = _attention(q, kv_all, pe["wuv"], causal=False, kv_len=kv_len)
            x = _matmul_residual([ya.reshape(n, -1), yb.reshape(n, -1)], [pe["wo_a"], pe["wo_b"]], x)
            conv_new.append(cs)
            ckv_new.append(ckv)
            kpe_new.append(kpe)
        else:
            o = l // 2
            z = _norm_matmul(x, p["norm_mix"][l], p["w_in_c"][o]).reshape(b, t, -1)
            if hgrn_prev is None:
                s0t = jnp.zeros((b, HG_HEADS, HG_DV, HG_DK), F32)
            else:
                s0t = jnp.swapaxes(hgrn_prev[o], -1, -2)
            og, st = _gla(z, p["lb_logits"], o, p["g_o"][o], s0t)
            x = _matmul_residual([og.reshape(n, -1)], [p["w_out_c"][o]], x)
            hgrn_new.append(jnp.swapaxes(st, -1, -2))
        g_final = p["norm_final"] if l == depth - 1 else None
        x = _ffn(x, p["norm_ffn2"][l], p["wg2"][l], p["wu2"][l], p["wd2"][l], g_final)
    return x.reshape(b, t, d), jnp.stack(conv_new), jnp.stack(ckv_new), jnp.stack(kpe_new), jnp.stack(hgrn_new)


def kernel(x_prompt, x_sample, cache_conv, cache_ckv, cache_kpe, state_hgrn, norm_ffn1, w_ffn1_gate, w_ffn1_up, w_ffn1_down, norm_mix, w_in_a, w_conv, g_q, w_uq, g_kv, w_ukv, w_out_a, w_in_c, lb_logits, g_o, w_out_c, norm_ffn2, w_ffn2_gate, w_ffn2_up, w_ffn2_down, norm_final):
    dc = w_conv.shape[2]
    p = dict(
        norm_ffn1=norm_ffn1, wg1=w_ffn1_gate.astype(BF16), wu1=w_ffn1_up.astype(BF16), wd1=w_ffn1_down.astype(BF16),
        norm_ffn2=norm_ffn2, wg2=w_ffn2_gate.astype(BF16), wu2=w_ffn2_up.astype(BF16), wd2=w_ffn2_down.astype(BF16),
        norm_mix=norm_mix, w_conv=w_conv, g_q=g_q, g_kv=g_kv,
        even=[_prep_even(w_in_a[e], w_uq[e], w_ukv[e], w_out_a[e], dc) for e in range(w_in_a.shape[0])],
        w_in_c=w_in_c.astype(BF16), lb_logits=lb_logits, g_o=g_o, w_out_c=w_out_c.astype(BF16),
        norm_final=norm_final,
    )
    pos_p = jnp.arange(x_prompt.shape[1], dtype=jnp.int32)
    pos_s = cache_ckv.shape[2] + jnp.arange(x_sample.shape[1], dtype=jnp.int32)
    y_p, conv_p, ckv_p, kpe_p, hgrn_p = _run_group(x_prompt, pos_p, None, None, None, None, p)
    y_s, conv_s, ckv_s, kpe_s, hgrn_s = _run_group(x_sample, pos_s, cache_conv, cache_ckv, cache_kpe, state_hgrn, p)
    return (y_p, y_s, conv_p, ckv_p, kpe_p, hgrn_p, conv_s, ckv_s, kpe_s, hgrn_s)
```

```python
import functools

import numpy as np
import jax
import jax.numpy as jnp
from jax import lax
from jax.experimental import pallas as pl
from jax.experimental.pallas import tpu as pltpu

F32 = jnp.float32
BF16 = jnp.bfloat16

EPS = 1e-6
CHUNK = 64
CONV_W = 3
MLA_HEADS = 8
NOPE_DIM = 128
ROPE_DIM = 64
V_DIM = 128
ROPE_THETA = 10000.0
MLA_SCALE = (NOPE_DIM + ROPE_DIM) ** -0.5
HG_HEADS = 16
HG_DK = 128
HG_DV = 128
LANES = 128
SUBLANES = 8
NEG_BIG = -1e30
LOG2E = 1.4426950408889634
LEVELS = (32, 16, 8, 4, 2)
VMEM_LIMIT_BYTES = 60 * 1024 * 1024


def _tile(n, pref):
    if n <= pref:
        return n
    for t in range(pref, 7, -8):
        if n % t == 0:
            return t
    return n


def _col_tile(n, pref):
    for t in range(min(pref, n) // LANES * LANES, 0, -LANES):
        if n % t == 0:
            return t
    return n


def _params(*sem, **kw):
    return pltpu.CompilerParams(dimension_semantics=sem, vmem_limit_bytes=VMEM_LIMIT_BYTES, **kw)


def _sigmoid(x):
    return 1.0 / (1.0 + jnp.exp(-x))


def _silu(x):
    h = 0.5 * x
    return h + h * jnp.tanh(h)


def _rms(x, g):
    return x * lax.rsqrt(jnp.mean(x * x, axis=-1, keepdims=True) + EPS) * g


def _dot(a, b):
    return jnp.dot(a, b, preferred_element_type=F32)


def _dot_nt(a, b):
    return lax.dot_general(a, b, (((1,), (1,)), ((), ())), preferred_element_type=F32)


def _dot_tn(a, b):
    return lax.dot_general(a, b, (((0,), (0,)), ((), ())), preferred_element_type=F32)


def _ffn_kernel(x_ref, g_ref, wg_ref, wu_ref, wd_ref, *rest, final_norm):
    if final_norm:
        gf_ref, o_ref, n_ref = rest
    else:
        o_ref, n_ref = rest
    j = pl.program_id(1)
    tm = x_ref.shape[0]
    rc = min(tm, 128)

    def rows(i):
        return pl.ds(pl.multiple_of(i * rc, rc), rc)

    @pl.when(j == 0)
    def _():
        def norm_rows(i, carry):
            n_ref[rows(i), :] = _rms(x_ref[rows(i), :], g_ref[...]).astype(BF16)
            return carry

        lax.fori_loop(0, tm // rc, norm_rows, 0)
        o_ref[...] = jnp.zeros_like(o_ref)

    n = n_ref[...]
    hg = _dot(n, wg_ref[...].astype(BF16))
    hu = _dot(n, wu_ref[...].astype(BF16))
    a = _silu(hg) * hu
    o_ref[...] += _dot(a.astype(BF16), wd_ref[...].astype(BF16))

    @pl.when(j == pl.num_programs(1) - 1)
    def _():
        def finish_rows(i, carry):
            y = x_ref[rows(i), :] + 0.5 * o_ref[rows(i), :]
            if final_norm:
                y = _rms(y, gf_ref[...])
            o_ref[rows(i), :] = y
            return carry

        lax.fori_loop(0, tm // rc, finish_rows, 0)


def _ffn(x, g, wg, wu, wd, g_final=None):
    n, d = x.shape
    f = wg.shape[1]
    tm = _tile(n, 1024)
    tf = _col_tile(f, 256)
    in_specs = [
        pl.BlockSpec((tm, d), lambda i, j: (i, 0)),
        pl.BlockSpec((1, d), lambda i, j: (0, 0)),
        pl.BlockSpec((d, tf), lambda i, j: (0, j)),
        pl.BlockSpec((d, tf), lambda i, j: (0, j)),
        pl.BlockSpec((tf, d), lambda i, j: (j, 0)),
    ]
    args = [x, g.reshape(1, d), wg, wu, wd]
    if g_final is not None:
        in_specs.append(pl.BlockSpec((1, d), lambda i, j: (0, 0)))
        args.append(g_final.reshape(1, d))
    return pl.pallas_call(
        functools.partial(_ffn_kernel, final_norm=g_final is not None),
        grid=(n // tm, f // tf),
        in_specs=in_specs,
        out_specs=pl.BlockSpec((tm, d), lambda i, j: (i, 0)),
        out_shape=jax.ShapeDtypeStruct((n, d), F32),
        scratch_shapes=[pltpu.VMEM((tm, d), BF16)],
        compiler_params=_params("parallel", "arbitrary"),
        name="ffn",
    )(*args)


def _nmm_kernel(x_ref, g_ref, w_ref, o_ref, n_ref):
    tm = x_ref.shape[0]
    rc = min(tm, 128)

    @pl.when(pl.program_id(1) == 0)
    def _():
        def norm_rows(i, carry):
            rows = pl.ds(pl.multiple_of(i * rc, rc), rc)
            n_ref[rows, :] = _rms(x_ref[rows, :], g_ref[...]).astype(BF16)
            return carry

        lax.fori_loop(0, tm // rc, norm_rows, 0)

    o_ref[...] = _dot(n_ref[...], w_ref[...])


def _norm_matmul(x, g, w):
    n, d = x.shape
    c = w.shape[1]
    tm = _tile(n, 1024)
    tn = _col_tile(c, 2048)
    return pl.pallas_call(
        _nmm_kernel,
        grid=(n // tm, c // tn),
        in_specs=[
            pl.BlockSpec((tm, d), lambda i, j: (i, 0)),
            pl.BlockSpec((1, d), lambda i, j: (0, 0)),
            pl.BlockSpec((d, tn), lambda i, j: (0, j)),
        ],
        out_specs=pl.BlockSpec((tm, tn), lambda i, j: (i, j)),
        out_shape=jax.ShapeDtypeStruct((n, c), F32),
        scratch_shapes=[pltpu.VMEM((tm, d), BF16)],
        compiler_params=_params("parallel", "arbitrary"),
        name="norm_matmul",
    )(x, g.reshape(1, d), w)


def _mmres_kernel(*refs, n_in):
    x_ref, o_ref = refs[2 * n_in], refs[2 * n_in + 1]
    acc = x_ref[...]
    for a_ref, w_ref in zip(refs[:n_in], refs[n_in:2 * n_in]):
        acc = acc + _dot(a_ref[...], w_ref[...])
    o_ref[...] = acc


def _matmul_residual(acts, weights, x):
    n, d = x.shape
    tm = _tile(n, 512)
    in_specs = [pl.BlockSpec((tm, a.shape[1]), lambda i: (i, 0)) for a in acts]
    in_specs += [pl.BlockSpec(w.shape, lambda i: (0, 0)) for w in weights]
    in_specs.append(pl.BlockSpec((tm, d), lambda i: (i, 0)))
    return pl.pallas_call(
        functools.partial(_mmres_kernel, n_in=len(acts)),
        grid=(n // tm,),
        in_specs=in_specs,
        out_specs=pl.BlockSpec((tm, d), lambda i: (i, 0)),
        out_shape=jax.ShapeDtypeStruct((n, d), F32),
        compiler_params=_params("parallel"),
        name="matmul_residual",
    )(*acts, *weights, x)


def _even_in_kernel(x_ref, gn_ref, win_ref, cp_ref, wc_ref, gq_ref, wqn_ref, wqp_ref, wqs_ref, wuk_ref, gkv_ref,
                    c1_ref, s1_ref, c8_ref, s8_ref,
                    ya_ref, cs_ref, q_ref, ckv_ref, kpe_ref, kv_ref, vp_ref, *, dc, ql, kl):
    t = pl.program_id(1)
    tt = x_ref.shape[1]
    o3 = 3 * dc
    o4 = o3 + ql
    o5 = o4 + kl
    n = _rms(x_ref[0], gn_ref[...]).astype(BF16)

    def proj(c0, c1):
        return _dot(n, win_ref[:, c0:c1])

    v = proj(dc, 2 * dc) * proj(2 * dc, o3)
    cp = cp_ref[0]
    first = t == 0
    pm2 = jnp.where(first, cp[0:1], vp_ref[SUBLANES - 2:SUBLANES - 1, :])
    pm1 = jnp.where(first, cp[1:2], vp_ref[SUBLANES - 1:SUBLANES, :])
    rows = lax.broadcasted_iota(jnp.int32, v.shape, 0)
    v1 = jnp.where(rows == 0, pm1, pltpu.roll(v, 1, 0))
    v2 = jnp.where(rows == 0, pm2, jnp.where(rows == 1, pm1, pltpu.roll(v, 2, 0)))
    w = wc_ref[...]
    conv = v2 * w[0:1] + v1 * w[1:2] + v * w[2:3]
    ya_ref[0] = (proj(0, dc) * conv).astype(BF16)
    vp_ref[...] = v[tt - SUBLANES:tt]

    @pl.when(t == pl.num_programs(1) - 1)
    def _():
        cs_ref[0] = v[tt - (CONV_W - 1):tt]

    cqn = _rms(proj(o3, o4), gq_ref[...]).astype(BF16)
    qn = _dot(cqn, wqn_ref[...])
    qpr = _dot(cqn, wqp_ref[...]) * c8_ref[...] + _dot(cqn, wqs_ref[...]) * s8_ref[...]
    for h in range(MLA_HEADS):
        qlat = _dot(qn[:, h * NOPE_DIM:(h + 1) * NOPE_DIM].astype(BF16), wuk_ref[h])
        q_ref[0, h, :, 0:kl] = qlat.astype(BF16)
        q_ref[0, h, :, kl:kl + ROPE_DIM] = qpr[:, h * ROPE_DIM:(h + 1) * ROPE_DIM].astype(BF16)
    cn = _rms(proj(o4, o5), gkv_ref[...])
    kp = proj(o5, o5 + 2 * ROPE_DIM)
    kr = kp[:, 0:ROPE_DIM] * c1_ref[...] + kp[:, ROPE_DIM:2 * ROPE_DIM] * s1_ref[...]
    ckv_ref[0] = cn
    kpe_ref[0] = kr
    kv_ref[0, :, 0:kl] = cn.astype(BF16)
    kv_ref[0, :, kl:kl + ROPE_DIM] = kr.astype(BF16)


def _even_in(x, g_norm, w_in, conv_prev, w_conv, g_q, wqn, wqp, wqs, wuk, g_kv, c1, s1, c8, s8):
    b, t, d = x.shape
    dc = w_conv.shape[1]
    ql = g_q.shape[0]
    kl = g_kv.shape[0]
    tt = _tile(t, 256)
    hr = MLA_HEADS * ROPE_DIM
    once = lambda a: pl.BlockSpec(a.shape, lambda i, j: (0,) * a.ndim, pipeline_mode=pl.Buffered(1))
    wts = [g_norm.reshape(1, d), w_in]
    wts2 = [w_conv, g_q.reshape(1, ql), wqn, wqp, wqs, wuk, g_kv.reshape(1, kl)]
    in_specs = [pl.BlockSpec((1, tt, d), lambda i, j: (i, j, 0))] + [once(a) for a in wts] + [
        pl.BlockSpec((1, CONV_W - 1, dc), lambda i, j: (i, 0, 0)),
    ] + [once(a) for a in wts2] + [
        pl.BlockSpec((tt, ROPE_DIM), lambda i, j: (j, 0)),
        pl.BlockSpec((tt, ROPE_DIM), lambda i, j: (j, 0)),
        pl.BlockSpec((tt, hr), lambda i, j: (j, 0)),
        pl.BlockSpec((tt, hr), lambda i, j: (j, 0)),
    ]
    out_shape = [
        jax.ShapeDtypeStruct((b, t, dc), BF16),
        jax.ShapeDtypeStruct((b, CONV_W - 1, dc), F32),
        jax.ShapeDtypeStruct((b, MLA_HEADS, t, kl + ROPE_DIM), BF16),
        jax.ShapeDtypeStruct((b, t, kl), F32),
        jax.ShapeDtypeStruct((b, t, ROPE_DIM), F32),
        jax.ShapeDtypeStruct((b, t, kl + ROPE_DIM), BF16),
    ]
    out_specs = [
        pl.BlockSpec((1, tt, dc), lambda i, j: (i, j, 0)),
        pl.BlockSpec((1, CONV_W - 1, dc), lambda i, j: (i, 0, 0)),
        pl.BlockSpec((1, MLA_HEADS, tt, kl + ROPE_DIM), lambda i, j: (i, 0, j, 0)),
        pl.BlockSpec((1, tt, kl), lambda i, j: (i, j, 0)),
        pl.BlockSpec((1, tt, ROPE_DIM), lambda i, j: (i, j, 0)),
        pl.BlockSpec((1, tt, kl + ROPE_DIM), lambda i, j: (i, j, 0)),
    ]
    return pl.pallas_call(
        functools.partial(_even_in_kernel, dc=dc, ql=ql, kl=kl),
        grid=(b, t // tt),
        in_specs=in_specs,
        out_specs=out_specs,
        out_shape=out_shape,
        scratch_shapes=[pltpu.VMEM((SUBLANES, dc), F32)],
        compiler_params=_params("parallel", "arbitrary"),
        name="even_in",
    )(x, *wts, conv_prev, *wts2, c1, s1, c8, s8)


def _attn_kernel(q_ref, kv_ref, wuv_ref, o_ref, *, tq, tk, causal, kv_len, kl):
    i = pl.program_id(1)
    rows = MLA_HEADS * tq
    q = q_ref[0].reshape(rows, q_ref.shape[3])
    if causal:
        tok = i * tq + lax.rem(lax.broadcasted_iota(jnp.int32, (rows, 1), 0), tq)
        limit = (tok // CHUNK + 1) * CHUNK
    else:
        limit = kv_len
    c = MLA_SCALE * LOG2E

    def scores(j):
        return _dot_nt(q, kv_ref[0, j * tk:(j + 1) * tk, :])

    def run(n):
        s = scores(0)
        m = l = acc = None
        for j in range(n):
            s_next = scores(j + 1) if j + 1 < n else None
            if j == n - 1:
                kpos = j * tk + lax.broadcasted_iota(jnp.int32, (1, tk), 1)
                s = jnp.where(kpos < limit, s, NEG_BIG)
            m_new = jnp.max(s, axis=-1, keepdims=True)
            if m is not None:
                m_new = jnp.maximum(m, m_new)
            p = jnp.exp2((s - m_new) * c)
            p_sum = jnp.sum(p, axis=-1, keepdims=True)
            pv = _dot(p.astype(BF16), kv_ref[0, j * tk:(j + 1) * tk, 0:kl])
            if m is None:
                l, acc = p_sum, pv
            else:
                alpha = jnp.exp2((m - m_new) * c)
                l = alpha * l + p_sum
                acc = alpha * acc + pv
            m, s = m_new, s_next
        o = (acc * (1.0 / l)).astype(BF16)
        for h in range(MLA_HEADS):
            o_ref[0, :, h * V_DIM:(h + 1) * V_DIM] = _dot(o[h * tq:(h + 1) * tq], wuv_ref[h]).astype(BF16)

    if causal:
        n_kv = (i * tq) // tk + 1
        for n in range(1, kv_ref.shape[1] // tk + 1):
            pl.when(n_kv == n)(functools.partial(run, n))
    else:
        run((kv_len + tk - 1) // tk)


def _attention(q, kv, wuv, *, causal, kv_len):
    b, h, t, dq = q.shape
    tkv = kv.shape[1]
    kl = wuv.shape[1]
    tq = _tile(t, 128)
    tk = _tile(tkv, 512)
    assert tk % tq == 0 and tkv % tk == 0 and tq % CHUNK == 0
    return pl.pallas_call(
        functools.partial(_attn_kernel, tq=tq, tk=tk, causal=causal, kv_len=kv_len, kl=kl),
        grid=(b, t // tq),
        in_specs=[
            pl.BlockSpec((1, h, tq, dq), lambda i, j: (i, 0, j, 0)),
            pl.BlockSpec((1, tkv, dq), lambda i, j: (i, 0, 0)),
            pl.BlockSpec(wuv.shape, lambda i, j: (0, 0, 0)),
        ],
        out_specs=pl.BlockSpec((1, tq, h * V_DIM), lambda i, j: (i, j, 0)),
        out_shape=jax.ShapeDtypeStruct((b, t, h * V_DIM), BF16),
        compiler_params=_params("parallel", "arbitrary"),
        name="mla_attention",
    )(q, kv, wuv)


def _level_masks():
    t = np.arange(CHUNK)[:, None]
    s = np.arange(CHUNK)[None, :]
    masks = [(t // (2 * m) == s // (2 * m)) & ((t & m) != 0) & ((s & m) == 0) for m in LEVELS]
    return np.stack(masks).astype(np.float32)


def _boundary_rows(a, m):
    w = a.shape[1]
    if m >= SUBLANES:
        parts = [jnp.broadcast_to(a[b0 + m - 1:b0 + m], (2 * m, w)) for b0 in range(0, CHUNK, 2 * m)]
        return parts[0] if len(parts) == 1 else jnp.concatenate(parts, axis=0)
    a3 = a.reshape(CHUNK // SUBLANES, SUBLANES, w)
    sub = lax.broadcasted_iota(jnp.int32, a3.shape, 1)
    out = None
    for b0 in range(SUBLANES - 2 * m, -1, -2 * m):
        piece = jnp.broadcast_to(a3[:, b0 + m - 1:b0 + m, :], a3.shape)
        out = piece if out is None else jnp.where(sub < b0 + 2 * m, piece, out)
    return out.reshape(CHUNK, w)


def _pick_halves(first, second, m, is_second):
    if m < SUBLANES:
        return jnp.where(is_second, second, first)
    parts = []
    for b0 in range(0, CHUNK, 2 * m):
        parts += [first[b0:b0 + m], second[b0 + m:b0 + 2 * m]]
    return jnp.concatenate(parts, axis=0)


def _gla_kernel(zq_ref, zf_ref, zi_ref, zg_ref, lbl_ref, go_ref, s0_ref, mk_ref, o_ref, st_ref, *, layer):
    t = pl.program_id(1)
    tc = zq_ref.shape[1]
    n_lv = len(LEVELS)

    @pl.when(t == 0)
    def _():
        st_ref[...] = s0_ref[...]

    ll = lbl_ref[...]
    e = jnp.exp(ll - jnp.max(ll, axis=0, keepdims=True))
    sm = e / jnp.sum(e, axis=0, keepdims=True)
    lb = jnp.zeros((1, ll.shape[1]), F32)
    for r in range(1, layer + 1):
        lb = lb + sm[r:r + 1]
    one_m_lb = 1.0 - lb
    go = go_ref[...]
    row = lax.broadcasted_iota(jnp.int32, (CHUNK, 1), 0)
    second = [(row & m) != 0 for m in LEVELS]
    odd = (row & 1) != 0

    def wide(c):
        rs = pl.ds(pl.multiple_of(c * CHUNK, CHUNK), CHUNK)
        zq = zq_ref[0, rs, :]
        zf = zf_ref[0, rs, :]
        v = zi_ref[0, rs, :]
        zg = zg_ref[0, rs, :]
        sig = _sigmoid(zf)
        g = jnp.log(lb + one_m_lb * sig) * LOG2E
        k = one_m_lb * (1.0 - sig)
        q = _silu(zq)
        gate = _silu(zg)
        a = g
        for sh in (1, 2, 4):
            a = a + jnp.where(row >= sh, pltpu.roll(a, sh, 0), 0.0)
        for sh in (8, 16, 32):
            a = a + jnp.concatenate([jnp.zeros((sh, a.shape[1]), F32), a[:CHUNK - sh]], axis=0)
        e_q = jnp.exp2(a)
        e_k = jnp.exp2(a[CHUNK - 1:CHUNK] - a)
        e_last = e_q[CHUNK - 1:CHUNK]
        qt = (q * e_q).astype(BF16)
        kt = (k * e_k).astype(BF16)
        qk = q * k
        vb = v.astype(BF16)
        pair = q * jnp.exp2(g) * pltpu.roll(k, 1, 0)
        v_prev = pltpu.roll(v, 1, 0)
        ops = []
        for li, m in enumerate(LEVELS):
            ex = jnp.exp2(-jnp.abs(a - _boundary_rows(a, m)))
            ops.append((_pick_halves(k, q, m, second[li]) * ex).astype(BF16))
        return ops, qt, kt, vb, qk, v, gate, e_last, pair, v_prev

    def heads(c, w, states):
        ops, qt, kt, vb, qk, v, gate, e_last, pair, v_prev = w
        rs = pl.ds(pl.multiple_of(c * CHUNK, CHUNK), CHUNK)
        sls = [slice(h * HG_DK, (h + 1) * HG_DK) for h in range(HG_HEADS)]
        ps = []
        for sl in sls:
            p = _dot_nt(ops[0][:, sl], ops[0][:, sl]) * mk_ref[0]
            for li in range(1, n_lv):
                p = p + _dot_nt(ops[li][:, sl], ops[li][:, sl]) * mk_ref[li]
            ps.append(p.astype(BF16))
        new_states = [states[h] * e_last[:, sl] + _dot_tn(vb[:, sl], kt[:, sl]) for h, sl in enumerate(sls)]
        os_ = [_dot_nt(qt[:, sl], states[h].astype(BF16)) + _dot(ps[h], vb[:, sl]) for h, sl in enumerate(sls)]
        for h, sl in enumerate(sls):
            o = os_[h] + jnp.sum(qk[:, sl], axis=-1, keepdims=True) * v[:, sl]
            c1 = jnp.where(odd, jnp.sum(pair[:, sl], axis=-1, keepdims=True), 0.0)
            o = o + c1 * v_prev[:, sl]
            og = _rms(o, go) * gate[:, sl]
            o_ref[0, rs, sl] = og.astype(BF16)
        return new_states

    def chunk_body(c, carry):
        states = heads(c, wide(c), [st_ref[0, h] for h in range(HG_HEADS)])
        for h in range(HG_HEADS):
            st_ref[0, h] = states[h]
        return carry

    lax.fori_loop(0, tc // CHUNK, chunk_body, 0)


def _gla(z, lb_logits, layer, g_o, s0t):
    b, t, zw = z.shape
    dh = zw // 4
    tc = _tile(t, 256)
    masks = jnp.asarray(_level_masks(), F32)
    zspec = lambda c: pl.BlockSpec((1, tc, dh), lambda i, j, c=c: (i, j, c))
    sspec = pl.BlockSpec((1,) + s0t.shape[1:], lambda i, j: (i, 0, 0, 0))
    return pl.pallas_call(
        functools.partial(_gla_kernel, layer=layer),
        grid=(b, t // tc),
        in_specs=[zspec(0), zspec(1), zspec(2), zspec(3),
                  pl.BlockSpec(lb_logits.shape, lambda i, j: (0, 0)),
                  pl.BlockSpec((1, HG_DV), lambda i, j: (0, 0)),
                  sspec,
                  pl.BlockSpec(masks.shape, lambda i, j: (0, 0, 0))],
        out_specs=[pl.BlockSpec((1, tc, dh), lambda i, j: (i, j, 0)), sspec],
        out_shape=[jax.ShapeDtypeStruct((b, t, dh), BF16), jax.ShapeDtypeStruct(s0t.shape, F32)],
        compiler_params=_params("parallel", "arbitrary"),
        name="hgrn2_recurrence",
    )(z, z, z, z, lb_logits, g_o.reshape(1, HG_DV), s0t, masks)


def _rope_tables(pos):
    half = ROPE_DIM // 2
    inv = ROPE_THETA ** (-jnp.arange(half, dtype=F32) / half)
    ang = pos.astype(F32)[:, None] * inv[None, :]
    cos, sin = jnp.cos(ang), jnp.sin(ang)
    c1 = jnp.concatenate([cos, cos], axis=1)
    s1 = jnp.concatenate([-sin, sin], axis=1)
    return c1, s1, jnp.tile(c1, (1, MLA_HEADS)), jnp.tile(s1, (1, MLA_HEADS))


def _swap_halves(w):
    k, c = w.shape
    w = w.reshape(k, c // ROPE_DIM, 2, ROPE_DIM // 2)
    return w[:, :, ::-1, :].reshape(k, c)


def _prep_even(w_in_a, w_uq, w_ukv, w_out_a, dc):
    d = w_in_a.shape[0]
    kpe = w_in_a[:, -ROPE_DIM:]
    w_in = jnp.concatenate([w_in_a, _swap_halves(kpe)], axis=1).astype(BF16)
    ql = w_uq.shape[0]
    wq = w_uq.reshape(ql, MLA_HEADS, NOPE_DIM + ROPE_DIM)
    wqn = wq[:, :, :NOPE_DIM].reshape(ql, MLA_HEADS * NOPE_DIM).astype(BF16)
    wqp = wq[:, :, NOPE_DIM:].reshape(ql, MLA_HEADS * ROPE_DIM)
    wqs = _swap_halves(wqp).astype(BF16)
    kl = w_ukv.shape[0]
    wkv = w_ukv.reshape(kl, MLA_HEADS, NOPE_DIM + V_DIM)
    wuk = jnp.transpose(wkv[:, :, :NOPE_DIM], (1, 2, 0)).astype(BF16)
    wuv = jnp.transpose(wkv[:, :, NOPE_DIM:], (1, 0, 2)).astype(BF16)
    wo = w_out_a.astype(BF16)
    return dict(w_in=w_in, wqn=wqn, wqp=wqp.astype(BF16), wqs=wqs, wuk=wuk, wuv=wuv, wo_a=wo[:dc], wo_b=wo[dc:])


def _run_group(x3, pos, conv_prev, ckv_prev, kpe_prev, hgrn_prev, p):
    b, t, d = x3.shape
    n = b * t
    depth = p["norm_ffn1"].shape[0]
    x = x3.reshape(n, d)
    c1, s1, c8, s8 = _rope_tables(pos)
    conv_new, ckv_new, kpe_new, hgrn_new = [], [], [], []
    for l in range(depth):
        x = _ffn(x, p["norm_ffn1"][l], p["wg1"][l], p["wu1"][l], p["wd1"][l])
        if l % 2 == 0:
            e = l // 2
            pe = p["even"][e]
            dc = p["w_conv"].shape[2]
            cprev = jnp.zeros((b, CONV_W - 1, dc), F32) if conv_prev is None else conv_prev[e]
            ya, cs, q, ckv, kpe, kv = _even_in(x.reshape(b, t, d), p["norm_mix"][l], pe["w_in"], cprev,
                                               p["w_conv"][e], p["g_q"][e], pe["wqn"], pe["wqp"], pe["wqs"],
                                               pe["wuk"], p["g_kv"][e], c1, s1, c8, s8)
            if ckv_prev is None:
                yb = _attention(q, kv, pe["wuv"], causal=True, kv_len=t)
            else:
                past = ckv_prev.shape[2]
                kv_len = past + t
                tk = 512
                pad = (-kv_len) % tk
                cache = jnp.concatenate([ckv_prev[e], kpe_prev[e]], axis=-1).astype(BF16)
                kv_all = jnp.concatenate([cache, kv, jnp.zeros((b, pad, kv.shape[2]), BF16)], axis=1)
                yb = _attention(q, kv_all, pe["wuv"], causal=False, kv_len=kv_len)
            x = _matmul_residual([ya.reshape(n, -1), yb.reshape(n, -1)], [pe["wo_a"], pe["wo_b"]], x)
            conv_new.append(cs)
            ckv_new.append(ckv)
            kpe_new.append(kpe)
        else:
            o = l // 2
            z = _norm_matmul(x, p["norm_mix"][l], p["w_in_c"][o]).reshape(b, t, -1)
            if hgrn_prev is None:
                s0t = jnp.zeros((b, HG_HEADS, HG_DV, HG_DK), F32)
            else:
                s0t = jnp.swapaxes(hgrn_prev[o], -1, -2)
            og, st = _gla(z, p["lb_logits"], o, p["g_o"][o], s0t)
            x = _matmul_residual([og.reshape(n, -1)], [p["w_out_c"][o]], x)
            hgrn_new.append(jnp.swapaxes(st, -1, -2))
        g_final = p["norm_final"] if l == depth - 1 else None
        x = _ffn(x, p["norm_ffn2"][l], p["wg2"][l], p["wu2"][l], p["wd2"][l], g_final)
    return x.reshape(b, t, d), jnp.stack(conv_new), jnp.stack(ckv_new), jnp.stack(kpe_new), jnp.stack(hgrn_new)


def kernel(x_prompt, x_sample, cache_conv, cache_ckv, cache_kpe, state_hgrn, norm_ffn1, w_ffn1_gate, w_ffn1_up, w_ffn1_down, norm_mix, w_in_a, w_conv, g_q, w_uq, g_kv, w_ukv, w_out_a, w_in_c, lb_logits, g_o, w_out_c, norm_ffn2, w_ffn2_gate, w_ffn2_up, w_ffn2_down, norm_final):
    dc = w_conv.shape[2]
    p = dict(
        norm_ffn1=norm_ffn1, wg1=w_ffn1_gate, wu1=w_ffn1_up, wd1=w_ffn1_down,
        norm_ffn2=norm_ffn2, wg2=w_ffn2_gate, wu2=w_ffn2_up, wd2=w_ffn2_down,
        norm_mix=norm_mix, w_conv=w_conv, g_q=g_q, g_kv=g_kv,
        even=[_prep_even(w_in_a[e], w_uq[e], w_ukv[e], w_out_a[e], dc) for e in range(w_in_a.shape[0])],
        w_in_c=w_in_c.astype(BF16), lb_logits=lb_logits, g_o=g_o, w_out_c=w_out_c.astype(BF16),
        norm_final=norm_final,
    )
    pos_p = jnp.arange(x_prompt.shape[1], dtype=jnp.int32)
    pos_s = cache_ckv.shape[2] + jnp.arange(x_sample.shape[1], dtype=jnp.int32)
    y_p, conv_p, ckv_p, kpe_p, hgrn_p = _run_group(x_prompt, pos_p, None, None, None, None, p)
    y_s, conv_s, ckv_s, kpe_s, hgrn_s = _run_group(x_sample, pos_s, cache_conv, cache_ckv, cache_kpe, state_hgrn, p)
    return (y_p, y_s, conv_p, ckv_p, kpe_p, hgrn_p, conv_s, ckv_s, kpe_s, hgrn_s)
```

```python
import functools

import numpy as np
import jax
import jax.numpy as jnp
from jax import lax
from jax.experimental import pallas as pl
from jax.experimental.pallas import tpu as pltpu

F32 = jnp.float32
BF16 = jnp.bfloat16

EPS = 1e-6
CHUNK = 64
CONV_W = 3
MLA_HEADS = 8
NOPE_DIM = 128
ROPE_DIM = 64
V_DIM = 128
ROPE_THETA = 10000.0
MLA_SCALE = (NOPE_DIM + ROPE_DIM) ** -0.5
HG_HEADS = 16
HG_DK = 128
HG_DV = 128
LANES = 128
SUBLANES = 8
NEG_BIG = -1e30
LOG2E = 1.4426950408889634
LEVELS = (32, 16, 8, 4, 2)
VMEM_LIMIT_BYTES = 60 * 1024 * 1024
FFN_TF = 512


def _tile(n, pref):
    if n <= pref:
        return n
    for t in range(pref, 7, -8):
        if n % t == 0:
            return t
    return n


def _col_tile(n, pref):
    for t in range(min(pref, n) // LANES * LANES, 0, -LANES):
        if n % t == 0:
            return t
    return n


def _params(*sem, **kw):
    return pltpu.CompilerParams(dimension_semantics=sem, vmem_limit_bytes=VMEM_LIMIT_BYTES, **kw)


def _sigmoid(x):
    return 1.0 / (1.0 + jnp.exp(-x))


def _silu(x):
    h = 0.5 * x
    return h + h * jnp.tanh(h)


def _rms(x, g):
    return x * lax.rsqrt(jnp.mean(x * x, axis=-1, keepdims=True) + EPS) * g


def _dot(a, b):
    return jnp.dot(a, b, preferred_element_type=F32)


def _dot_nt(a, b):
    return lax.dot_general(a, b, (((1,), (1,)), ((), ())), preferred_element_type=F32)


def _dot_tn(a, b):
    return lax.dot_general(a, b, (((0,), (0,)), ((), ())), preferred_element_type=F32)


def _ffn_kernel(x_ref, g_ref, wg_ref, wu_ref, wd_ref, *rest, final_norm):
    if final_norm:
        gf_ref, o_ref, n_ref = rest
    else:
        o_ref, n_ref = rest
    j = pl.program_id(1)
    tm = x_ref.shape[0]
    rc = min(tm, 128)

    def rows(i):
        return pl.ds(pl.multiple_of(i * rc, rc), rc)

    @pl.when(j == 0)
    def _():
        def norm_rows(i, carry):
            n_ref[rows(i), :] = _rms(x_ref[rows(i), :], g_ref[...]).astype(BF16)
            return carry

        lax.fori_loop(0, tm // rc, norm_rows, 0)
        o_ref[...] = jnp.zeros_like(o_ref)

    n = n_ref[...]
    hg = _dot(n, wg_ref[...])
    hu = _dot(n, wu_ref[...])
    a = _silu(hg) * hu
    o_ref[...] += _dot(a.astype(BF16), wd_ref[...])

    @pl.when(j == pl.num_programs(1) - 1)
    def _():
        def finish_rows(i, carry):
            y = x_ref[rows(i), :] + 0.5 * o_ref[rows(i), :]
            if final_norm:
                y = _rms(y, gf_ref[...])
            o_ref[rows(i), :] = y
            return carry

        lax.fori_loop(0, tm // rc, finish_rows, 0)


def _ffn(x, g, wg, wu, wd, layer, g_final=None):
    n, d = x.shape
    f = wg.shape[2]
    tm = _tile(n, 1024)
    tf = _col_tile(f, FFN_TF)
    in_specs = [
        pl.BlockSpec((tm, d), lambda i, j: (i, 0)),
        pl.BlockSpec((None, 1, d), lambda i, j: (layer, 0, 0)),
        pl.BlockSpec((None, d, tf), lambda i, j: (layer, 0, j)),
        pl.BlockSpec((None, d, tf), lambda i, j: (layer, 0, j)),
        pl.BlockSpec((None, tf, d), lambda i, j: (layer, j, 0)),
    ]
    args = [x, g.reshape(g.shape[0], 1, d), wg, wu, wd]
    if g_final is not None:
        in_specs.append(pl.BlockSpec((1, d), lambda i, j: (0, 0)))
        args.append(g_final.reshape(1, d))
    return pl.pallas_call(
        functools.partial(_ffn_kernel, final_norm=g_final is not None),
        grid=(n // tm, f // tf),
        in_specs=in_specs,
        out_specs=pl.BlockSpec((tm, d), lambda i, j: (i, 0)),
        out_shape=jax.ShapeDtypeStruct((n, d), F32),
        scratch_shapes=[pltpu.VMEM((tm, d), BF16)],
        compiler_params=_params("parallel", "arbitrary"),
        name="ffn",
    )(*args)


def _nmm_kernel(x_ref, g_ref, w_ref, o_ref, n_ref):
    tm = x_ref.shape[0]
    rc = min(tm, 128)

    @pl.when(pl.program_id(1) == 0)
    def _():
        def norm_rows(i, carry):
            rows = pl.ds(pl.multiple_of(i * rc, rc), rc)
            n_ref[rows, :] = _rms(x_ref[rows, :], g_ref[...]).astype(BF16)
            return carry

        lax.fori_loop(0, tm // rc, norm_rows, 0)

    o_ref[...] = _dot(n_ref[...], w_ref[...])


def _norm_matmul(x, g, w, layer):
    n, d = x.shape
    c = w.shape[2]
    tm = _tile(n, 1024)
    tn = _col_tile(c, 2048)
    return pl.pallas_call(
        _nmm_kernel,
        grid=(n // tm, c // tn),
        in_specs=[
            pl.BlockSpec((tm, d), lambda i, j: (i, 0)),
            pl.BlockSpec((1, d), lambda i, j: (0, 0)),
            pl.BlockSpec((None, d, tn), lambda i, j: (layer, 0, j)),
        ],
        out_specs=pl.BlockSpec((tm, tn), lambda i, j: (i, j)),
        out_shape=jax.ShapeDtypeStruct((n, c), F32),
        scratch_shapes=[pltpu.VMEM((tm, d), BF16)],
        compiler_params=_params("parallel", "arbitrary"),
        name="norm_matmul",
    )(x, g.reshape(1, d), w)


def _mmres_kernel(*refs, n_in):
    x_ref, o_ref = refs[2 * n_in], refs[2 * n_in + 1]
    acc = x_ref[...]
    for a_ref, w_ref in zip(refs[:n_in], refs[n_in:2 * n_in]):
        acc = acc + _dot(a_ref[...], w_ref[...])
    o_ref[...] = acc


def _matmul_residual(acts, weights, x):
    n, d = x.shape
    tm = _tile(n, 512)
    in_specs = [pl.BlockSpec((tm, a.shape[1]), lambda i: (i, 0)) for a in acts]
    in_specs += [pl.BlockSpec(w.shape, lambda i: (0, 0)) for w in weights]
    in_specs.append(pl.BlockSpec((tm, d), lambda i: (i, 0)))
    return pl.pallas_call(
        functools.partial(_mmres_kernel, n_in=len(acts)),
        grid=(n // tm,),
        in_specs=in_specs,
        out_specs=pl.BlockSpec((tm, d), lambda i: (i, 0)),
        out_shape=jax.ShapeDtypeStruct((n, d), F32),
        compiler_params=_params("parallel"),
        name="matmul_residual",
    )(*acts, *weights, x)


def _even_in_kernel(x_ref, gn_ref, win_ref, cp_ref, wc_ref, gq_ref, wqn_ref, wqp_ref, wqs_ref, wuk_ref, gkv_ref,
                    c1_ref, s1_ref, c8_ref, s8_ref,
                    ya_ref, cs_ref, q_ref, ckv_ref, kpe_ref, kv_ref, vp_ref, *, dc, ql, kl):
    t = pl.program_id(1)
    tt = x_ref.shape[1]
    o3 = 3 * dc
    o4 = o3 + ql
    o5 = o4 + kl
    n = _rms(x_ref[0], gn_ref[...]).astype(BF16)

    def proj(c0, c1):
        return _dot(n, win_ref[:, c0:c1])

    v = proj(dc, 2 * dc) * proj(2 * dc, o3)
    cp = cp_ref[0]
    first = t == 0
    pm2 = jnp.where(first, cp[0:1], vp_ref[SUBLANES - 2:SUBLANES - 1, :])
    pm1 = jnp.where(first, cp[1:2], vp_ref[SUBLANES - 1:SUBLANES, :])
    rows = lax.broadcasted_iota(jnp.int32, v.shape, 0)
    v1 = jnp.where(rows == 0, pm1, pltpu.roll(v, 1, 0))
    v2 = jnp.where(rows == 0, pm2, jnp.where(rows == 1, pm1, pltpu.roll(v, 2, 0)))
    w = wc_ref[...]
    conv = v2 * w[0:1] + v1 * w[1:2] + v * w[2:3]
    ya_ref[0] = (proj(0, dc) * conv).astype(BF16)
    vp_ref[...] = v[tt - SUBLANES:tt]

    @pl.when(t == pl.num_programs(1) - 1)
    def _():
        cs_ref[0] = v[tt - (CONV_W - 1):tt]

    cqn = _rms(proj(o3, o4), gq_ref[...]).astype(BF16)
    qn = _dot(cqn, wqn_ref[...])
    qpr = _dot(cqn, wqp_ref[...]) * c8_ref[...] + _dot(cqn, wqs_ref[...]) * s8_ref[...]
    for h in range(MLA_HEADS):
        qlat = _dot(qn[:, h * NOPE_DIM:(h + 1) * NOPE_DIM].astype(BF16), wuk_ref[h])
        q_ref[0, h, :, 0:kl] = qlat.astype(BF16)
        q_ref[0, h, :, kl:kl + ROPE_DIM] = qpr[:, h * ROPE_DIM:(h + 1) * ROPE_DIM].astype(BF16)
    cn = _rms(proj(o4, o5), gkv_ref[...])
    kp = proj(o5, o5 + 2 * ROPE_DIM)
    kr = kp[:, 0:ROPE_DIM] * c1_ref[...] + kp[:, ROPE_DIM:2 * ROPE_DIM] * s1_ref[...]
    ckv_ref[0] = cn
    kpe_ref[0] = kr
    kv_ref[0, :, 0:kl] = cn.astype(BF16)
    kv_ref[0, :, kl:kl + ROPE_DIM] = kr.astype(BF16)


def _even_in(x, g_norm, w_in, conv_prev, w_conv, g_q, wqn, wqp, wqs, wuk, g_kv, c1, s1, c8, s8):
    b, t, d = x.shape
    dc = w_conv.shape[1]
    ql = g_q.shape[0]
    kl = g_kv.shape[0]
    tt = _tile(t, 256)
    hr = MLA_HEADS * ROPE_DIM
    once = lambda a: pl.BlockSpec(a.shape, lambda i, j: (0,) * a.ndim, pipeline_mode=pl.Buffered(1))
    wts = [g_norm.reshape(1, d), w_in]
    wts2 = [w_conv, g_q.reshape(1, ql), wqn, wqp, wqs, wuk, g_kv.reshape(1, kl)]
    in_specs = [pl.BlockSpec((1, tt, d), lambda i, j: (i, j, 0))] + [once(a) for a in wts] + [
        pl.BlockSpec((1, CONV_W - 1, dc), lambda i, j: (i, 0, 0)),
    ] + [once(a) for a in wts2] + [
        pl.BlockSpec((tt, ROPE_DIM), lambda i, j: (j, 0)),
        pl.BlockSpec((tt, ROPE_DIM), lambda i, j: (j, 0)),
        pl.BlockSpec((tt, hr), lambda i, j: (j, 0)),
        pl.BlockSpec((tt, hr), lambda i, j: (j, 0)),
    ]
    out_shape = [
        jax.ShapeDtypeStruct((b, t, dc), BF16),
        jax.ShapeDtypeStruct((b, CONV_W - 1, dc), F32),
        jax.ShapeDtypeStruct((b, MLA_HEADS, t, kl + ROPE_DIM), BF16),
        jax.ShapeDtypeStruct((b, t, kl), F32),
        jax.ShapeDtypeStruct((b, t, ROPE_DIM), F32),
        jax.ShapeDtypeStruct((b, t, kl + ROPE_DIM), BF16),
    ]
    out_specs = [
        pl.BlockSpec((1, tt, dc), lambda i, j: (i, j, 0)),
        pl.BlockSpec((1, CONV_W - 1, dc), lambda i, j: (i, 0, 0)),
        pl.BlockSpec((1, MLA_HEADS, tt, kl + ROPE_DIM), lambda i, j: (i, 0, j, 0)),
        pl.BlockSpec((1, tt, kl), lambda i, j: (i, j, 0)),
        pl.BlockSpec((1, tt, ROPE_DIM), lambda i, j: (i, j, 0)),
        pl.BlockSpec((1, tt, kl + ROPE_DIM), lambda i, j: (i, j, 0)),
    ]
    return pl.pallas_call(
        functools.partial(_even_in_kernel, dc=dc, ql=ql, kl=kl),
        grid=(b, t // tt),
        in_specs=in_specs,
        out_specs=out_specs,
        out_shape=out_shape,
        scratch_shapes=[pltpu.VMEM((SUBLANES, dc), F32)],
        compiler_params=_params("parallel", "arbitrary"),
        name="even_in",
    )(x, *wts, conv_prev, *wts2, c1, s1, c8, s8)


def _attn_kernel(q_ref, kv_ref, wuv_ref, o_ref, *, tq, tk, causal, kv_len, kl):
    i = pl.program_id(1)
    rows = MLA_HEADS * tq
    q = q_ref[0].reshape(rows, q_ref.shape[3])
    if causal:
        tok = i * tq + lax.rem(lax.broadcasted_iota(jnp.int32, (rows, 1), 0), tq)
        limit = (tok // CHUNK + 1) * CHUNK
    else:
        limit = kv_len
    c = MLA_SCALE * LOG2E

    def scores(j):
        return _dot_nt(q, kv_ref[0, j * tk:(j + 1) * tk, :])

    def run(n):
        s = scores(0)
        m = l = acc = None
        for j in range(n):
            s_next = scores(j + 1) if j + 1 < n else None
            if j == n - 1:
                kpos = j * tk + lax.broadcasted_iota(jnp.int32, (1, tk), 1)
                s = jnp.where(kpos < limit, s, NEG_BIG)
            m_new = jnp.max(s, axis=-1, keepdims=True)
            if m is not None:
                m_new = jnp.maximum(m, m_new)
            p = jnp.exp2((s - m_new) * c)
            p_sum = jnp.sum(p, axis=-1, keepdims=True)
            pv = _dot(p.astype(BF16), kv_ref[0, j * tk:(j + 1) * tk, 0:kl])
            if m is None:
                l, acc = p_sum, pv
            else:
                alpha = jnp.exp2((m - m_new) * c)
                l = alpha * l + p_sum
                acc = alpha * acc + pv
            m, s = m_new, s_next
        o = (acc * (1.0 / l)).astype(BF16)
        for h in range(MLA_HEADS):
            o_ref[0, :, h * V_DIM:(h + 1) * V_DIM] = _dot(o[h * tq:(h + 1) * tq], wuv_ref[h]).astype(BF16)

    if causal:
        n_kv = (i * tq) // tk + 1
        for n in range(1, kv_ref.shape[1] // tk + 1):
            pl.when(n_kv == n)(functools.partial(run, n))
    else:
        run((kv_len + tk - 1) // tk)


def _attention(q, kv, wuv, *, causal, kv_len):
    b, h, t, dq = q.shape
    tkv = kv.shape[1]
    kl = wuv.shape[1]
    tq = _tile(t, 128)
    tk = _tile(tkv, 512)
    assert tk % tq == 0 and tkv % tk == 0 and tq % CHUNK == 0
    return pl.pallas_call(
        functools.partial(_attn_kernel, tq=tq, tk=tk, causal=causal, kv_len=kv_len, kl=kl),
        grid=(b, t // tq),
        in_specs=[
            pl.BlockSpec((1, h, tq, dq), lambda i, j: (i, 0, j, 0)),
            pl.BlockSpec((1, tkv, dq), lambda i, j: (i, 0, 0)),
            pl.BlockSpec(wuv.shape, lambda i, j: (0, 0, 0)),
        ],
        out_specs=pl.BlockSpec((1, tq, h * V_DIM), lambda i, j: (i, j, 0)),
        out_shape=jax.ShapeDtypeStruct((b, t, h * V_DIM), BF16),
        compiler_params=_params("parallel", "arbitrary"),
        name="mla_attention",
    )(q, kv, wuv)


def _level_masks():
    t = np.arange(CHUNK)[:, None]
    s = np.arange(CHUNK)[None, :]
    masks = [(t // (2 * m) == s // (2 * m)) & ((t & m) != 0) & ((s & m) == 0) for m in LEVELS]
    return np.stack(masks).astype(np.float32)


def _boundary_rows(a, m):
    w = a.shape[1]
    if m >= SUBLANES:
        parts = [jnp.broadcast_to(a[b0 + m - 1:b0 + m], (2 * m, w)) for b0 in range(0, CHUNK, 2 * m)]
        return parts[0] if len(parts) == 1 else jnp.concatenate(parts, axis=0)
    a3 = a.reshape(CHUNK // SUBLANES, SUBLANES, w)
    sub = lax.broadcasted_iota(jnp.int32, a3.shape, 1)
    out = None
    for b0 in range(SUBLANES - 2 * m, -1, -2 * m):
        piece = jnp.broadcast_to(a3[:, b0 + m - 1:b0 + m, :], a3.shape)
        out = piece if out is None else jnp.where(sub < b0 + 2 * m, piece, out)
    return out.reshape(CHUNK, w)


def _pick_halves(first, second, m, is_second):
    if m < SUBLANES:
        return jnp.where(is_second, second, first)
    parts = []
    for b0 in range(0, CHUNK, 2 * m):
        parts += [first[b0:b0 + m], second[b0 + m:b0 + 2 * m]]
    return jnp.concatenate(parts, axis=0)


def _gla_kernel(zq_ref, zf_ref, zi_ref, zg_ref, lbl_ref, go_ref, s0_ref, mk_ref, o_ref, st_ref, *, layer):
    t = pl.program_id(1)
    tc = zq_ref.shape[1]
    n_lv = len(LEVELS)

    @pl.when(t == 0)
    def _():
        st_ref[...] = s0_ref[...]

    ll = lbl_ref[...]
    e = jnp.exp(ll - jnp.max(ll, axis=0, keepdims=True))
    sm = e / jnp.sum(e, axis=0, keepdims=True)
    lb = jnp.zeros((1, ll.shape[1]), F32)
    for r in range(1, layer + 1):
        lb = lb + sm[r:r + 1]
    one_m_lb = 1.0 - lb
    go = go_ref[...]
    row = lax.broadcasted_iota(jnp.int32, (CHUNK, 1), 0)
    second = [(row & m) != 0 for m in LEVELS]
    odd = (row & 1) != 0

    def wide(c):
        rs = pl.ds(pl.multiple_of(c * CHUNK, CHUNK), CHUNK)
        zq = zq_ref[0, rs, :]
        zf = zf_ref[0, rs, :]
        v = zi_ref[0, rs, :]
        zg = zg_ref[0, rs, :]
        sig = _sigmoid(zf)
        g = jnp.log(lb + one_m_lb * sig) * LOG2E
        k = one_m_lb * (1.0 - sig)
        q = _silu(zq)
        gate = _silu(zg)
        a = g
        for sh in (1, 2, 4):
            a = a + jnp.where(row >= sh, pltpu.roll(a, sh, 0), 0.0)
        for sh in (8, 16, 32):
            a = a + jnp.concatenate([jnp.zeros((sh, a.shape[1]), F32), a[:CHUNK - sh]], axis=0)
        e_q = jnp.exp2(a)
        e_k = jnp.exp2(a[CHUNK - 1:CHUNK] - a)
        e_last = e_q[CHUNK - 1:CHUNK]
        qt = (q * e_q).astype(BF16)
        kt = (k * e_k).astype(BF16)
        qk = q * k
        vb = v.astype(BF16)
        pair = q * jnp.exp2(g) * pltpu.roll(k, 1, 0)
        v_prev = pltpu.roll(v, 1, 0)
        ops = []
        for li, m in enumerate(LEVELS):
            ex = jnp.exp2(-jnp.abs(a - _boundary_rows(a, m)))
            ops.append((_pick_halves(k, q, m, second[li]) * ex).astype(BF16))
        return ops, qt, kt, vb, qk, v, gate, e_last, pair, v_prev

    def heads(c, w, states):
        ops, qt, kt, vb, qk, v, gate, e_last, pair, v_prev = w
        rs = pl.ds(pl.multiple_of(c * CHUNK, CHUNK), CHUNK)
        sls = [slice(h * HG_DK, (h + 1) * HG_DK) for h in range(HG_HEADS)]
        ps = []
        for sl in sls:
            p = _dot_nt(ops[0][:, sl], ops[0][:, sl]) * mk_ref[0]
            for li in range(1, n_lv):
                p = p + _dot_nt(ops[li][:, sl], ops[li][:, sl]) * mk_ref[li]
            ps.append(p.astype(BF16))
        new_states = [states[h] * e_last[:, sl] + _dot_tn(vb[:, sl], kt[:, sl]) for h, sl in enumerate(sls)]
        os_ = [_dot_nt(qt[:, sl], states[h].astype(BF16)) + _dot(ps[h], vb[:, sl]) for h, sl in enumerate(sls)]
        for h, sl in enumerate(sls):
            o = os_[h] + jnp.sum(qk[:, sl], axis=-1, keepdims=True) * v[:, sl]
            c1 = jnp.where(odd, jnp.sum(pair[:, sl], axis=-1, keepdims=True), 0.0)
            o = o + c1 * v_prev[:, sl]
            og = _rms(o, go) * gate[:, sl]
            o_ref[0, rs, sl] = og.astype(BF16)
        return new_states

    def chunk_body(c, carry):
        states = heads(c, wide(c), [st_ref[0, h] for h in range(HG_HEADS)])
        for h in range(HG_HEADS):
            st_ref[0, h] = states[h]
        return carry

    lax.fori_loop(0, tc // CHUNK, chunk_body, 0)


def _gla(z, lb_logits, layer, g_o, s0t):
    b, t, zw = z.shape
    dh = zw // 4
    tc = _tile(t, 256)
    masks = jnp.asarray(_level_masks(), F32)
    zspec = lambda c: pl.BlockSpec((1, tc, dh), lambda i, j, c=c: (i, j, c))
    sspec = pl.BlockSpec((1,) + s0t.shape[1:], lambda i, j: (i, 0, 0, 0))
    return pl.pallas_call(
        functools.partial(_gla_kernel, layer=layer),
        grid=(b, t // tc),
        in_specs=[zspec(0), zspec(1), zspec(2), zspec(3),
                  pl.BlockSpec(lb_logits.shape, lambda i, j: (0, 0)),
                  pl.BlockSpec((1, HG_DV), lambda i, j: (0, 0)),
                  sspec,
                  pl.BlockSpec(masks.shape, lambda i, j: (0, 0, 0))],
        out_specs=[pl.BlockSpec((1, tc, dh), lambda i, j: (i, j, 0)), sspec],
        out_shape=[jax.ShapeDtypeStruct((b, t, dh), BF16), jax.ShapeDtypeStruct(s0t.shape, F32)],
        compiler_params=_params("parallel", "arbitrary"),
        name="hgrn2_recurrence",
    )(z, z, z, z, lb_logits, g_o.reshape(1, HG_DV), s0t, masks)


def _rope_tables(pos):
    half = ROPE_DIM // 2
    inv = ROPE_THETA ** (-jnp.arange(half, dtype=F32) / half)
    ang = pos.astype(F32)[:, None] * inv[None, :]
    cos, sin = jnp.cos(ang), jnp.sin(ang)
    c1 = jnp.concatenate([cos, cos], axis=1)
    s1 = jnp.concatenate([-sin, sin], axis=1)
    return c1, s1, jnp.tile(c1, (1, MLA_HEADS)), jnp.tile(s1, (1, MLA_HEADS))


def _swap_halves(w):
    k, c = w.shape
    w = w.reshape(k, c // ROPE_DIM, 2, ROPE_DIM // 2)
    return w[:, :, ::-1, :].reshape(k, c)


def _prep_even(w_in_a, w_uq, w_ukv, w_out_a, dc):
    d = w_in_a.shape[0]
    kpe = w_in_a[:, -ROPE_DIM:]
    w_in = jnp.concatenate([w_in_a, _swap_halves(kpe)], axis=1).astype(BF16)
    ql = w_uq.shape[0]
    wq = w_uq.reshape(ql, MLA_HEADS, NOPE_DIM + ROPE_DIM)
    wqn = wq[:, :, :NOPE_DIM].reshape(ql, MLA_HEADS * NOPE_DIM).astype(BF16)
    wqp = wq[:, :, NOPE_DIM:].reshape(ql, MLA_HEADS * ROPE_DIM)
    wqs = _swap_halves(wqp).astype(BF16)
    kl = w_ukv.shape[0]
    wkv = w_ukv.reshape(kl, MLA_HEADS, NOPE_DIM + V_DIM)
    wuk = jnp.transpose(wkv[:, :, :NOPE_DIM], (1, 2, 0)).astype(BF16)
    wuv = jnp.transpose(wkv[:, :, NOPE_DIM:], (1, 0, 2)).astype(BF16)
    wo = w_out_a.astype(BF16)
    return dict(w_in=w_in, wqn=wqn, wqp=wqp.astype(BF16), wqs=wqs, wuk=wuk, wuv=wuv, wo_a=wo[:dc], wo_b=wo[dc:])


def _run_group(x3, pos, conv_prev, ckv_prev, kpe_prev, hgrn_prev, p):
    b, t, d = x3.shape
    n = b * t
    depth = p["norm_ffn1"].shape[0]
    x = x3.reshape(n, d)
    c1, s1, c8, s8 = _rope_tables(pos)
    conv_new, ckv_new, kpe_new, hgrn_new = [], [], [], []
    for l in range(depth):
        x = _ffn(x, p["norm_ffn1"], p["wg1"], p["wu1"], p["wd1"], l)
        if l % 2 == 0:
            e = l // 2
            pe = p["even"][e]
            dc = p["w_conv"].shape[2]
            cprev = jnp.zeros((b, CONV_W - 1, dc), F32) if conv_prev is None else conv_prev[e]
            ya, cs, q, ckv, kpe, kv = _even_in(x.reshape(b, t, d), p["norm_mix"][l], pe["w_in"], cprev,
                                               p["w_conv"][e], p["g_q"][e], pe["wqn"], pe["wqp"], pe["wqs"],
                                               pe["wuk"], p["g_kv"][e], c1, s1, c8, s8)
            if ckv_prev is None:
                yb = _attention(q, kv, pe["wuv"], causal=True, kv_len=t)
            else:
                past = ckv_prev.shape[2]
                kv_len = past + t
                tk = 512
                pad = (-kv_len) % tk
                cache = jnp.concatenate([ckv_prev[e], kpe_prev[e]], axis=-1).astype(BF16)
                kv_all = jnp.concatenate([cache, kv, jnp.zeros((b, pad, kv.shape[2]), BF16)], axis=1)
                yb = _attention(q, kv_all, pe["wuv"], causal=False, kv_len=kv_len)
            x = _matmul_residual([ya.reshape(n, -1), yb.reshape(n, -1)], [pe["wo_a"], pe["wo_b"]], x)
            conv_new.append(cs)
            ckv_new.append(ckv)
            kpe_new.append(kpe)
        else:
            o = l // 2
            z = _norm_matmul(x, p["norm_mix"][l], p["w_in_c"], o).reshape(b, t, -1)
            if hgrn_prev is None:
                s0t = jnp.zeros((b, HG_HEADS, HG_DV, HG_DK), F32)
            else:
                s0t = jnp.swapaxes(hgrn_prev[o], -1, -2)
            og, st = _gla(z, p["lb_logits"], o, p["g_o"][o], s0t)
            x = _matmul_residual([og.reshape(n, -1)], [p["w_out_c"][o]], x)
            hgrn_new.append(jnp.swapaxes(st, -1, -2))
        g_final = p["norm_final"] if l == depth - 1 else None
        x = _ffn(x, p["norm_ffn2"], p["wg2"], p["wu2"], p["wd2"], l, g_final)
    return x.reshape(b, t, d), jnp.stack(conv_new), jnp.stack(ckv_new), jnp.stack(kpe_new), jnp.stack(hgrn_new)


def kernel(x_prompt, x_sample, cache_conv, cache_ckv, cache_kpe, state_hgrn, norm_ffn1, w_ffn1_gate, w_ffn1_up, w_ffn1_down, norm_mix, w_in_a, w_conv, g_q, w_uq, g_kv, w_ukv, w_out_a, w_in_c, lb_logits, g_o, w_out_c, norm_ffn2, w_ffn2_gate, w_ffn2_up, w_ffn2_down, norm_final):
    dc = w_conv.shape[2]
    p = dict(
        norm_ffn1=norm_ffn1, wg1=w_ffn1_gate.astype(BF16), wu1=w_ffn1_up.astype(BF16), wd1=w_ffn1_down.astype(BF16),
        norm_ffn2=norm_ffn2, wg2=w_ffn2_gate.astype(BF16), wu2=w_ffn2_up.astype(BF16), wd2=w_ffn2_down.astype(BF16),
        norm_mix=norm_mix, w_conv=w_conv, g_q=g_q, g_kv=g_kv,
        even=[_prep_even(w_in_a[e], w_uq[e], w_ukv[e], w_out_a[e], dc) for e in range(w_in_a.shape[0])],
        w_in_c=w_in_c.astype(BF16), lb_logits=lb_logits, g_o=g_o, w_out_c=w_out_c.astype(BF16),
        norm_final=norm_final,
    )
    pos_p = jnp.arange(x_prompt.shape[1], dtype=jnp.int32)
    pos_s = cache_ckv.shape[2] + jnp.arange(x_sample.shape[1], dtype=jnp.int32)
    y_p, conv_p, ckv_p, kpe_p, hgrn_p = _run_group(x_prompt, pos_p, None, None, None, None, p)
    y_s, conv_s, ckv_s, kpe_s, hgrn_s = _run_group(x_sample, pos_s, cache_conv, cache_ckv, cache_kpe, state_hgrn, p)
    return (y_p, y_s, conv_p, ckv_p, kpe_p, hgrn_p, conv_s, ckv_s, kpe_s, hgrn_s)
```

```python
import functools

import numpy as np
import jax
import jax.numpy as jnp
from jax import lax
from jax.experimental import pallas as pl
from jax.experimental.pallas import tpu as pltpu

F32 = jnp.float32
BF16 = jnp.bfloat16

EPS = 1e-6
CHUNK = 64
CONV_W = 3
MLA_HEADS = 8
NOPE_DIM = 128
ROPE_DIM = 64
V_DIM = 128
ROPE_THETA = 10000.0
MLA_SCALE = (NOPE_DIM + ROPE_DIM) ** -0.5
HG_HEADS = 16
HG_DK = 128
HG_DV = 128
LANES = 128
SUBLANES = 8
NEG_BIG = -1e30
LOG2E = 1.4426950408889634
LEVELS = (32, 16, 8, 4, 2)
VMEM_LIMIT_BYTES = 60 * 1024 * 1024
FFN_TF = 512


def _tile(n, pref):
    if n <= pref:
        return n
    for t in range(pref, 7, -8):
        if n % t == 0:
            return t
    return n


def _col_tile(n, pref):
    for t in range(min(pref, n) // LANES * LANES, 0, -LANES):
        if n % t == 0:
            return t
    return n


def _params(*sem, **kw):
    return pltpu.CompilerParams(dimension_semantics=sem, vmem_limit_bytes=VMEM_LIMIT_BYTES, **kw)


def _sigmoid(x):
    return 1.0 / (1.0 + jnp.exp(-x))


def _silu(x):
    h = 0.5 * x
    return h + h * jnp.tanh(h)


def _rms(x, g):
    return x * lax.rsqrt(jnp.mean(x * x, axis=-1, keepdims=True) + EPS) * g


def _dot(a, b):
    return jnp.dot(a, b, preferred_element_type=F32)


def _dot_nt(a, b):
    return lax.dot_general(a, b, (((1,), (1,)), ((), ())), preferred_element_type=F32)


def _dot_tn(a, b):
    return lax.dot_general(a, b, (((0,), (0,)), ((), ())), preferred_element_type=F32)


def _ffn_kernel(x_ref, g_ref, wg_ref, wu_ref, wd_ref, *rest, final_norm):
    if final_norm:
        gf_ref, o_ref, n_ref = rest
    else:
        o_ref, n_ref = rest
    j = pl.program_id(1)
    tm = x_ref.shape[0]
    rc = min(tm, 128)

    def rows(i):
        return pl.ds(pl.multiple_of(i * rc, rc), rc)

    @pl.when(j == 0)
    def _():
        def norm_rows(i, carry):
            n_ref[rows(i), :] = _rms(x_ref[rows(i), :], g_ref[...]).astype(BF16)
            o_ref[rows(i), :] = jnp.zeros((rc, o_ref.shape[1]), F32)
            return carry

        lax.fori_loop(0, tm // rc, norm_rows, 0)

    n = n_ref[...]
    hg = _dot(n, wg_ref[...])
    hu = _dot(n, wu_ref[...])
    a = _silu(hg) * hu
    o_ref[...] += _dot(a.astype(BF16), wd_ref[...])

    @pl.when(j == pl.num_programs(1) - 1)
    def _():
        def finish_rows(i, carry):
            y = x_ref[rows(i), :] + 0.5 * o_ref[rows(i), :]
            if final_norm:
                y = _rms(y, gf_ref[...])
            o_ref[rows(i), :] = y
            return carry

        lax.fori_loop(0, tm // rc, finish_rows, 0)


def _ffn(x, g, wg, wu, wd, layer, g_final=None):
    n, d = x.shape
    f = wg.shape[2]
    tm = _tile(n, 1024)
    tf = _col_tile(f, FFN_TF)
    in_specs = [
        pl.BlockSpec((tm, d), lambda i, j: (i, 0)),
        pl.BlockSpec((None, 1, d), lambda i, j: (layer, 0, 0)),
        pl.BlockSpec((None, d, tf), lambda i, j: (layer, 0, j)),
        pl.BlockSpec((None, d, tf), lambda i, j: (layer, 0, j)),
        pl.BlockSpec((None, tf, d), lambda i, j: (layer, j, 0)),
    ]
    args = [x, g.reshape(g.shape[0], 1, d), wg, wu, wd]
    if g_final is not None:
        in_specs.append(pl.BlockSpec((1, d), lambda i, j: (0, 0)))
        args.append(g_final.reshape(1, d))
    return pl.pallas_call(
        functools.partial(_ffn_kernel, final_norm=g_final is not None),
        grid=(n // tm, f // tf),
        in_specs=in_specs,
        out_specs=pl.BlockSpec((tm, d), lambda i, j: (i, 0)),
        out_shape=jax.ShapeDtypeStruct((n, d), F32),
        scratch_shapes=[pltpu.VMEM((tm, d), BF16)],
        compiler_params=_params("parallel", "arbitrary"),
        name="ffn",
    )(*args)


def _nmm_kernel(x_ref, g_ref, w_ref, o_ref, n_ref):
    tm = x_ref.shape[0]
    rc = min(tm, 128)

    @pl.when(pl.program_id(1) == 0)
    def _():
        def norm_rows(i, carry):
            rows = pl.ds(pl.multiple_of(i * rc, rc), rc)
            n_ref[rows, :] = _rms(x_ref[rows, :], g_ref[...]).astype(BF16)
            return carry

        lax.fori_loop(0, tm // rc, norm_rows, 0)

    o_ref[...] = _dot(n_ref[...], w_ref[...])


def _norm_matmul(x, g, w, layer):
    n, d = x.shape
    c = w.shape[2]
    tm = _tile(n, 1024)
    tn = _col_tile(c, 2048)
    return pl.pallas_call(
        _nmm_kernel,
        grid=(n // tm, c // tn),
        in_specs=[
            pl.BlockSpec((tm, d), lambda i, j: (i, 0)),
            pl.BlockSpec((1, d), lambda i, j: (0, 0)),
            pl.BlockSpec((None, d, tn), lambda i, j: (layer, 0, j)),
        ],
        out_specs=pl.BlockSpec((tm, tn), lambda i, j: (i, j)),
        out_shape=jax.ShapeDtypeStruct((n, c), F32),
        scratch_shapes=[pltpu.VMEM((tm, d), BF16)],
        compiler_params=_params("parallel", "arbitrary"),
        name="norm_matmul",
    )(x, g.reshape(1, d), w)


def _mmres_kernel(*refs, n_in):
    x_ref, o_ref = refs[2 * n_in], refs[2 * n_in + 1]
    acc = x_ref[...]
    for a_ref, w_ref in zip(refs[:n_in], refs[n_in:2 * n_in]):
        acc = acc + _dot(a_ref[...], w_ref[...])
    o_ref[...] = acc


def _matmul_residual(acts, weights, x):
    n, d = x.shape
    tm = _tile(n, 512)
    in_specs = [pl.BlockSpec((tm, a.shape[1]), lambda i: (i, 0)) for a in acts]
    in_specs += [pl.BlockSpec(w.shape, lambda i: (0, 0)) for w in weights]
    in_specs.append(pl.BlockSpec((tm, d), lambda i: (i, 0)))
    return pl.pallas_call(
        functools.partial(_mmres_kernel, n_in=len(acts)),
        grid=(n // tm,),
        in_specs=in_specs,
        out_specs=pl.BlockSpec((tm, d), lambda i: (i, 0)),
        out_shape=jax.ShapeDtypeStruct((n, d), F32),
        compiler_params=_params("parallel"),
        name="matmul_residual",
    )(*acts, *weights, x)


def _even_in_kernel(x_ref, gn_ref, win_ref, cp_ref, wc_ref, gq_ref, wqn_ref, wqp_ref, wqs_ref, wuk_ref, gkv_ref,
                    c1_ref, s1_ref, c8_ref, s8_ref,
                    ya_ref, cs_ref, q_ref, ckv_ref, kpe_ref, kv_ref, vp_ref, *, dc, ql, kl):
    t = pl.program_id(1)
    tt = x_ref.shape[1]
    o3 = 3 * dc
    o4 = o3 + ql
    o5 = o4 + kl
    n = _rms(x_ref[0], gn_ref[...]).astype(BF16)

    def proj(c0, c1):
        return _dot(n, win_ref[:, c0:c1])

    cqn = _rms(proj(o3, o4), gq_ref[...]).astype(BF16)
    qn = _dot(cqn, wqn_ref[...])
    qpr = _dot(cqn, wqp_ref[...]) * c8_ref[...] + _dot(cqn, wqs_ref[...]) * s8_ref[...]
    v = proj(dc, 2 * dc) * proj(2 * dc, o3)
    gb = proj(0, dc)
    cn = _rms(proj(o4, o5), gkv_ref[...])
    kp = proj(o5, o5 + 2 * ROPE_DIM)
    kr = kp[:, 0:ROPE_DIM] * c1_ref[...] + kp[:, ROPE_DIM:2 * ROPE_DIM] * s1_ref[...]
    for h in range(MLA_HEADS):
        qlat = _dot(qn[:, h * NOPE_DIM:(h + 1) * NOPE_DIM].astype(BF16), wuk_ref[h])
        q_ref[0, h, :, 0:kl] = qlat.astype(BF16)
        q_ref[0, h, :, kl:kl + ROPE_DIM] = qpr[:, h * ROPE_DIM:(h + 1) * ROPE_DIM].astype(BF16)
    cp = cp_ref[0]
    first = t == 0
    pm2 = jnp.where(first, cp[0:1], vp_ref[SUBLANES - 2:SUBLANES - 1, :])
    pm1 = jnp.where(first, cp[1:2], vp_ref[SUBLANES - 1:SUBLANES, :])
    rows = lax.broadcasted_iota(jnp.int32, v.shape, 0)
    v1 = jnp.where(rows == 0, pm1, pltpu.roll(v, 1, 0))
    v2 = jnp.where(rows == 0, pm2, jnp.where(rows == 1, pm1, pltpu.roll(v, 2, 0)))
    w = wc_ref[...]
    conv = v2 * w[0:1] + v1 * w[1:2] + v * w[2:3]
    ya_ref[0] = (gb * conv).astype(BF16)
    vp_ref[...] = v[tt - SUBLANES:tt]
    ckv_ref[0] = cn
    kpe_ref[0] = kr
    kv_ref[0, :, 0:kl] = cn.astype(BF16)
    kv_ref[0, :, kl:kl + ROPE_DIM] = kr.astype(BF16)

    @pl.when(t == pl.num_programs(1) - 1)
    def _():
        cs_ref[0] = v[tt - (CONV_W - 1):tt]


def _even_in(x, g_norm, w_in, conv_prev, w_conv, g_q, wqn, wqp, wqs, wuk, g_kv, c1, s1, c8, s8):
    b, t, d = x.shape
    dc = w_conv.shape[1]
    ql = g_q.shape[0]
    kl = g_kv.shape[0]
    tt = _tile(t, 256)
    hr = MLA_HEADS * ROPE_DIM
    once = lambda a: pl.BlockSpec(a.shape, lambda i, j: (0,) * a.ndim, pipeline_mode=pl.Buffered(1))
    wts = [g_norm.reshape(1, d), w_in]
    wts2 = [w_conv, g_q.reshape(1, ql), wqn, wqp, wqs, wuk, g_kv.reshape(1, kl)]
    in_specs = [pl.BlockSpec((1, tt, d), lambda i, j: (i, j, 0))] + [once(a) for a in wts] + [
        pl.BlockSpec((1, CONV_W - 1, dc), lambda i, j: (i, 0, 0)),
    ] + [once(a) for a in wts2] + [
        pl.BlockSpec((tt, ROPE_DIM), lambda i, j: (j, 0)),
        pl.BlockSpec((tt, ROPE_DIM), lambda i, j: (j, 0)),
        pl.BlockSpec((tt, hr), lambda i, j: (j, 0)),
        pl.BlockSpec((tt, hr), lambda i, j: (j, 0)),
    ]
    out_shape = [
        jax.ShapeDtypeStruct((b, t, dc), BF16),
        jax.ShapeDtypeStruct((b, CONV_W - 1, dc), F32),
        jax.ShapeDtypeStruct((b, MLA_HEADS, t, kl + ROPE_DIM), BF16),
        jax.ShapeDtypeStruct((b, t, kl), F32),
        jax.ShapeDtypeStruct((b, t, ROPE_DIM), F32),
        jax.ShapeDtypeStruct((b, t, kl + ROPE_DIM), BF16),
    ]
    out_specs = [
        pl.BlockSpec((1, tt, dc), lambda i, j: (i, j, 0)),
        pl.BlockSpec((1, CONV_W - 1, dc), lambda i, j: (i, 0, 0)),
        pl.BlockSpec((1, MLA_HEADS, tt, kl + ROPE_DIM), lambda i, j: (i, 0, j, 0)),
        pl.BlockSpec((1, tt, kl), lambda i, j: (i, j, 0)),
        pl.BlockSpec((1, tt, ROPE_DIM), lambda i, j: (i, j, 0)),
        pl.BlockSpec((1, tt, kl + ROPE_DIM), lambda i, j: (i, j, 0)),
    ]
    return pl.pallas_call(
        functools.partial(_even_in_kernel, dc=dc, ql=ql, kl=kl),
        grid=(b, t // tt),
        in_specs=in_specs,
        out_specs=out_specs,
        out_shape=out_shape,
        scratch_shapes=[pltpu.VMEM((SUBLANES, dc), F32)],
        compiler_params=_params("parallel", "arbitrary"),
        name="even_in",
    )(x, *wts, conv_prev, *wts2, c1, s1, c8, s8)


def _online_softmax(score_fns, value_fns, mask_last):
    c = MLA_SCALE * LOG2E
    n = len(score_fns)
    s = score_fns[0]()
    m = l = acc = None
    for j in range(n):
        s_next = score_fns[j + 1]() if j + 1 < n else None
        if j == n - 1 and mask_last is not None:
            s = mask_last(s)
        m_new = jnp.max(s, axis=-1, keepdims=True)
        if m is not None:
            m_new = jnp.maximum(m, m_new)
        p = jnp.exp2((s - m_new) * c)
        p_sum = jnp.sum(p, axis=-1, keepdims=True)
        pv = _dot(p.astype(BF16), value_fns[j]())
        if m is None:
            l, acc = p_sum, pv
        else:
            alpha = jnp.exp2((m - m_new) * c)
            l = alpha * l + p_sum
            acc = alpha * acc + pv
        m, s = m_new, s_next
    return acc * (1.0 / l)


def _value_up(o, wuv_ref, o_ref, tq):
    o = o.astype(BF16)
    for h in range(MLA_HEADS):
        o_ref[0, :, h * V_DIM:(h + 1) * V_DIM] = _dot(o[h * tq:(h + 1) * tq], wuv_ref[h]).astype(BF16)


def _attn_kernel(q_ref, kv_ref, wuv_ref, o_ref, *, tq, tk, kl):
    i = pl.program_id(1)
    rows = MLA_HEADS * tq
    q = q_ref[0].reshape(rows, q_ref.shape[3])
    tok = i * tq + lax.rem(lax.broadcasted_iota(jnp.int32, (rows, 1), 0), tq)
    limit = (tok // CHUNK + 1) * CHUNK

    def run(n):
        def mask_last(s):
            kpos = (n - 1) * tk + lax.broadcasted_iota(jnp.int32, (1, tk), 1)
            return jnp.where(kpos < limit, s, NEG_BIG)

        score_fns = [functools.partial(lambda j: _dot_nt(q, kv_ref[0, j * tk:(j + 1) * tk, :]), j) for j in range(n)]
        value_fns = [functools.partial(lambda j: kv_ref[0, j * tk:(j + 1) * tk, 0:kl], j) for j in range(n)]
        _value_up(_online_softmax(score_fns, value_fns, mask_last), wuv_ref, o_ref, tq)

    n_kv = (i * tq) // tk + 1
    for n in range(1, kv_ref.shape[1] // tk + 1):
        pl.when(n_kv == n)(functools.partial(run, n))


def _attn_cache_kernel(q_ref, kv_ref, cc_ref, ck_ref, wuv_ref, o_ref, *, tk, kl):
    tq = q_ref.shape[2]
    rows = MLA_HEADS * tq
    q = q_ref[0].reshape(rows, q_ref.shape[3])
    q_lat, q_pe = q[:, 0:kl], q[:, kl:]
    n_cache = cc_ref.shape[1] // tk

    def cache_scores(j):
        ks = slice(j * tk, (j + 1) * tk)
        return _dot_nt(q_lat, cc_ref[0, ks, :].astype(BF16)) + _dot_nt(q_pe, ck_ref[0, ks, :].astype(BF16))

    score_fns = [functools.partial(cache_scores, j) for j in range(n_cache)]
    value_fns = [functools.partial(lambda j: cc_ref[0, j * tk:(j + 1) * tk, :].astype(BF16), j) for j in range(n_cache)]
    score_fns.append(lambda: _dot_nt(q, kv_ref[0]))
    value_fns.append(lambda: kv_ref[0, :, 0:kl])
    _value_up(_online_softmax(score_fns, value_fns, None), wuv_ref, o_ref, tq)


def _attention_cached(q, kv_new, cache_ckv, cache_kpe, layer, wuv):
    b, h, t, dq = q.shape
    past = cache_ckv.shape[2]
    kl = wuv.shape[1]
    tk = _tile(past, 512)
    assert past % tk == 0
    return pl.pallas_call(
        functools.partial(_attn_cache_kernel, tk=tk, kl=kl),
        grid=(b,),
        in_specs=[
            pl.BlockSpec((1, h, t, dq), lambda i: (i, 0, 0, 0)),
            pl.BlockSpec((1, t, dq), lambda i: (i, 0, 0)),
            pl.BlockSpec((None, 1, past, kl), lambda i: (layer, i, 0, 0)),
            pl.BlockSpec((None, 1, past, dq - kl), lambda i: (layer, i, 0, 0)),
            pl.BlockSpec(wuv.shape, lambda i: (0, 0, 0)),
        ],
        out_specs=pl.BlockSpec((1, t, h * V_DIM), lambda i: (i, 0, 0)),
        out_shape=jax.ShapeDtypeStruct((b, t, h * V_DIM), BF16),
        compiler_params=_params("parallel"),
        name="mla_attention_cached",
    )(q, kv_new, cache_ckv, cache_kpe, wuv)


def _attention(q, kv, wuv):
    b, h, t, dq = q.shape
    tkv = kv.shape[1]
    kl = wuv.shape[1]
    tq = _tile(t, 128)
    tk = _tile(tkv, 512)
    assert tk % tq == 0 and tkv % tk == 0 and tq % CHUNK == 0
    return pl.pallas_call(
        functools.partial(_attn_kernel, tq=tq, tk=tk, kl=kl),
        grid=(b, t // tq),
        in_specs=[
            pl.BlockSpec((1, h, tq, dq), lambda i, j: (i, 0, j, 0)),
            pl.BlockSpec((1, tkv, dq), lambda i, j: (i, 0, 0)),
            pl.BlockSpec(wuv.shape, lambda i, j: (0, 0, 0)),
        ],
        out_specs=pl.BlockSpec((1, tq, h * V_DIM), lambda i, j: (i, j, 0)),
        out_shape=jax.ShapeDtypeStruct((b, t, h * V_DIM), BF16),
        compiler_params=_params("parallel", "arbitrary"),
        name="mla_attention",
    )(q, kv, wuv)


def _level_masks():
    t = np.arange(CHUNK)[:, None]
    s = np.arange(CHUNK)[None, :]
    masks = [(t // (2 * m) == s // (2 * m)) & ((t & m) != 0) & ((s & m) == 0) for m in LEVELS]
    return np.stack(masks).astype(np.float32)


def _boundary_rows(a, m):
    w = a.shape[1]
    if m >= SUBLANES:
        parts = [jnp.broadcast_to(a[b0 + m - 1:b0 + m], (2 * m, w)) for b0 in range(0, CHUNK, 2 * m)]
        return parts[0] if len(parts) == 1 else jnp.concatenate(parts, axis=0)
    a3 = a.reshape(CHUNK // SUBLANES, SUBLANES, w)
    sub = lax.broadcasted_iota(jnp.int32, a3.shape, 1)
    out = None
    for b0 in range(SUBLANES - 2 * m, -1, -2 * m):
        piece = jnp.broadcast_to(a3[:, b0 + m - 1:b0 + m, :], a3.shape)
        out = piece if out is None else jnp.where(sub < b0 + 2 * m, piece, out)
    return out.reshape(CHUNK, w)


def _level_exponent(a, m):
    if m < SUBLANES:
        return -jnp.abs(a - _boundary_rows(a, m))
    parts = []
    for b0 in range(0, CHUNK, 2 * m):
        r = a[b0 + m - 1:b0 + m]
        parts += [r - a[b0:b0 + m], a[b0 + m:b0 + 2 * m] - r]
    return jnp.concatenate(parts, axis=0)


def _pick_halves(first, second, m, is_second):
    if m < SUBLANES:
        return jnp.where(is_second, second, first)
    parts = []
    for b0 in range(0, CHUNK, 2 * m):
        parts += [first[b0:b0 + m], second[b0 + m:b0 + 2 * m]]
    return jnp.concatenate(parts, axis=0)


def _gla_kernel(zq_ref, zf_ref, zi_ref, zg_ref, lbl_ref, go_ref, s0_ref, mk_ref, o_ref, st_ref, *, layer):
    t = pl.program_id(1)
    tc = zq_ref.shape[1]
    n_lv = len(LEVELS)

    @pl.when(t == 0)
    def _():
        st_ref[...] = s0_ref[...]

    ll = lbl_ref[...]
    e = jnp.exp(ll - jnp.max(ll, axis=0, keepdims=True))
    sm = e / jnp.sum(e, axis=0, keepdims=True)
    lb = jnp.zeros((1, ll.shape[1]), F32)
    for r in range(1, layer + 1):
        lb = lb + sm[r:r + 1]
    one_m_lb = 1.0 - lb
    go = go_ref[...]
    row = lax.broadcasted_iota(jnp.int32, (CHUNK, 1), 0)
    second = [(row & m) != 0 for m in LEVELS]
    odd = (row & 1) != 0

    def wide(c):
        rs = pl.ds(pl.multiple_of(c * CHUNK, CHUNK), CHUNK)
        zq = zq_ref[0, rs, :]
        zf = zf_ref[0, rs, :]
        v = zi_ref[0, rs, :]
        zg = zg_ref[0, rs, :]
        sig = _sigmoid(zf)
        g = jnp.log(lb + one_m_lb * sig) * LOG2E
        k = one_m_lb * (1.0 - sig)
        q = _silu(zq)
        gate = _silu(zg)
        a = g
        for sh in (1, 2, 4):
            rolled = pltpu.roll(a, sh, 0)
            top = jnp.where(row[:SUBLANES] >= sh, rolled[:SUBLANES], 0.0)
            a = a + jnp.concatenate([top, rolled[SUBLANES:]], axis=0)
        for sh in (8, 16, 32):
            a = a + jnp.concatenate([jnp.zeros((sh, a.shape[1]), F32), a[:CHUNK - sh]], axis=0)
        e_q = jnp.exp2(a)
        e_k = jnp.exp2(a[CHUNK - 1:CHUNK] - a)
        e_last = e_q[CHUNK - 1:CHUNK]
        qt = (q * e_q).astype(BF16)
        kt = (k * e_k).astype(BF16)
        qk = q * k
        vb = v.astype(BF16)
        pair = q * jnp.exp2(g) * pltpu.roll(k, 1, 0)
        v_prev = pltpu.roll(v, 1, 0)
        ops = []
        for li, m in enumerate(LEVELS):
            ex = jnp.exp2(_level_exponent(a, m))
            ops.append((_pick_halves(k, q, m, second[li]) * ex).astype(BF16))
        return ops, qt, kt, vb, qk, v, gate, e_last, pair, v_prev

    def heads(c, w, states):
        ops, qt, kt, vb, qk, v, gate, e_last, pair, v_prev = w
        rs = pl.ds(pl.multiple_of(c * CHUNK, CHUNK), CHUNK)
        sls = [slice(h * HG_DK, (h + 1) * HG_DK) for h in range(HG_HEADS)]
        ps = []
        for sl in sls:
            groups = [None] * (CHUNK // SUBLANES)
            for li, m in enumerate(LEVELS):
                r = _dot_nt(ops[li][:, sl], ops[li][:, sl])
                for gi in range(len(groups)):
                    if m >= SUBLANES and not (gi * SUBLANES) & m:
                        continue
                    gs = slice(gi * SUBLANES, (gi + 1) * SUBLANES)
                    term = r[gs] * mk_ref[li, gs, :]
                    groups[gi] = term if groups[gi] is None else groups[gi] + term
            ps.append(jnp.concatenate(groups, axis=0).astype(BF16))
        new_states = [states[h] * e_last[:, sl] + _dot_tn(vb[:, sl], kt[:, sl]) for h, sl in enumerate(sls)]
        os_ = [_dot_nt(qt[:, sl], states[h].astype(BF16)) + _dot(ps[h], vb[:, sl]) for h, sl in enumerate(sls)]
        for h, sl in enumerate(sls):
            o = os_[h] + jnp.sum(qk[:, sl], axis=-1, keepdims=True) * v[:, sl]
            c1 = jnp.where(odd, jnp.sum(pair[:, sl], axis=-1, keepdims=True), 0.0)
            o = o + c1 * v_prev[:, sl]
            og = _rms(o, go) * gate[:, sl]
            o_ref[0, rs, sl] = og.astype(BF16)
        return new_states

    def chunk_body(c, carry):
        states = heads(c, wide(c), [st_ref[0, h] for h in range(HG_HEADS)])
        for h in range(HG_HEADS):
            st_ref[0, h] = states[h]
        return carry

    lax.fori_loop(0, tc // CHUNK, chunk_body, 0)


def _gla(z, lb_logits, layer, g_o, s0t):
    b, t, zw = z.shape
    dh = zw // 4
    tc = _tile(t, 256)
    masks = jnp.asarray(_level_masks(), F32)
    zspec = lambda c: pl.BlockSpec((1, tc, dh), lambda i, j, c=c: (i, j, c))
    sspec = pl.BlockSpec((1,) + s0t.shape[1:], lambda i, j: (i, 0, 0, 0))
    return pl.pallas_call(
        functools.partial(_gla_kernel, layer=layer),
        grid=(b, t // tc),
        in_specs=[zspec(0), zspec(1), zspec(2), zspec(3),
                  pl.BlockSpec(lb_logits.shape, lambda i, j: (0, 0)),
                  pl.BlockSpec((1, HG_DV), lambda i, j: (0, 0)),
                  sspec,
                  pl.BlockSpec(masks.shape, lambda i, j: (0, 0, 0))],
        out_specs=[pl.BlockSpec((1, tc, dh), lambda i, j: (i, j, 0)), sspec],
        out_shape=[jax.ShapeDtypeStruct((b, t, dh), BF16), jax.ShapeDtypeStruct(s0t.shape, F32)],
        compiler_params=_params("parallel", "arbitrary"),
        name="hgrn2_recurrence",
    )(z, z, z, z, lb_logits, g_o.reshape(1, HG_DV), s0t, masks)


def _rope_tables(pos):
    half = ROPE_DIM // 2
    inv = ROPE_THETA ** (-jnp.arange(half, dtype=F32) / half)
    ang = pos.astype(F32)[:, None] * inv[None, :]
    cos, sin = jnp.cos(ang), jnp.sin(ang)
    c1 = jnp.concatenate([cos, cos], axis=1)
    s1 = jnp.concatenate([-sin, sin], axis=1)
    return c1, s1, jnp.tile(c1, (1, MLA_HEADS)), jnp.tile(s1, (1, MLA_HEADS))


def _swap_halves(w):
    k, c = w.shape
    w = w.reshape(k, c // ROPE_DIM, 2, ROPE_DIM // 2)
    return w[:, :, ::-1, :].reshape(k, c)


def _prep_even(w_in_a, w_uq, w_ukv, w_out_a, dc):
    d = w_in_a.shape[0]
    kpe = w_in_a[:, -ROPE_DIM:]
    w_in = jnp.concatenate([w_in_a, _swap_halves(kpe)], axis=1).astype(BF16)
    ql = w_uq.shape[0]
    wq = w_uq.reshape(ql, MLA_HEADS, NOPE_DIM + ROPE_DIM)
    wqn = wq[:, :, :NOPE_DIM].reshape(ql, MLA_HEADS * NOPE_DIM).astype(BF16)
    wqp = wq[:, :, NOPE_DIM:].reshape(ql, MLA_HEADS * ROPE_DIM)
    wqs = _swap_halves(wqp).astype(BF16)
    kl = w_ukv.shape[0]
    wkv = w_ukv.reshape(kl, MLA_HEADS, NOPE_DIM + V_DIM)
    wuk = jnp.transpose(wkv[:, :, :NOPE_DIM], (1, 2, 0)).astype(BF16)
    wuv = jnp.transpose(wkv[:, :, NOPE_DIM:], (1, 0, 2)).astype(BF16)
    wo = w_out_a.astype(BF16)
    return dict(w_in=w_in, wqn=wqn, wqp=wqp.astype(BF16), wqs=wqs, wuk=wuk, wuv=wuv, wo_a=wo[:dc], wo_b=wo[dc:])


def _run_group(x3, pos, conv_prev, ckv_prev, kpe_prev, hgrn_prev, p):
    b, t, d = x3.shape
    n = b * t
    depth = p["norm_ffn1"].shape[0]
    x = x3.reshape(n, d)
    c1, s1, c8, s8 = _rope_tables(pos)
    conv_new, ckv_new, kpe_new, hgrn_new = [], [], [], []
    for l in range(depth):
        x = _ffn(x, p["norm_ffn1"], p["wg1"], p["wu1"], p["wd1"], l)
        if l % 2 == 0:
            e = l // 2
            pe = p["even"][e]
            dc = p["w_conv"].shape[2]
            cprev = jnp.zeros((b, CONV_W - 1, dc), F32) if conv_prev is None else conv_prev[e]
            ya, cs, q, ckv, kpe, kv = _even_in(x.reshape(b, t, d), p["norm_mix"][l], pe["w_in"], cprev,
                                               p["w_conv"][e], p["g_q"][e], pe["wqn"], pe["wqp"], pe["wqs"],
                                               pe["wuk"], p["g_kv"][e], c1, s1, c8, s8)
            if ckv_prev is None:
                yb = _attention(q, kv, pe["wuv"])
            else:
                yb = _attention_cached(q, kv, ckv_prev, kpe_prev, e, pe["wuv"])
            x = _matmul_residual([ya.reshape(n, -1), yb.reshape(n, -1)], [pe["wo_a"], pe["wo_b"]], x)
            conv_new.append(cs)
            ckv_new.append(ckv)
            kpe_new.append(kpe)
        else:
            o = l // 2
            z = _norm_matmul(x, p["norm_mix"][l], p["w_in_c"], o).reshape(b, t, -1)
            if hgrn_prev is None:
                s0t = jnp.zeros((b, HG_HEADS, HG_DV, HG_DK), F32)
            else:
                s0t = jnp.swapaxes(hgrn_prev[o], -1, -2)
            og, st = _gla(z, p["lb_logits"], o, p["g_o"][o], s0t)
            x = _matmul_residual([og.reshape(n, -1)], [p["w_out_c"][o]], x)
            hgrn_new.append(jnp.swapaxes(st, -1, -2))
        g_final = p["norm_final"] if l == depth - 1 else None
        x = _ffn(x, p["norm_ffn2"], p["wg2"], p["wu2"], p["wd2"], l, g_final)
    return x.reshape(b, t, d), jnp.stack(conv_new), jnp.stack(ckv_new), jnp.stack(kpe_new), jnp.stack(hgrn_new)


def kernel(x_prompt, x_sample, cache_conv, cache_ckv, cache_kpe, state_hgrn, norm_ffn1, w_ffn1_gate, w_ffn1_up, w_ffn1_down, norm_mix, w_in_a, w_conv, g_q, w_uq, g_kv, w_ukv, w_out_a, w_in_c, lb_logits, g_o, w_out_c, norm_ffn2, w_ffn2_gate, w_ffn2_up, w_ffn2_down, norm_final):
    dc = w_conv.shape[2]
    p = dict(
        norm_ffn1=norm_ffn1, wg1=w_ffn1_gate.astype(BF16), wu1=w_ffn1_up.astype(BF16), wd1=w_ffn1_down.astype(BF16),
        norm_ffn2=norm_ffn2, wg2=w_ffn2_gate.astype(BF16), wu2=w_ffn2_up.astype(BF16), wd2=w_ffn2_down.astype(BF16),
        norm_mix=norm_mix, w_conv=w_conv, g_q=g_q, g_kv=g_kv,
        even=[_prep_even(w_in_a[e], w_uq[e], w_ukv[e], w_out_a[e], dc) for e in range(w_in_a.shape[0])],
        w_in_c=w_in_c.astype(BF16), lb_logits=lb_logits, g_o=g_o, w_out_c=w_out_c.astype(BF16),
        norm_final=norm_final,
    )
    pos_p = jnp.arange(x_prompt.shape[1], dtype=jnp.int32)
    pos_s = cache_ckv.shape[2] + jnp.arange(x_sample.shape[1], dtype=jnp.int32)
    y_p, conv_p, ckv_p, kpe_p, hgrn_p = _run_group(x_prompt, pos_p, None, None, None, None, p)
    y_s, conv_s, ckv_s, kpe_s, hgrn_s = _run_group(x_sample, pos_s, cache_conv, cache_ckv, cache_kpe, state_hgrn, p)
    return (y_p, y_s, conv_p, ckv_p, kpe_p, hgrn_p, conv_s, ckv_s, kpe_s, hgrn_s)
```

```python
import functools

import numpy as np
import jax
import jax.numpy as jnp
from jax import lax
from jax.experimental import pallas as pl
from jax.experimental.pallas import tpu as pltpu

F32 = jnp.float32
BF16 = jnp.bfloat16

EPS = 1e-6
CHUNK = 64
CONV_W = 3
MLA_HEADS = 8
NOPE_DIM = 128
ROPE_DIM = 64
V_DIM = 128
ROPE_THETA = 10000.0
MLA_SCALE = (NOPE_DIM + ROPE_DIM) ** -0.5
HG_HEADS = 16
HG_DK = 128
HG_DV = 128
LANES = 128
SUBLANES = 8
NEG_BIG = -1e30
LOG2E = 1.4426950408889634
LEVELS = (32, 16, 8, 4, 2)
VMEM_LIMIT_BYTES = 60 * 1024 * 1024
FFN_TF = 512


def _tile(n, pref):
    if n <= pref:
        return n
    for t in range(pref, 7, -8):
        if n % t == 0:
            return t
    return n


def _col_tile(n, pref):
    for t in range(min(pref, n) // LANES * LANES, 0, -LANES):
        if n % t == 0:
            return t
    return n


def _params(*sem, **kw):
    return pltpu.CompilerParams(dimension_semantics=sem, vmem_limit_bytes=VMEM_LIMIT_BYTES, **kw)


def _sigmoid(x):
    return 1.0 / (1.0 + jnp.exp(-x))


def _silu(x):
    h = 0.5 * x
    return h + h * jnp.tanh(h)


def _rms(x, g):
    return x * lax.rsqrt(jnp.mean(x * x, axis=-1, keepdims=True) + EPS) * g


def _dot(a, b):
    return jnp.dot(a, b, preferred_element_type=F32)


def _dot_nt(a, b):
    return lax.dot_general(a, b, (((1,), (1,)), ((), ())), preferred_element_type=F32)


def _dot_tn(a, b):
    return lax.dot_general(a, b, (((0,), (0,)), ((), ())), preferred_element_type=F32)


def _ffn_kernel(x_ref, g_ref, wg_ref, wu_ref, wd_ref, *rest, final_norm):
    if final_norm:
        gf_ref, o_ref, n_ref = rest
    else:
        o_ref, n_ref = rest
    j = pl.program_id(1)
    tm = x_ref.shape[0]
    rc = min(tm, 128)

    def rows(i):
        return pl.ds(pl.multiple_of(i * rc, rc), rc)

    @pl.when(j == 0)
    def _():
        def norm_rows(i, carry):
            x = x_ref[rows(i), :]
            n_ref[rows(i), :] = _rms(x, g_ref[...]).astype(BF16)
            o_ref[rows(i), :] = x
            return carry

        lax.fori_loop(0, tm // rc, norm_rows, 0)

    n = n_ref[...]
    hg = _dot(n, wg_ref[...])
    hu = _dot(n, wu_ref[...])
    a = _silu(hg) * (0.5 * hu)
    o_ref[...] += _dot(a.astype(BF16), wd_ref[...])

    if final_norm:
        @pl.when(j == pl.num_programs(1) - 1)
        def _():
            def norm_out_rows(i, carry):
                o_ref[rows(i), :] = _rms(o_ref[rows(i), :], gf_ref[...])
                return carry

            lax.fori_loop(0, tm // rc, norm_out_rows, 0)


def _ffn(x, g, wg, wu, wd, layer, g_final=None):
    n, d = x.shape
    f = wg.shape[2]
    tm = _tile(n, 1024)
    tf = _col_tile(f, FFN_TF)
    in_specs = [
        pl.BlockSpec((tm, d), lambda i, j: (i, 0)),
        pl.BlockSpec((None, 1, d), lambda i, j: (layer, 0, 0)),
        pl.BlockSpec((None, d, tf), lambda i, j: (layer, 0, j)),
        pl.BlockSpec((None, d, tf), lambda i, j: (layer, 0, j)),
        pl.BlockSpec((None, tf, d), lambda i, j: (layer, j, 0)),
    ]
    args = [x, g.reshape(g.shape[0], 1, d), wg, wu, wd]
    if g_final is not None:
        in_specs.append(pl.BlockSpec((1, d), lambda i, j: (0, 0)))
        args.append(g_final.reshape(1, d))
    return pl.pallas_call(
        functools.partial(_ffn_kernel, final_norm=g_final is not None),
        grid=(n // tm, f // tf),
        in_specs=in_specs,
        out_specs=pl.BlockSpec((tm, d), lambda i, j: (i, 0)),
        out_shape=jax.ShapeDtypeStruct((n, d), F32),
        scratch_shapes=[pltpu.VMEM((tm, d), BF16)],
        compiler_params=_params("parallel", "arbitrary"),
        name="ffn",
    )(*args)


def _nmm_kernel(x_ref, g_ref, w_ref, o_ref, n_ref):
    tm = x_ref.shape[0]
    rc = min(tm, 128)

    @pl.when(pl.program_id(1) == 0)
    def _():
        def norm_rows(i, carry):
            rows = pl.ds(pl.multiple_of(i * rc, rc), rc)
            n_ref[rows, :] = _rms(x_ref[rows, :], g_ref[...]).astype(BF16)
            return carry

        lax.fori_loop(0, tm // rc, norm_rows, 0)

    o_ref[...] = _dot(n_ref[...], w_ref[...])


def _norm_matmul(x, g, w, layer):
    n, d = x.shape
    c = w.shape[2]
    tm = _tile(n, 1024)
    tn = _col_tile(c, 2048)
    return pl.pallas_call(
        _nmm_kernel,
        grid=(n // tm, c // tn),
        in_specs=[
            pl.BlockSpec((tm, d), lambda i, j: (i, 0)),
            pl.BlockSpec((1, d), lambda i, j: (0, 0)),
            pl.BlockSpec((None, d, tn), lambda i, j: (layer, 0, j)),
        ],
        out_specs=pl.BlockSpec((tm, tn), lambda i, j: (i, j)),
        out_shape=jax.ShapeDtypeStruct((n, c), F32),
        scratch_shapes=[pltpu.VMEM((tm, d), BF16)],
        compiler_params=_params("parallel", "arbitrary"),
        name="norm_matmul",
    )(x, g.reshape(1, d), w)


def _mmres_kernel(*refs, n_in):
    x_ref, o_ref = refs[2 * n_in], refs[2 * n_in + 1]
    acc = x_ref[...]
    for a_ref, w_ref in zip(refs[:n_in], refs[n_in:2 * n_in]):
        acc = acc + _dot(a_ref[...], w_ref[...])
    o_ref[...] = acc


def _matmul_residual(acts, weights, x):
    n, d = x.shape
    tm = _tile(n, 512)
    in_specs = [pl.BlockSpec((tm, a.shape[1]), lambda i: (i, 0)) for a in acts]
    in_specs += [pl.BlockSpec(w.shape, lambda i: (0, 0)) for w in weights]
    in_specs.append(pl.BlockSpec((tm, d), lambda i: (i, 0)))
    return pl.pallas_call(
        functools.partial(_mmres_kernel, n_in=len(acts)),
        grid=(n // tm,),
        in_specs=in_specs,
        out_specs=pl.BlockSpec((tm, d), lambda i: (i, 0)),
        out_shape=jax.ShapeDtypeStruct((n, d), F32),
        compiler_params=_params("parallel"),
        name="matmul_residual",
    )(*acts, *weights, x)


def _even_in_kernel(x_ref, gn_ref, win_ref, cp_ref, wc_ref, gq_ref, wqn_ref, wqp_ref, wqs_ref, wuk_ref, gkv_ref,
                    c1_ref, s1_ref, c8_ref, s8_ref,
                    ya_ref, cs_ref, q_ref, ckv_ref, kpe_ref, kv_ref, vp_ref, *, dc, ql, kl):
    t = pl.program_id(1)
    tt = x_ref.shape[1]
    o3 = 3 * dc
    o4 = o3 + ql
    o5 = o4 + kl
    n = _rms(x_ref[0], gn_ref[...]).astype(BF16)

    def proj(c0, c1):
        return _dot(n, win_ref[:, c0:c1])

    cqn = _rms(proj(o3, o4), gq_ref[...]).astype(BF16)
    qn = _dot(cqn, wqn_ref[...])
    qpr = _dot(cqn, wqp_ref[...]) * c8_ref[...] + _dot(cqn, wqs_ref[...]) * s8_ref[...]
    v = proj(dc, 2 * dc) * proj(2 * dc, o3)
    gb = proj(0, dc)
    cn = _rms(proj(o4, o5), gkv_ref[...])
    kp = proj(o5, o5 + 2 * ROPE_DIM)
    kr = kp[:, 0:ROPE_DIM] * c1_ref[...] + kp[:, ROPE_DIM:2 * ROPE_DIM] * s1_ref[...]
    for h in range(MLA_HEADS):
        qlat = _dot(qn[:, h * NOPE_DIM:(h + 1) * NOPE_DIM].astype(BF16), wuk_ref[h])
        q_ref[0, h, :, 0:kl] = qlat.astype(BF16)
        q_ref[0, h, :, kl:kl + ROPE_DIM] = qpr[:, h * ROPE_DIM:(h + 1) * ROPE_DIM].astype(BF16)
    cp = cp_ref[0]
    first = t == 0
    pm2 = jnp.where(first, cp[0:1], vp_ref[SUBLANES - 2:SUBLANES - 1, :])
    pm1 = jnp.where(first, cp[1:2], vp_ref[SUBLANES - 1:SUBLANES, :])
    rows = lax.broadcasted_iota(jnp.int32, v.shape, 0)
    v1 = jnp.where(rows == 0, pm1, pltpu.roll(v, 1, 0))
    v2 = jnp.where(rows == 0, pm2, jnp.where(rows == 1, pm1, pltpu.roll(v, 2, 0)))
    w = wc_ref[...]
    conv = v2 * w[0:1] + v1 * w[1:2] + v * w[2:3]
    ya_ref[0] = (gb * conv).astype(BF16)
    vp_ref[...] = v[tt - SUBLANES:tt]
    ckv_ref[0] = cn
    kpe_ref[0] = kr
    kv_ref[0, :, 0:kl] = cn.astype(BF16)
    kv_ref[0, :, kl:kl + ROPE_DIM] = kr.astype(BF16)

    @pl.when(t == pl.num_programs(1) - 1)
    def _():
        cs_ref[0] = v[tt - (CONV_W - 1):tt]


def _even_in(x, g_norm, w_in, conv_prev, w_conv, g_q, wqn, wqp, wqs, wuk, g_kv, c1, s1, c8, s8):
    b, t, d = x.shape
    dc = w_conv.shape[1]
    ql = g_q.shape[0]
    kl = g_kv.shape[0]
    tt = _tile(t, 256)
    hr = MLA_HEADS * ROPE_DIM
    once = lambda a: pl.BlockSpec(a.shape, lambda i, j: (0,) * a.ndim, pipeline_mode=pl.Buffered(1))
    wts = [g_norm.reshape(1, d), w_in]
    wts2 = [w_conv, g_q.reshape(1, ql), wqn, wqp, wqs, wuk, g_kv.reshape(1, kl)]
    in_specs = [pl.BlockSpec((1, tt, d), lambda i, j: (i, j, 0))] + [once(a) for a in wts] + [
        pl.BlockSpec((1, CONV_W - 1, dc), lambda i, j: (i, 0, 0)),
    ] + [once(a) for a in wts2] + [
        pl.BlockSpec((tt, ROPE_DIM), lambda i, j: (j, 0)),
        pl.BlockSpec((tt, ROPE_DIM), lambda i, j: (j, 0)),
        pl.BlockSpec((tt, hr), lambda i, j: (j, 0)),
        pl.BlockSpec((tt, hr), lambda i, j: (j, 0)),
    ]
    out_shape = [
        jax.ShapeDtypeStruct((b, t, dc), BF16),
        jax.ShapeDtypeStruct((b, CONV_W - 1, dc), F32),
        jax.ShapeDtypeStruct((b, MLA_HEADS, t, kl + ROPE_DIM), BF16),
        jax.ShapeDtypeStruct((b, t, kl), F32),
        jax.ShapeDtypeStruct((b, t, ROPE_DIM), F32),
        jax.ShapeDtypeStruct((b, t, kl + ROPE_DIM), BF16),
    ]
    out_specs = [
        pl.BlockSpec((1, tt, dc), lambda i, j: (i, j, 0)),
        pl.BlockSpec((1, CONV_W - 1, dc), lambda i, j: (i, 0, 0)),
        pl.BlockSpec((1, MLA_HEADS, tt, kl + ROPE_DIM), lambda i, j: (i, 0, j, 0)),
        pl.BlockSpec((1, tt, kl), lambda i, j: (i, j, 0)),
        pl.BlockSpec((1, tt, ROPE_DIM), lambda i, j: (i, j, 0)),
        pl.BlockSpec((1, tt, kl + ROPE_DIM), lambda i, j: (i, j, 0)),
    ]
    return pl.pallas_call(
        functools.partial(_even_in_kernel, dc=dc, ql=ql, kl=kl),
        grid=(b, t // tt),
        in_specs=in_specs,
        out_specs=out_specs,
        out_shape=out_shape,
        scratch_shapes=[pltpu.VMEM((SUBLANES, dc), F32)],
        compiler_params=_params("parallel", "arbitrary"),
        name="even_in",
    )(x, *wts, conv_prev, *wts2, c1, s1, c8, s8)


def _online_softmax(score_fns, value_fns, mask_last):
    c = MLA_SCALE * LOG2E
    n = len(score_fns)
    s = score_fns[0]()
    m = l = acc = None
    for j in range(n):
        s_next = score_fns[j + 1]() if j + 1 < n else None
        if j == n - 1 and mask_last is not None:
            s = mask_last(s)
        m_new = jnp.max(s, axis=-1, keepdims=True)
        if m is not None:
            m_new = jnp.maximum(m, m_new)
        p = jnp.exp2((s - m_new) * c)
        p_sum = jnp.sum(p, axis=-1, keepdims=True)
        pv = _dot(p.astype(BF16), value_fns[j]())
        if m is None:
            l, acc = p_sum, pv
        else:
            alpha = jnp.exp2((m - m_new) * c)
            l = alpha * l + p_sum
            acc = alpha * acc + pv
        m, s = m_new, s_next
    return acc * (1.0 / l)


def _value_up(o, wuv_ref, o_ref, tq):
    o = o.astype(BF16)
    for h in range(MLA_HEADS):
        o_ref[0, :, h * V_DIM:(h + 1) * V_DIM] = _dot(o[h * tq:(h + 1) * tq], wuv_ref[h]).astype(BF16)


def _attn_kernel(q_ref, kv_ref, wuv_ref, o_ref, *, tq, tk, kl):
    i = pl.program_id(1)
    rows = MLA_HEADS * tq
    q = q_ref[0].reshape(rows, q_ref.shape[3])
    tok = i * tq + lax.rem(lax.broadcasted_iota(jnp.int32, (rows, 1), 0), tq)
    limit = (tok // CHUNK + 1) * CHUNK

    def run(n):
        def mask_last(s):
            kpos = (n - 1) * tk + lax.broadcasted_iota(jnp.int32, (1, tk), 1)
            return jnp.where(kpos < limit, s, NEG_BIG)

        score_fns = [functools.partial(lambda j: _dot_nt(q, kv_ref[0, j * tk:(j + 1) * tk, :]), j) for j in range(n)]
        value_fns = [functools.partial(lambda j: kv_ref[0, j * tk:(j + 1) * tk, 0:kl], j) for j in range(n)]
        _value_up(_online_softmax(score_fns, value_fns, mask_last), wuv_ref, o_ref, tq)

    n_kv = (i * tq) // tk + 1
    for n in range(1, kv_ref.shape[1] // tk + 1):
        pl.when(n_kv == n)(functools.partial(run, n))


def _attn_cache_kernel(q_ref, kv_ref, cc_ref, ck_ref, wuv_ref, o_ref, *, tk, kl):
    tq = q_ref.shape[2]
    rows = MLA_HEADS * tq
    q = q_ref[0].reshape(rows, q_ref.shape[3])
    q_lat, q_pe = q[:, 0:kl], q[:, kl:]
    n_cache = cc_ref.shape[1] // tk

    def cache_scores(j):
        ks = slice(j * tk, (j + 1) * tk)
        return _dot_nt(q_lat, cc_ref[0, ks, :].astype(BF16)) + _dot_nt(q_pe, ck_ref[0, ks, :].astype(BF16))

    score_fns = [functools.partial(cache_scores, j) for j in range(n_cache)]
    value_fns = [functools.partial(lambda j: cc_ref[0, j * tk:(j + 1) * tk, :].astype(BF16), j) for j in range(n_cache)]
    score_fns.append(lambda: _dot_nt(q, kv_ref[0]))
    value_fns.append(lambda: kv_ref[0, :, 0:kl])
    _value_up(_online_softmax(score_fns, value_fns, None), wuv_ref, o_ref, tq)


def _attention_cached(q, kv_new, cache_ckv, cache_kpe, layer, wuv):
    b, h, t, dq = q.shape
    past = cache_ckv.shape[2]
    kl = wuv.shape[1]
    tk = _tile(past, 512)
    assert past % tk == 0
    return pl.pallas_call(
        functools.partial(_attn_cache_kernel, tk=tk, kl=kl),
        grid=(b,),
        in_specs=[
            pl.BlockSpec((1, h, t, dq), lambda i: (i, 0, 0, 0)),
            pl.BlockSpec((1, t, dq), lambda i: (i, 0, 0)),
            pl.BlockSpec((None, 1, past, kl), lambda i: (layer, i, 0, 0)),
            pl.BlockSpec((None, 1, past, dq - kl), lambda i: (layer, i, 0, 0)),
            pl.BlockSpec(wuv.shape, lambda i: (0, 0, 0)),
        ],
        out_specs=pl.BlockSpec((1, t, h * V_DIM), lambda i: (i, 0, 0)),
        out_shape=jax.ShapeDtypeStruct((b, t, h * V_DIM), BF16),
        compiler_params=_params("parallel"),
        name="mla_attention_cached",
    )(q, kv_new, cache_ckv, cache_kpe, wuv)


def _attention(q, kv, wuv):
    b, h, t, dq = q.shape
    tkv = kv.shape[1]
    kl = wuv.shape[1]
    tq = _tile(t, 128)
    tk = _tile(tkv, 512)
    assert tk % tq == 0 and tkv % tk == 0 and tq % CHUNK == 0
    return pl.pallas_call(
        functools.partial(_attn_kernel, tq=tq, tk=tk, kl=kl),
        grid=(b, t // tq),
        in_specs=[
            pl.BlockSpec((1, h, tq, dq), lambda i, j: (i, 0, j, 0)),
            pl.BlockSpec((1, tkv, dq), lambda i, j: (i, 0, 0)),
            pl.BlockSpec(wuv.shape, lambda i, j: (0, 0, 0)),
        ],
        out_specs=pl.BlockSpec((1, tq, h * V_DIM), lambda i, j: (i, j, 0)),
        out_shape=jax.ShapeDtypeStruct((b, t, h * V_DIM), BF16),
        compiler_params=_params("parallel", "arbitrary"),
        name="mla_attention",
    )(q, kv, wuv)


def _level_masks():
    t = np.arange(CHUNK)[:, None]
    s = np.arange(CHUNK)[None, :]
    masks = [(t // (2 * m) == s // (2 * m)) & ((t & m) != 0) & ((s & m) == 0) for m in LEVELS]
    return np.stack(masks).astype(np.float32)


def _boundary_rows(a, m):
    w = a.shape[1]
    if m >= SUBLANES:
        parts = [jnp.broadcast_to(a[b0 + m - 1:b0 + m], (2 * m, w)) for b0 in range(0, CHUNK, 2 * m)]
        return parts[0] if len(parts) == 1 else jnp.concatenate(parts, axis=0)
    a3 = a.reshape(CHUNK // SUBLANES, SUBLANES, w)
    sub = lax.broadcasted_iota(jnp.int32, a3.shape, 1)
    out = None
    for b0 in range(SUBLANES - 2 * m, -1, -2 * m):
        piece = jnp.broadcast_to(a3[:, b0 + m - 1:b0 + m, :], a3.shape)
        out = piece if out is None else jnp.where(sub < b0 + 2 * m, piece, out)
    return out.reshape(CHUNK, w)


def _level_exponent(a, m):
    if m < SUBLANES:
        return -jnp.abs(a - _boundary_rows(a, m))
    parts = []
    for b0 in range(0, CHUNK, 2 * m):
        r = a[b0 + m - 1:b0 + m]
        parts += [r - a[b0:b0 + m], a[b0 + m:b0 + 2 * m] - r]
    return jnp.concatenate(parts, axis=0)


def _pick_halves(first, second, m, is_second):
    if m < SUBLANES:
        return jnp.where(is_second, second, first)
    parts = []
    for b0 in range(0, CHUNK, 2 * m):
        parts += [first[b0:b0 + m], second[b0 + m:b0 + 2 * m]]
    return jnp.concatenate(parts, axis=0)


def _gla_kernel(zq_ref, zf_ref, zi_ref, zg_ref, lbl_ref, go_ref, s0_ref, mk_ref, o_ref, st_ref, *, layer):
    t = pl.program_id(1)
    tc = zq_ref.shape[1]
    n_lv = len(LEVELS)

    @pl.when(t == 0)
    def _():
        st_ref[...] = s0_ref[...]

    ll = lbl_ref[...]
    e = jnp.exp(ll - jnp.max(ll, axis=0, keepdims=True))
    sm = e / jnp.sum(e, axis=0, keepdims=True)
    lb = jnp.zeros((1, ll.shape[1]), F32)
    for r in range(1, layer + 1):
        lb = lb + sm[r:r + 1]
    one_m_lb = 1.0 - lb
    go = go_ref[...]
    row = lax.broadcasted_iota(jnp.int32, (CHUNK, 1), 0)
    second = [(row & m) != 0 for m in LEVELS]
    odd = (row & 1) != 0

    def wide(c):
        rs = pl.ds(pl.multiple_of(c * CHUNK, CHUNK), CHUNK)
        zq = zq_ref[0, rs, :]
        zf = zf_ref[0, rs, :]
        v = zi_ref[0, rs, :]
        zg = zg_ref[0, rs, :]
        sig = _sigmoid(zf)
        g = jnp.log(lb + one_m_lb * sig) * LOG2E
        k = one_m_lb * (1.0 - sig)
        q = _silu(zq)
        gate = _silu(zg)
        a = g
        for sh in (1, 2, 4):
            rolled = pltpu.roll(a, sh, 0)
            top = jnp.where(row[:SUBLANES] >= sh, rolled[:SUBLANES], 0.0)
            a = a + jnp.concatenate([top, rolled[SUBLANES:]], axis=0)
        for sh in (8, 16, 32):
            a = a + jnp.concatenate([jnp.zeros((sh, a.shape[1]), F32), a[:CHUNK - sh]], axis=0)
        e_q = jnp.exp2(a)
        e_k = jnp.exp2(a[CHUNK - 1:CHUNK] - a)
        e_last = e_q[CHUNK - 1:CHUNK]
        qt = (q * e_q).astype(BF16)
        kt = (k * e_k).astype(BF16)
        qk = q * k
        vb = v.astype(BF16)
        pair = q * jnp.exp2(g) * pltpu.roll(k, 1, 0)
        v_prev = pltpu.roll(v, 1, 0)
        ops = []
        for li, m in enumerate(LEVELS):
            ex = jnp.exp2(_level_exponent(a, m))
            ops.append((_pick_halves(k, q, m, second[li]) * ex).astype(BF16))
        return ops, qt, kt, vb, qk, v, gate, e_last, pair, v_prev

    def heads(c, w, states):
        ops, qt, kt, vb, qk, v, gate, e_last, pair, v_prev = w
        rs = pl.ds(pl.multiple_of(c * CHUNK, CHUNK), CHUNK)
        sls = [slice(h * HG_DK, (h + 1) * HG_DK) for h in range(HG_HEADS)]
        ps = []
        for sl in sls:
            groups = [None] * (CHUNK // SUBLANES)
            for li, m in enumerate(LEVELS):
                r = _dot_nt(ops[li][:, sl], ops[li][:, sl])
                for gi in range(len(groups)):
                    if m >= SUBLANES and not (gi * SUBLANES) & m:
                        continue
                    gs = slice(gi * SUBLANES, (gi + 1) * SUBLANES)
                    term = r[gs] * mk_ref[li, gs, :]
                    groups[gi] = term if groups[gi] is None else groups[gi] + term
            ps.append(jnp.concatenate(groups, axis=0).astype(BF16))
        new_states = [states[h] * e_last[:, sl] + _dot_tn(vb[:, sl], kt[:, sl]) for h, sl in enumerate(sls)]
        os_ = [_dot_nt(qt[:, sl], states[h].astype(BF16)) + _dot(ps[h], vb[:, sl]) for h, sl in enumerate(sls)]
        for h, sl in enumerate(sls):
            o = os_[h] + jnp.sum(qk[:, sl], axis=-1, keepdims=True) * v[:, sl]
            c1 = jnp.where(odd, jnp.sum(pair[:, sl], axis=-1, keepdims=True), 0.0)
            o = o + c1 * v_prev[:, sl]
            og = _rms(o, go) * gate[:, sl]
            o_ref[0, rs, sl] = og.astype(BF16)
        return new_states

    def chunk_body(c, carry):
        states = heads(c, wide(c), [st_ref[0, h] for h in range(HG_HEADS)])
        for h in range(HG_HEADS):
            st_ref[0, h] = states[h]
        return carry

    lax.fori_loop(0, tc // CHUNK, chunk_body, 0)


def _gla(z, lb_logits, layer, g_o, s0t):
    b, t, zw = z.shape
    dh = zw // 4
    tc = _tile(t, 256)
    masks = jnp.asarray(_level_masks(), F32)
    zspec = lambda c: pl.BlockSpec((1, tc, dh), lambda i, j, c=c: (i, j, c))
    sspec = pl.BlockSpec((1,) + s0t.shape[1:], lambda i, j: (i, 0, 0, 0))
    return pl.pallas_call(
        functools.partial(_gla_kernel, layer=layer),
        grid=(b, t // tc),
        in_specs=[zspec(0), zspec(1), zspec(2), zspec(3),
                  pl.BlockSpec(lb_logits.shape, lambda i, j: (0, 0)),
                  pl.BlockSpec((1, HG_DV), lambda i, j: (0, 0)),
                  sspec,
                  pl.BlockSpec(masks.shape, lambda i, j: (0, 0, 0))],
        out_specs=[pl.BlockSpec((1, tc, dh), lambda i, j: (i, j, 0)), sspec],
        out_shape=[jax.ShapeDtypeStruct((b, t, dh), BF16), jax.ShapeDtypeStruct(s0t.shape, F32)],
        compiler_params=_params("parallel", "arbitrary"),
        name="hgrn2_recurrence",
    )(z, z, z, z, lb_logits, g_o.reshape(1, HG_DV), s0t, masks)


def _rope_tables(pos):
    half = ROPE_DIM // 2
    inv = ROPE_THETA ** (-jnp.arange(half, dtype=F32) / half)
    ang = pos.astype(F32)[:, None] * inv[None, :]
    cos, sin = jnp.cos(ang), jnp.sin(ang)
    c1 = jnp.concatenate([cos, cos], axis=1)
    s1 = jnp.concatenate([-sin, sin], axis=1)
    return c1, s1, jnp.tile(c1, (1, MLA_HEADS)), jnp.tile(s1, (1, MLA_HEADS))


def _swap_halves(w):
    k, c = w.shape
    w = w.reshape(k, c // ROPE_DIM, 2, ROPE_DIM // 2)
    return w[:, :, ::-1, :].reshape(k, c)


def _prep_even(w_in_a, w_uq, w_ukv, w_out_a, dc):
    d = w_in_a.shape[0]
    kpe = w_in_a[:, -ROPE_DIM:]
    w_in = jnp.concatenate([w_in_a, _swap_halves(kpe)], axis=1).astype(BF16)
    ql = w_uq.shape[0]
    wq = w_uq.reshape(ql, MLA_HEADS, NOPE_DIM + ROPE_DIM)
    wqn = wq[:, :, :NOPE_DIM].reshape(ql, MLA_HEADS * NOPE_DIM).astype(BF16)
    wqp = wq[:, :, NOPE_DIM:].reshape(ql, MLA_HEADS * ROPE_DIM)
    wqs = _swap_halves(wqp).astype(BF16)
    kl = w_ukv.shape[0]
    wkv = w_ukv.reshape(kl, MLA_HEADS, NOPE_DIM + V_DIM)
    wuk = jnp.transpose(wkv[:, :, :NOPE_DIM], (1, 2, 0)).astype(BF16)
    wuv = jnp.transpose(wkv[:, :, NOPE_DIM:], (1, 0, 2)).astype(BF16)
    wo = w_out_a.astype(BF16)
    return dict(w_in=w_in, wqn=wqn, wqp=wqp.astype(BF16), wqs=wqs, wuk=wuk, wuv=wuv, wo_a=wo[:dc], wo_b=wo[dc:])


def _run_group(x3, pos, conv_prev, ckv_prev, kpe_prev, hgrn_prev, p):
    b, t, d = x3.shape
    n = b * t
    depth = p["norm_ffn1"].shape[0]
    x = x3.reshape(n, d)
    c1, s1, c8, s8 = _rope_tables(pos)
    conv_new, ckv_new, kpe_new, hgrn_new = [], [], [], []
    for l in range(depth):
        x = _ffn(x, p["norm_ffn1"], p["wg1"], p["wu1"], p["wd1"], l)
        if l % 2 == 0:
            e = l // 2
            pe = p["even"][e]
            dc = p["w_conv"].shape[2]
            cprev = jnp.zeros((b, CONV_W - 1, dc), F32) if conv_prev is None else conv_prev[e]
            ya, cs, q, ckv, kpe, kv = _even_in(x.reshape(b, t, d), p["norm_mix"][l], pe["w_in"], cprev,
                                               p["w_conv"][e], p["g_q"][e], pe["wqn"], pe["wqp"], pe["wqs"],
                                               pe["wuk"], p["g_kv"][e], c1, s1, c8, s8)
            if ckv_prev is None:
                yb = _attention(q, kv, pe["wuv"])
            else:
                yb = _attention_cached(q, kv, ckv_prev, kpe_prev, e, pe["wuv"])
            x = _matmul_residual([ya.reshape(n, -1), yb.reshape(n, -1)], [pe["wo_a"], pe["wo_b"]], x)
            conv_new.append(cs)
            ckv_new.append(ckv)
            kpe_new.append(kpe)
        else:
            o = l // 2
            z = _norm_matmul(x, p["norm_mix"][l], p["w_in_c"], o).reshape(b, t, -1)
            if hgrn_prev is None:
                s0t = jnp.zeros((b, HG_HEADS, HG_DV, HG_DK), F32)
            else:
                s0t = jnp.swapaxes(hgrn_prev[o], -1, -2)
            og, st = _gla(z, p["lb_logits"], o, p["g_o"][o], s0t)
            x = _matmul_residual([og.reshape(n, -1)], [p["w_out_c"][o]], x)
            hgrn_new.append(jnp.swapaxes(st, -1, -2))
        g_final = p["norm_final"] if l == depth - 1 else None
        x = _ffn(x, p["norm_ffn2"], p["wg2"], p["wu2"], p["wd2"], l, g_final)
    return x.reshape(b, t, d), jnp.stack(conv_new), jnp.stack(ckv_new), jnp.stack(kpe_new), jnp.stack(hgrn_new)


def kernel(x_prompt, x_sample, cache_conv, cache_ckv, cache_kpe, state_hgrn, norm_ffn1, w_ffn1_gate, w_ffn1_up, w_ffn1_down, norm_mix, w_in_a, w_conv, g_q, w_uq, g_kv, w_ukv, w_out_a, w_in_c, lb_logits, g_o, w_out_c, norm_ffn2, w_ffn2_gate, w_ffn2_up, w_ffn2_down, norm_final):
    dc = w_conv.shape[2]
    p = dict(
        norm_ffn1=norm_ffn1, wg1=w_ffn1_gate.astype(BF16), wu1=w_ffn1_up.astype(BF16), wd1=w_ffn1_down.astype(BF16),
        norm_ffn2=norm_ffn2, wg2=w_ffn2_gate.astype(BF16), wu2=w_ffn2_up.astype(BF16), wd2=w_ffn2_down.astype(BF16),
        norm_mix=norm_mix, w_conv=w_conv, g_q=g_q, g_kv=g_kv,
        even=[_prep_even(w_in_a[e], w_uq[e], w_ukv[e], w_out_a[e], dc) for e in range(w_in_a.shape[0])],
        w_in_c=w_in_c.astype(BF16), lb_logits=lb_logits, g_o=g_o, w_out_c=w_out_c.astype(BF16),
        norm_final=norm_final,
    )
    pos_p = jnp.arange(x_prompt.shape[1], dtype=jnp.int32)
    pos_s = cache_ckv.shape[2] + jnp.arange(x_sample.shape[1], dtype=jnp.int32)
    y_p, conv_p, ckv_p, kpe_p, hgrn_p = _run_group(x_prompt, pos_p, None, None, None, None, p)
    y_s, conv_s, ckv_s, kpe_s, hgrn_s = _run_group(x_sample, pos_s, cache_conv, cache_ckv, cache_kpe, state_hgrn, p)
    return (y_p, y_s, conv_p, ckv_p, kpe_p, hgrn_p, conv_s, ckv_s, kpe_s, hgrn_s)
```

```python
import functools

import numpy as np
import jax
import jax.numpy as jnp
from jax import lax
from jax.experimental import pallas as pl
from jax.experimental.pallas import tpu as pltpu

F32 = jnp.float32
BF16 = jnp.bfloat16

EPS = 1e-6
CHUNK = 64
CONV_W = 3
MLA_HEADS = 8
NOPE_DIM = 128
ROPE_DIM = 64
V_DIM = 128
ROPE_THETA = 10000.0
MLA_SCALE = (NOPE_DIM + ROPE_DIM) ** -0.5
HG_HEADS = 16
HG_DK = 128
HG_DV = 128
LANES = 128
SUBLANES = 8
NEG_BIG = -1e30
LOG2E = 1.4426950408889634
LEVELS = (32, 16, 8, 4, 2)
VMEM_LIMIT_BYTES = 60 * 1024 * 1024
FFN_TF = 512


def _tile(n, pref):
    if n <= pref:
        return n
    for t in range(pref, 7, -8):
        if n % t == 0:
            return t
    return n


def _col_tile(n, pref):
    for t in range(min(pref, n) // LANES * LANES, 0, -LANES):
        if n % t == 0:
            return t
    return n


def _params(*sem, **kw):
    return pltpu.CompilerParams(dimension_semantics=sem, vmem_limit_bytes=VMEM_LIMIT_BYTES, **kw)


def _sigmoid(x):
    return 1.0 / (1.0 + jnp.exp(-x))


def _silu(x):
    h = 0.5 * x
    return h + h * jnp.tanh(h)


def _rms(x, g):
    return x * lax.rsqrt(jnp.mean(x * x, axis=-1, keepdims=True) + EPS) * g


def _dot(a, b):
    return jnp.dot(a, b, preferred_element_type=F32)


def _dot_nt(a, b):
    return lax.dot_general(a, b, (((1,), (1,)), ((), ())), preferred_element_type=F32)


def _dot_tn(a, b):
    return lax.dot_general(a, b, (((0,), (0,)), ((), ())), preferred_element_type=F32)


def _ffn_kernel(x_ref, g_ref, wg_ref, wu_ref, wd_ref, *rest, final_norm):
    if final_norm:
        gf_ref, o_ref, n_ref = rest
    else:
        o_ref, n_ref = rest
    j = pl.program_id(1)
    tm = x_ref.shape[0]
    rc = min(tm, 128)

    def rows(i):
        return pl.ds(pl.multiple_of(i * rc, rc), rc)

    @pl.when(j == 0)
    def _():
        def norm_rows(i, carry):
            x = x_ref[rows(i), :]
            n_ref[rows(i), :] = _rms(x, g_ref[...]).astype(BF16)
            o_ref[rows(i), :] = x
            return carry

        lax.fori_loop(0, tm // rc, norm_rows, 0)

    n = n_ref[...]
    hg = _dot(n, wg_ref[...])
    hu = _dot(n, wu_ref[...])
    a = _silu(hg) * (0.5 * hu)
    o_ref[...] += _dot(a.astype(BF16), wd_ref[...])

    if final_norm:
        @pl.when(j == pl.num_programs(1) - 1)
        def _():
            def norm_out_rows(i, carry):
                o_ref[rows(i), :] = _rms(o_ref[rows(i), :], gf_ref[...])
                return carry

            lax.fori_loop(0, tm // rc, norm_out_rows, 0)


def _ffn(x, g, wg, wu, wd, layer, g_final=None):
    n, d = x.shape
    f = wg.shape[2]
    tm = _tile(n, 1024)
    tf = _col_tile(f, FFN_TF)
    in_specs = [
        pl.BlockSpec((tm, d), lambda i, j: (i, 0)),
        pl.BlockSpec((None, 1, d), lambda i, j: (layer, 0, 0)),
        pl.BlockSpec((None, d, tf), lambda i, j: (layer, 0, j)),
        pl.BlockSpec((None, d, tf), lambda i, j: (layer, 0, j)),
        pl.BlockSpec((None, tf, d), lambda i, j: (layer, j, 0)),
    ]
    args = [x, g.reshape(g.shape[0], 1, d), wg, wu, wd]
    if g_final is not None:
        in_specs.append(pl.BlockSpec((1, d), lambda i, j: (0, 0)))
        args.append(g_final.reshape(1, d))
    return pl.pallas_call(
        functools.partial(_ffn_kernel, final_norm=g_final is not None),
        grid=(n // tm, f // tf),
        in_specs=in_specs,
        out_specs=pl.BlockSpec((tm, d), lambda i, j: (i, 0)),
        out_shape=jax.ShapeDtypeStruct((n, d), F32),
        scratch_shapes=[pltpu.VMEM((tm, d), BF16)],
        compiler_params=_params("parallel", "arbitrary"),
        name="ffn",
    )(*args)


def _nmm_kernel(x_ref, g_ref, w_ref, o_ref, n_ref):
    tm = x_ref.shape[0]
    rc = min(tm, 128)

    @pl.when(pl.program_id(1) == 0)
    def _():
        def norm_rows(i, carry):
            rows = pl.ds(pl.multiple_of(i * rc, rc), rc)
            n_ref[rows, :] = _rms(x_ref[rows, :], g_ref[...]).astype(BF16)
            return carry

        lax.fori_loop(0, tm // rc, norm_rows, 0)

    o_ref[...] = _dot(n_ref[...], w_ref[...])


def _norm_matmul(x, g, w, layer):
    n, d = x.shape
    c = w.shape[2]
    tm = _tile(n, 1024)
    tn = _col_tile(c, 2048)
    return pl.pallas_call(
        _nmm_kernel,
        grid=(n // tm, c // tn),
        in_specs=[
            pl.BlockSpec((tm, d), lambda i, j: (i, 0)),
            pl.BlockSpec((1, d), lambda i, j: (0, 0)),
            pl.BlockSpec((None, d, tn), lambda i, j: (layer, 0, j)),
        ],
        out_specs=pl.BlockSpec((tm, tn), lambda i, j: (i, j)),
        out_shape=jax.ShapeDtypeStruct((n, c), F32),
        scratch_shapes=[pltpu.VMEM((tm, d), BF16)],
        compiler_params=_params("parallel", "arbitrary"),
        name="norm_matmul",
    )(x, g.reshape(1, d), w)


def _mmres_kernel(*refs, n_in):
    x_ref, o_ref = refs[2 * n_in], refs[2 * n_in + 1]
    acc = x_ref[...]
    for a_ref, w_ref in zip(refs[:n_in], refs[n_in:2 * n_in]):
        acc = acc + _dot(a_ref[...], w_ref[...])
    o_ref[...] = acc


def _matmul_residual(acts, weights, x):
    n, d = x.shape
    tm = _tile(n, 512)
    in_specs = [pl.BlockSpec((tm, a.shape[1]), lambda i: (i, 0)) for a in acts]
    in_specs += [pl.BlockSpec(w.shape, lambda i: (0, 0)) for w in weights]
    in_specs.append(pl.BlockSpec((tm, d), lambda i: (i, 0)))
    return pl.pallas_call(
        functools.partial(_mmres_kernel, n_in=len(acts)),
        grid=(n // tm,),
        in_specs=in_specs,
        out_specs=pl.BlockSpec((tm, d), lambda i: (i, 0)),
        out_shape=jax.ShapeDtypeStruct((n, d), F32),
        compiler_params=_params("parallel"),
        name="matmul_residual",
    )(*acts, *weights, x)


def _even_in_kernel(x_ref, gn_ref, win_ref, cp_ref, wc_ref, gq_ref, wqn_ref, wqp_ref, wqs_ref, wuk_ref, gkv_ref,
                    c1_ref, s1_ref, c8_ref, s8_ref,
                    ya_ref, cs_ref, q_ref, ckv_ref, kpe_ref, kv_ref, vp_ref, *, dc, ql, kl):
    t = pl.program_id(1)
    tt = x_ref.shape[1]
    o3 = 3 * dc
    o4 = o3 + ql
    o5 = o4 + kl
    n = _rms(x_ref[0], gn_ref[...]).astype(BF16)

    def proj(c0, c1):
        return _dot(n, win_ref[:, c0:c1])

    cqn = _rms(proj(o3, o4), gq_ref[...]).astype(BF16)
    qn = _dot(cqn, wqn_ref[...])
    qpr = _dot(cqn, wqp_ref[...]) * c8_ref[...] + _dot(cqn, wqs_ref[...]) * s8_ref[...]
    v = proj(dc, 2 * dc) * proj(2 * dc, o3)
    gb = proj(0, dc)
    cn = _rms(proj(o4, o5), gkv_ref[...])
    kp = proj(o5, o5 + 2 * ROPE_DIM)
    kr = kp[:, 0:ROPE_DIM] * c1_ref[...] + kp[:, ROPE_DIM:2 * ROPE_DIM] * s1_ref[...]
    for h in range(MLA_HEADS):
        qlat = _dot(qn[:, h * NOPE_DIM:(h + 1) * NOPE_DIM].astype(BF16), wuk_ref[h])
        q_ref[0, h, :, 0:kl] = qlat.astype(BF16)
        q_ref[0, h, :, kl:kl + ROPE_DIM] = qpr[:, h * ROPE_DIM:(h + 1) * ROPE_DIM].astype(BF16)
    cp = cp_ref[0]
    first = t == 0
    pm2 = jnp.where(first, cp[0:1], vp_ref[SUBLANES - 2:SUBLANES - 1, :])
    pm1 = jnp.where(first, cp[1:2], vp_ref[SUBLANES - 1:SUBLANES, :])
    rows = lax.broadcasted_iota(jnp.int32, v.shape, 0)
    v1 = jnp.where(rows == 0, pm1, pltpu.roll(v, 1, 0))
    v2 = jnp.where(rows == 0, pm2, jnp.where(rows == 1, pm1, pltpu.roll(v, 2, 0)))
    w = wc_ref[...]
    conv = v2 * w[0:1] + v1 * w[1:2] + v * w[2:3]
    ya_ref[0] = (gb * conv).astype(BF16)
    vp_ref[...] = v[tt - SUBLANES:tt]
    ckv_ref[0] = cn
    kpe_ref[0] = kr
    kv_ref[0, :, 0:kl] = cn.astype(BF16)
    kv_ref[0, :, kl:kl + ROPE_DIM] = kr.astype(BF16)

    @pl.when(t == pl.num_programs(1) - 1)
    def _():
        cs_ref[0] = v[tt - (CONV_W - 1):tt]


def _even_in(x, g_norm, w_in, conv_prev, w_conv, g_q, wqn, wqp, wqs, wuk, g_kv, c1, s1, c8, s8):
    b, t, d = x.shape
    dc = w_conv.shape[1]
    ql = g_q.shape[0]
    kl = g_kv.shape[0]
    tt = _tile(t, 256)
    hr = MLA_HEADS * ROPE_DIM
    once = lambda a: pl.BlockSpec(a.shape, lambda i, j: (0,) * a.ndim, pipeline_mode=pl.Buffered(1))
    wts = [g_norm.reshape(1, d), w_in]
    wts2 = [w_conv, g_q.reshape(1, ql), wqn, wqp, wqs, wuk, g_kv.reshape(1, kl)]
    in_specs = [pl.BlockSpec((1, tt, d), lambda i, j: (i, j, 0))] + [once(a) for a in wts] + [
        pl.BlockSpec((1, CONV_W - 1, dc), lambda i, j: (i, 0, 0)),
    ] + [once(a) for a in wts2] + [
        pl.BlockSpec((tt, ROPE_DIM), lambda i, j: (j, 0)),
        pl.BlockSpec((tt, ROPE_DIM), lambda i, j: (j, 0)),
        pl.BlockSpec((tt, hr), lambda i, j: (j, 0)),
        pl.BlockSpec((tt, hr), lambda i, j: (j, 0)),
    ]
    out_shape = [
        jax.ShapeDtypeStruct((b, t, dc), BF16),
        jax.ShapeDtypeStruct((b, CONV_W - 1, dc), F32),
        jax.ShapeDtypeStruct((b, MLA_HEADS, t, kl + ROPE_DIM), BF16),
        jax.ShapeDtypeStruct((b, t, kl), F32),
        jax.ShapeDtypeStruct((b, t, ROPE_DIM), F32),
        jax.ShapeDtypeStruct((b, t, kl + ROPE_DIM), BF16),
    ]
    out_specs = [
        pl.BlockSpec((1, tt, dc), lambda i, j: (i, j, 0)),
        pl.BlockSpec((1, CONV_W - 1, dc), lambda i, j: (i, 0, 0)),
        pl.BlockSpec((1, MLA_HEADS, tt, kl + ROPE_DIM), lambda i, j: (i, 0, j, 0)),
        pl.BlockSpec((1, tt, kl), lambda i, j: (i, j, 0)),
        pl.BlockSpec((1, tt, ROPE_DIM), lambda i, j: (i, j, 0)),
        pl.BlockSpec((1, tt, kl + ROPE_DIM), lambda i, j: (i, j, 0)),
    ]
    return pl.pallas_call(
        functools.partial(_even_in_kernel, dc=dc, ql=ql, kl=kl),
        grid=(b, t // tt),
        in_specs=in_specs,
        out_specs=out_specs,
        out_shape=out_shape,
        scratch_shapes=[pltpu.VMEM((SUBLANES, dc), F32)],
        compiler_params=_params("parallel", "arbitrary"),
        name="even_in",
    )(x, *wts, conv_prev, *wts2, c1, s1, c8, s8)


def _online_softmax(score_fns, value_fns, mask_last):
    c = MLA_SCALE * LOG2E
    n = len(score_fns)
    s = score_fns[0]()
    m = l = acc = None
    for j in range(n):
        s_next = score_fns[j + 1]() if j + 1 < n else None
        if j == n - 1 and mask_last is not None:
            s = mask_last(s)
        m_new = jnp.max(s, axis=-1, keepdims=True)
        if m is not None:
            m_new = jnp.maximum(m, m_new)
        p = jnp.exp2((s - m_new) * c)
        p_sum = jnp.sum(p, axis=-1, keepdims=True)
        pv = _dot(p.astype(BF16), value_fns[j]())
        if m is None:
            l, acc = p_sum, pv
        else:
            alpha = jnp.exp2((m - m_new) * c)
            l = alpha * l + p_sum
            acc = alpha * acc + pv
        m, s = m_new, s_next
    return acc * (1.0 / l)


def _value_up(o, wuv_ref, o_ref, tq):
    o = o.astype(BF16)
    for h in range(MLA_HEADS):
        o_ref[0, :, h * V_DIM:(h + 1) * V_DIM] = _dot(o[h * tq:(h + 1) * tq], wuv_ref[h]).astype(BF16)


def _attn_kernel(q_ref, kv_ref, wuv_ref, o_ref, *, tq, tk, kl):
    i = pl.program_id(1)
    rows = MLA_HEADS * tq
    q = q_ref[0].reshape(rows, q_ref.shape[3])
    tok = i * tq + lax.rem(lax.broadcasted_iota(jnp.int32, (rows, 1), 0), tq)
    limit = (tok // CHUNK + 1) * CHUNK

    def run(n_full, tail):
        bounds = [(j * tk, (j + 1) * tk) for j in range(n_full)] + [(n_full * tk, n_full * tk + tail)]

        def mask_last(s):
            kpos = n_full * tk + lax.broadcasted_iota(jnp.int32, (1, tail), 1)
            return jnp.where(kpos < limit, s, NEG_BIG)

        score_fns = [functools.partial(lambda lo, hi: _dot_nt(q, kv_ref[0, lo:hi, :]), lo, hi) for lo, hi in bounds]
        value_fns = [functools.partial(lambda lo, hi: kv_ref[0, lo:hi, 0:kl], lo, hi) for lo, hi in bounds]
        _value_up(_online_softmax(score_fns, value_fns, mask_last), wuv_ref, o_ref, tq)

    start = i * tq
    n_full = start // tk
    in_second_half = lax.rem(start, tk) >= tk // 2
    for a in range(kv_ref.shape[1] // tk):
        pl.when((n_full == a) & jnp.logical_not(in_second_half))(functools.partial(run, a, tk // 2))
        pl.when((n_full == a) & in_second_half)(functools.partial(run, a, tk))


def _attn_cache_kernel(q_ref, kv_ref, cc_ref, ck_ref, wuv_ref, o_ref, *, tk, kl):
    tq = q_ref.shape[2]
    rows = MLA_HEADS * tq
    q = q_ref[0].reshape(rows, q_ref.shape[3])
    q_lat, q_pe = q[:, 0:kl], q[:, kl:]
    n_cache = cc_ref.shape[1] // tk

    def cache_scores(j):
        ks = slice(j * tk, (j + 1) * tk)
        return _dot_nt(q_lat, cc_ref[0, ks, :].astype(BF16)) + _dot_nt(q_pe, ck_ref[0, ks, :].astype(BF16))

    score_fns = [functools.partial(cache_scores, j) for j in range(n_cache)]
    value_fns = [functools.partial(lambda j: cc_ref[0, j * tk:(j + 1) * tk, :].astype(BF16), j) for j in range(n_cache)]
    score_fns.append(lambda: _dot_nt(q, kv_ref[0]))
    value_fns.append(lambda: kv_ref[0, :, 0:kl])
    _value_up(_online_softmax(score_fns, value_fns, None), wuv_ref, o_ref, tq)


def _attention_cached(q, kv_new, cache_ckv, cache_kpe, layer, wuv):
    b, h, t, dq = q.shape
    past = cache_ckv.shape[2]
    kl = wuv.shape[1]
    tk = _tile(past, 512)
    assert past % tk == 0
    return pl.pallas_call(
        functools.partial(_attn_cache_kernel, tk=tk, kl=kl),
        grid=(b,),
        in_specs=[
            pl.BlockSpec((1, h, t, dq), lambda i: (i, 0, 0, 0)),
            pl.BlockSpec((1, t, dq), lambda i: (i, 0, 0)),
            pl.BlockSpec((None, 1, past, kl), lambda i: (layer, i, 0, 0)),
            pl.BlockSpec((None, 1, past, dq - kl), lambda i: (layer, i, 0, 0)),
            pl.BlockSpec(wuv.shape, lambda i: (0, 0, 0)),
        ],
        out_specs=pl.BlockSpec((1, t, h * V_DIM), lambda i: (i, 0, 0)),
        out_shape=jax.ShapeDtypeStruct((b, t, h * V_DIM), BF16),
        compiler_params=_params("parallel"),
        name="mla_attention_cached",
    )(q, kv_new, cache_ckv, cache_kpe, wuv)


def _attention(q, kv, wuv):
    b, h, t, dq = q.shape
    tkv = kv.shape[1]
    kl = wuv.shape[1]
    tq = _tile(t, 128)
    tk = _tile(tkv, 512)
    assert (tk // 2) % tq == 0 and tkv % tk == 0 and tq % CHUNK == 0
    return pl.pallas_call(
        functools.partial(_attn_kernel, tq=tq, tk=tk, kl=kl),
        grid=(b, t // tq),
        in_specs=[
            pl.BlockSpec((1, h, tq, dq), lambda i, j: (i, 0, j, 0)),
            pl.BlockSpec((1, tkv, dq), lambda i, j: (i, 0, 0)),
            pl.BlockSpec(wuv.shape, lambda i, j: (0, 0, 0)),
        ],
        out_specs=pl.BlockSpec((1, tq, h * V_DIM), lambda i, j: (i, j, 0)),
        out_shape=jax.ShapeDtypeStruct((b, t, h * V_DIM), BF16),
        compiler_params=_params("parallel", "arbitrary"),
        name="mla_attention",
    )(q, kv, wuv)


def _level_masks():
    t = np.arange(CHUNK)[:, None]
    s = np.arange(CHUNK)[None, :]
    masks = [(t // (2 * m) == s // (2 * m)) & ((t & m) != 0) & ((s & m) == 0) for m in LEVELS]
    return np.stack(masks).astype(np.float32)


def _boundary_rows(a, m):
    w = a.shape[1]
    if m >= SUBLANES:
        parts = [jnp.broadcast_to(a[b0 + m - 1:b0 + m], (2 * m, w)) for b0 in range(0, CHUNK, 2 * m)]
        return parts[0] if len(parts) == 1 else jnp.concatenate(parts, axis=0)
    a3 = a.reshape(CHUNK // SUBLANES, SUBLANES, w)
    sub = lax.broadcasted_iota(jnp.int32, a3.shape, 1)
    out = None
    for b0 in range(SUBLANES - 2 * m, -1, -2 * m):
        piece = jnp.broadcast_to(a3[:, b0 + m - 1:b0 + m, :], a3.shape)
        out = piece if out is None else jnp.where(sub < b0 + 2 * m, piece, out)
    return out.reshape(CHUNK, w)


def _prev_row_in_group(x):
    w = x.shape[1]
    x3 = x.reshape(CHUNK // SUBLANES, SUBLANES, w)
    return pltpu.roll(x3, 1, 1).reshape(CHUNK, w)


def _level_exponent(a, m):
    if m < SUBLANES:
        return -jnp.abs(a - _boundary_rows(a, m))
    parts = []
    for b0 in range(0, CHUNK, 2 * m):
        r = a[b0 + m - 1:b0 + m]
        parts += [r - a[b0:b0 + m], a[b0 + m:b0 + 2 * m] - r]
    return jnp.concatenate(parts, axis=0)


def _pick_halves(first, second, m, is_second):
    if m < SUBLANES:
        return jnp.where(is_second, second, first)
    parts = []
    for b0 in range(0, CHUNK, 2 * m):
        parts += [first[b0:b0 + m], second[b0 + m:b0 + 2 * m]]
    return jnp.concatenate(parts, axis=0)


def _gla_kernel(zq_ref, zf_ref, zi_ref, zg_ref, lbl_ref, go_ref, s0_ref, mk_ref, o_ref, st_ref, *, layer):
    t = pl.program_id(1)
    tc = zq_ref.shape[1]
    n_lv = len(LEVELS)

    @pl.when(t == 0)
    def _():
        st_ref[...] = s0_ref[...]

    ll = lbl_ref[...]
    e = jnp.exp(ll - jnp.max(ll, axis=0, keepdims=True))
    sm = e / jnp.sum(e, axis=0, keepdims=True)
    lb = jnp.zeros((1, ll.shape[1]), F32)
    for r in range(1, layer + 1):
        lb = lb + sm[r:r + 1]
    one_m_lb = 1.0 - lb
    go = go_ref[...]
    row = lax.broadcasted_iota(jnp.int32, (CHUNK, 1), 0)
    second = [(row & m) != 0 for m in LEVELS]
    odd = (row & 1) != 0

    def wide(c):
        rs = pl.ds(pl.multiple_of(c * CHUNK, CHUNK), CHUNK)
        zq = zq_ref[0, rs, :]
        zf = zf_ref[0, rs, :]
        v = zi_ref[0, rs, :]
        zg = zg_ref[0, rs, :]
        sig = _sigmoid(zf)
        g = jnp.log(lb + one_m_lb * sig) * LOG2E
        k = one_m_lb * (1.0 - sig)
        q = _silu(zq)
        gate = _silu(zg)
        a = g
        for sh in (1, 2, 4):
            rolled = pltpu.roll(a, sh, 0)
            top = jnp.where(row[:SUBLANES] >= sh, rolled[:SUBLANES], 0.0)
            a = a + jnp.concatenate([top, rolled[SUBLANES:]], axis=0)
        for sh in (8, 16, 32):
            a = a + jnp.concatenate([jnp.zeros((sh, a.shape[1]), F32), a[:CHUNK - sh]], axis=0)
        e_q = jnp.exp2(a)
        e_k = jnp.exp2(a[CHUNK - 1:CHUNK] - a)
        e_last = e_q[CHUNK - 1:CHUNK]
        qt = (q * e_q).astype(BF16)
        kt = (k * e_k).astype(BF16)
        qk = q * k
        vb = v.astype(BF16)
        pair = q * jnp.exp2(g) * _prev_row_in_group(k)
        v_prev = _prev_row_in_group(v)
        ops = []
        for li, m in enumerate(LEVELS):
            ex = jnp.exp2(_level_exponent(a, m))
            ops.append((_pick_halves(k, q, m, second[li]) * ex).astype(BF16))
        return ops, qt, kt, vb, qk, v, gate, e_last, pair, v_prev

    def heads(c, w, states):
        ops, qt, kt, vb, qk, v, gate, e_last, pair, v_prev = w
        rs = pl.ds(pl.multiple_of(c * CHUNK, CHUNK), CHUNK)
        sls = [slice(h * HG_DK, (h + 1) * HG_DK) for h in range(HG_HEADS)]
        ps = []
        for sl in sls:
            groups = [None] * (CHUNK // SUBLANES)
            for li, m in enumerate(LEVELS):
                r = _dot_nt(ops[li][:, sl], ops[li][:, sl])
                for gi in range(len(groups)):
                    if m >= SUBLANES and not (gi * SUBLANES) & m:
                        continue
                    gs = slice(gi * SUBLANES, (gi + 1) * SUBLANES)
                    term = r[gs] * mk_ref[li, gs, :]
                    groups[gi] = term if groups[gi] is None else groups[gi] + term
            ps.append(jnp.concatenate(groups, axis=0).astype(BF16))
        new_states = [states[h] * e_last[:, sl] + _dot_tn(vb[:, sl], kt[:, sl]) for h, sl in enumerate(sls)]
        os_ = [_dot_nt(qt[:, sl], states[h].astype(BF16)) + _dot(ps[h], vb[:, sl]) for h, sl in enumerate(sls)]
        for h, sl in enumerate(sls):
            o = os_[h] + jnp.sum(qk[:, sl], axis=-1, keepdims=True) * v[:, sl]
            c1 = jnp.where(odd, jnp.sum(pair[:, sl], axis=-1, keepdims=True), 0.0)
            o = o + c1 * v_prev[:, sl]
            og = _rms(o, go) * gate[:, sl]
            o_ref[0, rs, sl] = og.astype(BF16)
        return new_states

    def chunk_body(c, carry):
        states = heads(c, wide(c), [st_ref[0, h] for h in range(HG_HEADS)])
        for h in range(HG_HEADS):
            st_ref[0, h] = states[h]
        return carry

    lax.fori_loop(0, tc // CHUNK, chunk_body, 0)


def _gla(z, lb_logits, layer, g_o, s0t):
    b, t, zw = z.shape
    dh = zw // 4
    tc = _tile(t, 256)
    masks = jnp.asarray(_level_masks(), F32)
    zspec = lambda c: pl.BlockSpec((1, tc, dh), lambda i, j, c=c: (i, j, c))
    sspec = pl.BlockSpec((1,) + s0t.shape[1:], lambda i, j: (i, 0, 0, 0))
    return pl.pallas_call(
        functools.partial(_gla_kernel, layer=layer),
        grid=(b, t // tc),
        in_specs=[zspec(0), zspec(1), zspec(2), zspec(3),
                  pl.BlockSpec(lb_logits.shape, lambda i, j: (0, 0)),
                  pl.BlockSpec((1, HG_DV), lambda i, j: (0, 0)),
                  sspec,
                  pl.BlockSpec(masks.shape, lambda i, j: (0, 0, 0))],
        out_specs=[pl.BlockSpec((1, tc, dh), lambda i, j: (i, j, 0)), sspec],
        out_shape=[jax.ShapeDtypeStruct((b, t, dh), BF16), jax.ShapeDtypeStruct(s0t.shape, F32)],
        compiler_params=_params("parallel", "arbitrary"),
        name="hgrn2_recurrence",
    )(z, z, z, z, lb_logits, g_o.reshape(1, HG_DV), s0t, masks)


def _rope_tables(pos):
    half = ROPE_DIM // 2
    inv = ROPE_THETA ** (-jnp.arange(half, dtype=F32) / half)
    ang = pos.astype(F32)[:, None] * inv[None, :]
    cos, sin = jnp.cos(ang), jnp.sin(ang)
    c1 = jnp.concatenate([cos, cos], axis=1)
    s1 = jnp.concatenate([-sin, sin], axis=1)
    return c1, s1, jnp.tile(c1, (1, MLA_HEADS)), jnp.tile(s1, (1, MLA_HEADS))


def _swap_halves(w):
    k, c = w.shape
    w = w.reshape(k, c // ROPE_DIM, 2, ROPE_DIM // 2)
    return w[:, :, ::-1, :].reshape(k, c)


def _prep_even(w_in_a, w_uq, w_ukv, w_out_a, dc):
    d = w_in_a.shape[0]
    kpe = w_in_a[:, -ROPE_DIM:]
    w_in = jnp.concatenate([w_in_a, _swap_halves(kpe)], axis=1).astype(BF16)
    ql = w_uq.shape[0]
    wq = w_uq.reshape(ql, MLA_HEADS, NOPE_DIM + ROPE_DIM)
    wqn = wq[:, :, :NOPE_DIM].reshape(ql, MLA_HEADS * NOPE_DIM).astype(BF16)
    wqp = wq[:, :, NOPE_DIM:].reshape(ql, MLA_HEADS * ROPE_DIM)
    wqs = _swap_halves(wqp).astype(BF16)
    kl = w_ukv.shape[0]
    wkv = w_ukv.reshape(kl, MLA_HEADS, NOPE_DIM + V_DIM)
    wuk = jnp.transpose(wkv[:, :, :NOPE_DIM], (1, 2, 0)).astype(BF16)
    wuv = jnp.transpose(wkv[:, :, NOPE_DIM:], (1, 0, 2)).astype(BF16)
    wo = w_out_a.astype(BF16)
    return dict(w_in=w_in, wqn=wqn, wqp=wqp.astype(BF16), wqs=wqs, wuk=wuk, wuv=wuv, wo_a=wo[:dc], wo_b=wo[dc:])


def _run_group(x3, pos, conv_prev, ckv_prev, kpe_prev, hgrn_prev, p):
    b, t, d = x3.shape
    n = b * t
    depth = p["norm_ffn1"].shape[0]
    x = x3.reshape(n, d)
    c1, s1, c8, s8 = _rope_tables(pos)
    conv_new, ckv_new, kpe_new, hgrn_new = [], [], [], []
    for l in range(depth):
        x = _ffn(x, p["norm_ffn1"], p["wg1"], p["wu1"], p["wd1"], l)
        if l % 2 == 0:
            e = l // 2
            pe = p["even"][e]
            dc = p["w_conv"].shape[2]
            cprev = jnp.zeros((b, CONV_W - 1, dc), F32) if conv_prev is None else conv_prev[e]
            ya, cs, q, ckv, kpe, kv = _even_in(x.reshape(b, t, d), p["norm_mix"][l], pe["w_in"], cprev,
                                               p["w_conv"][e], p["g_q"][e], pe["wqn"], pe["wqp"], pe["wqs"],
                                               pe["wuk"], p["g_kv"][e], c1, s1, c8, s8)
            if ckv_prev is None:
                yb = _attention(q, kv, pe["wuv"])
            else:
                yb = _attention_cached(q, kv, ckv_prev, kpe_prev, e, pe["wuv"])
            x = _matmul_residual([ya.reshape(n, -1), yb.reshape(n, -1)], [pe["wo_a"], pe["wo_b"]], x)
            conv_new.append(cs)
            ckv_new.append(ckv)
            kpe_new.append(kpe)
        else:
            o = l // 2
            z = _norm_matmul(x, p["norm_mix"][l], p["w_in_c"], o).reshape(b, t, -1)
            if hgrn_prev is None:
                s0t = jnp.zeros((b, HG_HEADS, HG_DV, HG_DK), F32)
            else:
                s0t = jnp.swapaxes(hgrn_prev[o], -1, -2)
            og, st = _gla(z, p["lb_logits"], o, p["g_o"][o], s0t)
            x = _matmul_residual([og.reshape(n, -1)], [p["w_out_c"][o]], x)
            hgrn_new.append(jnp.swapaxes(st, -1, -2))
        g_final = p["norm_final"] if l == depth - 1 else None
        x = _ffn(x, p["norm_ffn2"], p["wg2"], p["wu2"], p["wd2"], l, g_final)
    return x.reshape(b, t, d), jnp.stack(conv_new), jnp.stack(ckv_new), jnp.stack(kpe_new), jnp.stack(hgrn_new)


def kernel(x_prompt, x_sample, cache_conv, cache_ckv, cache_kpe, state_hgrn, norm_ffn1, w_ffn1_gate, w_ffn1_up, w_ffn1_down, norm_mix, w_in_a, w_conv, g_q, w_uq, g_kv, w_ukv, w_out_a, w_in_c, lb_logits, g_o, w_out_c, norm_ffn2, w_ffn2_gate, w_ffn2_up, w_ffn2_down, norm_final):
    dc = w_conv.shape[2]
    p = dict(
        norm_ffn1=norm_ffn1, wg1=w_ffn1_gate.astype(BF16), wu1=w_ffn1_up.astype(BF16), wd1=w_ffn1_down.astype(BF16),
        norm_ffn2=norm_ffn2, wg2=w_ffn2_gate.astype(BF16), wu2=w_ffn2_up.astype(BF16), wd2=w_ffn2_down.astype(BF16),
        norm_mix=norm_mix, w_conv=w_conv, g_q=g_q, g_kv=g_kv,
        even=[_prep_even(w_in_a[e], w_uq[e], w_ukv[e], w_out_a[e], dc) for e in range(w_in_a.shape[0])],
        w_in_c=w_in_c.astype(BF16), lb_logits=lb_logits, g_o=g_o, w_out_c=w_out_c.astype(BF16),
        norm_final=norm_final,
    )
    pos_p = jnp.arange(x_prompt.shape[1], dtype=jnp.int32)
    pos_s = cache_ckv.shape[2] + jnp.arange(x_sample.shape[1], dtype=jnp.int32)
    y_p, conv_p, ckv_p, kpe_p, hgrn_p = _run_group(x_prompt, pos_p, None, None, None, None, p)
    y_s, conv_s, ckv_s, kpe_s, hgrn_s = _run_group(x_sample, pos_s, cache_conv, cache_ckv, cache_kpe, state_hgrn, p)
    return (y_p, y_s, conv_p, ckv_p, kpe_p, hgrn_p, conv_s, ckv_s, kpe_s, hgrn_s)
```

```python
import functools

import numpy as np
import jax
import jax.numpy as jnp
from jax import lax
from jax.experimental import pallas as pl
from jax.experimental.pallas import tpu as pltpu

F32 = jnp.float32
BF16 = jnp.bfloat16

EPS = 1e-6
CHUNK = 64
CONV_W = 3
MLA_HEADS = 8
NOPE_DIM = 128
ROPE_DIM = 64
V_DIM = 128
ROPE_THETA = 10000.0
MLA_SCALE = (NOPE_DIM + ROPE_DIM) ** -0.5
HG_HEADS = 16
HG_DK = 128
HG_DV = 128
LANES = 128
SUBLANES = 8
NEG_BIG = -1e30
LOG2E = 1.4426950408889634
LEVELS = (32, 16, 8, 4, 2)
VMEM_LIMIT_BYTES = 60 * 1024 * 1024
FFN_TF = 512


def _tile(n, pref):
    if n <= pref:
        return n
    for t in range(pref, 7, -8):
        if n % t == 0:
            return t
    return n


def _col_tile(n, pref):
    for t in range(min(pref, n) // LANES * LANES, 0, -LANES):
        if n % t == 0:
            return t
    return n


def _params(*sem, **kw):
    return pltpu.CompilerParams(dimension_semantics=sem, vmem_limit_bytes=VMEM_LIMIT_BYTES, **kw)


def _sigmoid(x):
    return 1.0 / (1.0 + jnp.exp(-x))


def _silu(x):
    h = 0.5 * x
    return h + h * jnp.tanh(h)


def _rms(x, g):
    return x * lax.rsqrt(jnp.mean(x * x, axis=-1, keepdims=True) + EPS) * g


def _dot(a, b):
    return jnp.dot(a, b, preferred_element_type=F32)


def _dot_nt(a, b):
    return lax.dot_general(a, b, (((1,), (1,)), ((), ())), preferred_element_type=F32)


def _dot_tn(a, b):
    return lax.dot_general(a, b, (((0,), (0,)), ((), ())), preferred_element_type=F32)


def _ffn_kernel(x_ref, g_ref, wg_ref, wu_ref, wd_ref, *rest, final_norm):
    if final_norm:
        gf_ref, o_ref, n_ref = rest
    else:
        o_ref, n_ref = rest
    j = pl.program_id(1)
    tm = x_ref.shape[0]
    rc = min(tm, 128)

    def rows(i):
        return pl.ds(pl.multiple_of(i * rc, rc), rc)

    @pl.when(j == 0)
    def _():
        def norm_rows(i, carry):
            x = x_ref[rows(i), :]
            n_ref[rows(i), :] = _rms(x, g_ref[...]).astype(BF16)
            o_ref[rows(i), :] = x
            return carry

        lax.fori_loop(0, tm // rc, norm_rows, 0)

    n = n_ref[...]
    hg = _dot(n, wg_ref[...])
    hu = _dot(n, wu_ref[...])
    a = _silu(hg) * (0.5 * hu)
    o_ref[...] += _dot(a.astype(BF16), wd_ref[...])

    if final_norm:
        @pl.when(j == pl.num_programs(1) - 1)
        def _():
            def norm_out_rows(i, carry):
                o_ref[rows(i), :] = _rms(o_ref[rows(i), :], gf_ref[...])
                return carry

            lax.fori_loop(0, tm // rc, norm_out_rows, 0)


def _ffn(x, g, wg, wu, wd, layer, g_final=None):
    n, d = x.shape
    f = wg.shape[2]
    tm = _tile(n, 1024)
    tf = _col_tile(f, FFN_TF)
    in_specs = [
        pl.BlockSpec((tm, d), lambda i, j: (i, 0)),
        pl.BlockSpec((None, 1, d), lambda i, j: (layer, 0, 0)),
        pl.BlockSpec((None, d, tf), lambda i, j: (layer, 0, j)),
        pl.BlockSpec((None, d, tf), lambda i, j: (layer, 0, j)),
        pl.BlockSpec((None, tf, d), lambda i, j: (layer, j, 0)),
    ]
    args = [x, g.reshape(g.shape[0], 1, d), wg, wu, wd]
    if g_final is not None:
        in_specs.append(pl.BlockSpec((1, d), lambda i, j: (0, 0)))
        args.append(g_final.reshape(1, d))
    return pl.pallas_call(
        functools.partial(_ffn_kernel, final_norm=g_final is not None),
        grid=(n // tm, f // tf),
        in_specs=in_specs,
        out_specs=pl.BlockSpec((tm, d), lambda i, j: (i, 0)),
        out_shape=jax.ShapeDtypeStruct((n, d), F32),
        scratch_shapes=[pltpu.VMEM((tm, d), BF16)],
        compiler_params=_params("parallel", "arbitrary"),
        name="ffn",
    )(*args)


def _nmm_kernel(x_ref, g_ref, w_ref, o_ref, n_ref):
    tm = x_ref.shape[0]
    rc = min(tm, 128)

    @pl.when(pl.program_id(1) == 0)
    def _():
        def norm_rows(i, carry):
            rows = pl.ds(pl.multiple_of(i * rc, rc), rc)
            n_ref[rows, :] = _rms(x_ref[rows, :], g_ref[...]).astype(BF16)
            return carry

        lax.fori_loop(0, tm // rc, norm_rows, 0)

    o_ref[...] = _dot(n_ref[...], w_ref[...])


def _norm_matmul(x, g, w, layer):
    n, d = x.shape
    c = w.shape[2]
    tm = _tile(n, 1024)
    tn = _col_tile(c, 2048)
    return pl.pallas_call(
        _nmm_kernel,
        grid=(n // tm, c // tn),
        in_specs=[
            pl.BlockSpec((tm, d), lambda i, j: (i, 0)),
            pl.BlockSpec((1, d), lambda i, j: (0, 0)),
            pl.BlockSpec((None, d, tn), lambda i, j: (layer, 0, j)),
        ],
        out_specs=pl.BlockSpec((tm, tn), lambda i, j: (i, j)),
        out_shape=jax.ShapeDtypeStruct((n, c), F32),
        scratch_shapes=[pltpu.VMEM((tm, d), BF16)],
        compiler_params=_params("parallel", "arbitrary"),
        name="norm_matmul",
    )(x, g.reshape(1, d), w)


def _mmres_kernel(*refs, n_in):
    x_ref, o_ref = refs[2 * n_in], refs[2 * n_in + 1]
    acc = x_ref[...]
    for a_ref, w_ref in zip(refs[:n_in], refs[n_in:2 * n_in]):
        acc = acc + _dot(a_ref[...], w_ref[...])
    o_ref[...] = acc


def _matmul_residual(acts, weights, x):
    n, d = x.shape
    tm = _tile(n, 512)
    in_specs = [pl.BlockSpec((tm, a.shape[1]), lambda i: (i, 0)) for a in acts]
    in_specs += [pl.BlockSpec(w.shape, lambda i: (0, 0)) for w in weights]
    in_specs.append(pl.BlockSpec((tm, d), lambda i: (i, 0)))
    return pl.pallas_call(
        functools.partial(_mmres_kernel, n_in=len(acts)),
        grid=(n // tm,),
        in_specs=in_specs,
        out_specs=pl.BlockSpec((tm, d), lambda i: (i, 0)),
        out_shape=jax.ShapeDtypeStruct((n, d), F32),
        compiler_params=_params("parallel"),
        name="matmul_residual",
    )(*acts, *weights, x)


def _even_in_kernel(x_ref, gn_ref, win_ref, cp_ref, wc_ref, gq_ref, wqn_ref, wqp_ref, wqs_ref, wuk_ref, gkv_ref,
                    c1_ref, s1_ref, c8_ref, s8_ref,
                    ya_ref, cs_ref, q_ref, ckv_ref, kpe_ref, kv_ref, vp_ref, *, dc, ql, kl):
    t = pl.program_id(1)
    tt = x_ref.shape[1]
    o3 = 3 * dc
    o4 = o3 + ql
    o5 = o4 + kl
    n = _rms(x_ref[0], gn_ref[...]).astype(BF16)

    def proj(c0, c1):
        return _dot(n, win_ref[:, c0:c1])

    cqn = _rms(proj(o3, o4), gq_ref[...]).astype(BF16)
    qn = _dot(cqn, wqn_ref[...])
    qpr = _dot(cqn, wqp_ref[...]) * c8_ref[...] + _dot(cqn, wqs_ref[...]) * s8_ref[...]
    v = proj(dc, 2 * dc) * proj(2 * dc, o3)
    gb = proj(0, dc)
    cn = _rms(proj(o4, o5), gkv_ref[...])
    kp = proj(o5, o5 + 2 * ROPE_DIM)
    kr = kp[:, 0:ROPE_DIM] * c1_ref[...] + kp[:, ROPE_DIM:2 * ROPE_DIM] * s1_ref[...]
    for h in range(MLA_HEADS):
        qlat = _dot(qn[:, h * NOPE_DIM:(h + 1) * NOPE_DIM].astype(BF16), wuk_ref[h])
        q_ref[0, h, :, 0:kl] = qlat.astype(BF16)
        q_ref[0, h, :, kl:kl + ROPE_DIM] = qpr[:, h * ROPE_DIM:(h + 1) * ROPE_DIM].astype(BF16)
    cp = cp_ref[0]
    first = t == 0
    pm2 = jnp.where(first, cp[0:1], vp_ref[SUBLANES - 2:SUBLANES - 1, :])
    pm1 = jnp.where(first, cp[1:2], vp_ref[SUBLANES - 1:SUBLANES, :])
    rows = lax.broadcasted_iota(jnp.int32, v.shape, 0)
    v1 = jnp.where(rows == 0, pm1, pltpu.roll(v, 1, 0))
    v2 = jnp.where(rows == 0, pm2, jnp.where(rows == 1, pm1, pltpu.roll(v, 2, 0)))
    w = wc_ref[...]
    conv = v2 * w[0:1] + v1 * w[1:2] + v * w[2:3]
    ya_ref[0] = (gb * conv).astype(BF16)
    vp_ref[...] = v[tt - SUBLANES:tt]
    ckv_ref[0] = cn
    kpe_ref[0] = kr
    kv_ref[0, :, 0:kl] = cn.astype(BF16)
    kv_ref[0, :, kl:kl + ROPE_DIM] = kr.astype(BF16)

    @pl.when(t == pl.num_programs(1) - 1)
    def _():
        cs_ref[0] = v[tt - (CONV_W - 1):tt]


def _even_in(x, g_norm, w_in, conv_prev, w_conv, g_q, wqn, wqp, wqs, wuk, g_kv, c1, s1, c8, s8):
    b, t, d = x.shape
    dc = w_conv.shape[1]
    ql = g_q.shape[0]
    kl = g_kv.shape[0]
    tt = _tile(t, 256)
    hr = MLA_HEADS * ROPE_DIM
    once = lambda a: pl.BlockSpec(a.shape, lambda i, j: (0,) * a.ndim, pipeline_mode=pl.Buffered(1))
    wts = [g_norm.reshape(1, d), w_in]
    wts2 = [w_conv, g_q.reshape(1, ql), wqn, wqp, wqs, wuk, g_kv.reshape(1, kl)]
    in_specs = [pl.BlockSpec((1, tt, d), lambda i, j: (i, j, 0))] + [once(a) for a in wts] + [
        pl.BlockSpec((1, CONV_W - 1, dc), lambda i, j: (i, 0, 0)),
    ] + [once(a) for a in wts2] + [
        pl.BlockSpec((tt, ROPE_DIM), lambda i, j: (j, 0)),
        pl.BlockSpec((tt, ROPE_DIM), lambda i, j: (j, 0)),
        pl.BlockSpec((tt, hr), lambda i, j: (j, 0)),
        pl.BlockSpec((tt, hr), lambda i, j: (j, 0)),
    ]
    out_shape = [
        jax.ShapeDtypeStruct((b, t, dc), BF16),
        jax.ShapeDtypeStruct((b, CONV_W - 1, dc), F32),
        jax.ShapeDtypeStruct((b, MLA_HEADS, t, kl + ROPE_DIM), BF16),
        jax.ShapeDtypeStruct((b, t, kl), F32),
        jax.ShapeDtypeStruct((b, t, ROPE_DIM), F32),
        jax.ShapeDtypeStruct((b, t, kl + ROPE_DIM), BF16),
    ]
    out_specs = [
        pl.BlockSpec((1, tt, dc), lambda i, j: (i, j, 0)),
        pl.BlockSpec((1, CONV_W - 1, dc), lambda i, j: (i, 0, 0)),
        pl.BlockSpec((1, MLA_HEADS, tt, kl + ROPE_DIM), lambda i, j: (i, 0, j, 0)),
        pl.BlockSpec((1, tt, kl), lambda i, j: (i, j, 0)),
        pl.BlockSpec((1, tt, ROPE_DIM), lambda i, j: (i, j, 0)),
        pl.BlockSpec((1, tt, kl + ROPE_DIM), lambda i, j: (i, j, 0)),
    ]
    return pl.pallas_call(
        functools.partial(_even_in_kernel, dc=dc, ql=ql, kl=kl),
        grid=(b, t // tt),
        in_specs=in_specs,
        out_specs=out_specs,
        out_shape=out_shape,
        scratch_shapes=[pltpu.VMEM((SUBLANES, dc), F32)],
        compiler_params=_params("parallel", "arbitrary"),
        name="even_in",
    )(x, *wts, conv_prev, *wts2, c1, s1, c8, s8)


def _online_softmax(score_fns, value_fns, mask_last):
    c = MLA_SCALE * LOG2E
    n = len(score_fns)
    s = score_fns[0]()
    m = l = acc = None
    for j in range(n):
        s_next = score_fns[j + 1]() if j + 1 < n else None
        if j == n - 1 and mask_last is not None:
            s = mask_last(s)
        m_new = jnp.max(s, axis=-1, keepdims=True)
        if m is not None:
            m_new = jnp.maximum(m, m_new)
        p = jnp.exp2((s - m_new) * c)
        p_sum = jnp.sum(p, axis=-1, keepdims=True)
        pv = _dot(p.astype(BF16), value_fns[j]())
        if m is None:
            l, acc = p_sum, pv
        else:
            alpha = jnp.exp2((m - m_new) * c)
            l = alpha * l + p_sum
            acc = alpha * acc + pv
        m, s = m_new, s_next
    return acc * (1.0 / l)


def _value_up(o, wuv_ref, o_ref, tq):
    o = o.astype(BF16)
    for h in range(MLA_HEADS):
        o_ref[0, :, h * V_DIM:(h + 1) * V_DIM] = _dot(o[h * tq:(h + 1) * tq], wuv_ref[h]).astype(BF16)


def _attn_kernel(q_ref, kv_ref, wuv_ref, o_ref, *, tq, tk, kl):
    i = pl.program_id(1)
    rows = MLA_HEADS * tq
    q = q_ref[0].reshape(rows, q_ref.shape[3])
    tok = i * tq + lax.rem(lax.broadcasted_iota(jnp.int32, (rows, 1), 0), tq)
    limit = (tok // CHUNK + 1) * CHUNK

    def run(n_full, tail):
        bounds = [(j * tk, (j + 1) * tk) for j in range(n_full)] + [(n_full * tk, n_full * tk + tail)]

        def mask_last(s):
            kpos = n_full * tk + lax.broadcasted_iota(jnp.int32, (1, tail), 1)
            return jnp.where(kpos < limit, s, NEG_BIG)

        score_fns = [functools.partial(lambda lo, hi: _dot_nt(q, kv_ref[0, lo:hi, :]), lo, hi) for lo, hi in bounds]
        value_fns = [functools.partial(lambda lo, hi: kv_ref[0, lo:hi, 0:kl], lo, hi) for lo, hi in bounds]
        _value_up(_online_softmax(score_fns, value_fns, mask_last), wuv_ref, o_ref, tq)

    start = i * tq
    n_full = start // tk
    in_second_half = lax.rem(start, tk) >= tk // 2
    for a in range(kv_ref.shape[1] // tk):
        pl.when((n_full == a) & jnp.logical_not(in_second_half))(functools.partial(run, a, tk // 2))
        pl.when((n_full == a) & in_second_half)(functools.partial(run, a, tk))


def _attn_cache_kernel(q_ref, kv_ref, cc_ref, ck_ref, wuv_ref, o_ref, *, tk, kl):
    tq = q_ref.shape[2]
    rows = MLA_HEADS * tq
    q = q_ref[0].reshape(rows, q_ref.shape[3])
    q_lat, q_pe = q[:, 0:kl], q[:, kl:]
    n_cache = cc_ref.shape[1] // tk

    def cache_scores(j):
        ks = slice(j * tk, (j + 1) * tk)
        return _dot_nt(q_lat, cc_ref[0, ks, :].astype(BF16)) + _dot_nt(q_pe, ck_ref[0, ks, :].astype(BF16))

    score_fns = [functools.partial(cache_scores, j) for j in range(n_cache)]
    value_fns = [functools.partial(lambda j: cc_ref[0, j * tk:(j + 1) * tk, :].astype(BF16), j) for j in range(n_cache)]
    score_fns.append(lambda: _dot_nt(q, kv_ref[0]))
    value_fns.append(lambda: kv_ref[0, :, 0:kl])
    _value_up(_online_softmax(score_fns, value_fns, None), wuv_ref, o_ref, tq)


def _attention_cached(q, kv_new, cache_ckv, cache_kpe, layer, wuv):
    b, h, t, dq = q.shape
    past = cache_ckv.shape[2]
    kl = wuv.shape[1]
    tk = _tile(past, 512)
    assert past % tk == 0
    return pl.pallas_call(
        functools.partial(_attn_cache_kernel, tk=tk, kl=kl),
        grid=(b,),
        in_specs=[
            pl.BlockSpec((1, h, t, dq), lambda i: (i, 0, 0, 0)),
            pl.BlockSpec((1, t, dq), lambda i: (i, 0, 0)),
            pl.BlockSpec((None, 1, past, kl), lambda i: (layer, i, 0, 0)),
            pl.BlockSpec((None, 1, past, dq - kl), lambda i: (layer, i, 0, 0)),
            pl.BlockSpec(wuv.shape, lambda i: (0, 0, 0)),
        ],
        out_specs=pl.BlockSpec((1, t, h * V_DIM), lambda i: (i, 0, 0)),
        out_shape=jax.ShapeDtypeStruct((b, t, h * V_DIM), BF16),
        compiler_params=_params("parallel"),
        name="mla_attention_cached",
    )(q, kv_new, cache_ckv, cache_kpe, wuv)


def _attention(q, kv, wuv):
    b, h, t, dq = q.shape
    tkv = kv.shape[1]
    kl = wuv.shape[1]
    tq = _tile(t, 128)
    tk = _tile(tkv, 512)
    assert (tk // 2) % tq == 0 and tkv % tk == 0 and tq % CHUNK == 0
    return pl.pallas_call(
        functools.partial(_attn_kernel, tq=tq, tk=tk, kl=kl),
        grid=(b, t // tq),
        in_specs=[
            pl.BlockSpec((1, h, tq, dq), lambda i, j: (i, 0, j, 0)),
            pl.BlockSpec((1, tkv, dq), lambda i, j: (i, 0, 0)),
            pl.BlockSpec(wuv.shape, lambda i, j: (0, 0, 0)),
        ],
        out_specs=pl.BlockSpec((1, tq, h * V_DIM), lambda i, j: (i, j, 0)),
        out_shape=jax.ShapeDtypeStruct((b, t, h * V_DIM), BF16),
        compiler_params=_params("parallel", "arbitrary"),
        name="mla_attention",
    )(q, kv, wuv)


def _level_masks():
    t = np.arange(CHUNK)[:, None]
    s = np.arange(CHUNK)[None, :]
    masks = [(t // (2 * m) == s // (2 * m)) & ((t & m) != 0) & ((s & m) == 0) for m in LEVELS]
    return np.stack(masks).astype(np.float32)


def _boundary_rows(a, m):
    w = a.shape[1]
    if m >= SUBLANES:
        parts = [jnp.broadcast_to(a[b0 + m - 1:b0 + m], (2 * m, w)) for b0 in range(0, CHUNK, 2 * m)]
        return parts[0] if len(parts) == 1 else jnp.concatenate(parts, axis=0)
    a3 = a.reshape(CHUNK // SUBLANES, SUBLANES, w)
    sub = lax.broadcasted_iota(jnp.int32, a3.shape, 1)
    out = None
    for b0 in range(SUBLANES - 2 * m, -1, -2 * m):
        piece = jnp.broadcast_to(a3[:, b0 + m - 1:b0 + m, :], a3.shape)
        out = piece if out is None else jnp.where(sub < b0 + 2 * m, piece, out)
    return out.reshape(CHUNK, w)


def _prev_row_in_group(x):
    w = x.shape[1]
    x3 = x.reshape(CHUNK // SUBLANES, SUBLANES, w)
    return pltpu.roll(x3, 1, 1).reshape(CHUNK, w)


def _level_exponent(a, m):
    if m < SUBLANES:
        return -jnp.abs(a - _boundary_rows(a, m))
    parts = []
    for b0 in range(0, CHUNK, 2 * m):
        r = a[b0 + m - 1:b0 + m]
        parts += [r - a[b0:b0 + m], a[b0 + m:b0 + 2 * m] - r]
    return jnp.concatenate(parts, axis=0)


def _pick_halves(first, second, m, is_second):
    if m < SUBLANES:
        return jnp.where(is_second, second, first)
    parts = []
    for b0 in range(0, CHUNK, 2 * m):
        parts += [first[b0:b0 + m], second[b0 + m:b0 + 2 * m]]
    return jnp.concatenate(parts, axis=0)


def _gla_kernel(zq_ref, zf_ref, zi_ref, zg_ref, lbl_ref, go_ref, s0_ref, mk_ref, o_ref, st_ref, *, layer):
    t = pl.program_id(1)
    tc = zq_ref.shape[1]
    n_lv = len(LEVELS)

    @pl.when(t == 0)
    def _():
        for h in range(HG_HEADS):
            st_ref[0, h] = s0_ref[0, h].T

    ll = lbl_ref[...]
    e = jnp.exp(ll - jnp.max(ll, axis=0, keepdims=True))
    sm = e / jnp.sum(e, axis=0, keepdims=True)
    lb = jnp.zeros((1, ll.shape[1]), F32)
    for r in range(1, layer + 1):
        lb = lb + sm[r:r + 1]
    one_m_lb = 1.0 - lb
    go = go_ref[...]
    row = lax.broadcasted_iota(jnp.int32, (CHUNK, 1), 0)
    second = [(row & m) != 0 for m in LEVELS]
    odd = (row & 1) != 0

    def wide(c):
        rs = pl.ds(pl.multiple_of(c * CHUNK, CHUNK), CHUNK)
        zq = zq_ref[0, rs, :]
        zf = zf_ref[0, rs, :]
        v = zi_ref[0, rs, :]
        zg = zg_ref[0, rs, :]
        sig = _sigmoid(zf)
        g = jnp.log(lb + one_m_lb * sig) * LOG2E
        k = one_m_lb * (1.0 - sig)
        q = _silu(zq)
        gate = _silu(zg)
        a = g
        for sh in (1, 2, 4):
            rolled = pltpu.roll(a, sh, 0)
            top = jnp.where(row[:SUBLANES] >= sh, rolled[:SUBLANES], 0.0)
            a = a + jnp.concatenate([top, rolled[SUBLANES:]], axis=0)
        for sh in (8, 16, 32):
            a = a + jnp.concatenate([jnp.zeros((sh, a.shape[1]), F32), a[:CHUNK - sh]], axis=0)
        e_q = jnp.exp2(a)
        e_k = jnp.exp2(a[CHUNK - 1:CHUNK] - a)
        e_last = e_q[CHUNK - 1:CHUNK]
        qt = (q * e_q).astype(BF16)
        kt = (k * e_k).astype(BF16)
        qk = q * k
        vb = v.astype(BF16)
        pair = q * jnp.exp2(g) * _prev_row_in_group(k)
        v_prev = _prev_row_in_group(v)
        ops = []
        for li, m in enumerate(LEVELS):
            ex = jnp.exp2(_level_exponent(a, m))
            ops.append((_pick_halves(k, q, m, second[li]) * ex).astype(BF16))
        return ops, qt, kt, vb, qk, v, gate, e_last, pair, v_prev

    def heads(c, w, states):
        ops, qt, kt, vb, qk, v, gate, e_last, pair, v_prev = w
        rs = pl.ds(pl.multiple_of(c * CHUNK, CHUNK), CHUNK)
        sls = [slice(h * HG_DK, (h + 1) * HG_DK) for h in range(HG_HEADS)]
        ps = []
        for sl in sls:
            groups = [None] * (CHUNK // SUBLANES)
            for li, m in enumerate(LEVELS):
                r = _dot_nt(ops[li][:, sl], ops[li][:, sl])
                for gi in range(len(groups)):
                    if m >= SUBLANES and not (gi * SUBLANES) & m:
                        continue
                    gs = slice(gi * SUBLANES, (gi + 1) * SUBLANES)
                    term = r[gs] * mk_ref[li, gs, :]
                    groups[gi] = term if groups[gi] is None else groups[gi] + term
            ps.append(jnp.concatenate(groups, axis=0).astype(BF16))
        new_states = [states[h] * e_last[:, sl] + _dot_tn(vb[:, sl], kt[:, sl]) for h, sl in enumerate(sls)]
        os_ = [_dot_nt(qt[:, sl], states[h].astype(BF16)) + _dot(ps[h], vb[:, sl]) for h, sl in enumerate(sls)]
        for h, sl in enumerate(sls):
            o = os_[h] + jnp.sum(qk[:, sl], axis=-1, keepdims=True) * v[:, sl]
            c1 = jnp.where(odd, jnp.sum(pair[:, sl], axis=-1, keepdims=True), 0.0)
            o = o + c1 * v_prev[:, sl]
            og = _rms(o, go) * gate[:, sl]
            o_ref[0, rs, sl] = og.astype(BF16)
        return new_states

    def chunk_body(c, carry):
        states = heads(c, wide(c), [st_ref[0, h] for h in range(HG_HEADS)])
        for h in range(HG_HEADS):
            st_ref[0, h] = states[h]
        return carry

    lax.fori_loop(0, tc // CHUNK, chunk_body, 0)

    @pl.when(t == pl.num_programs(1) - 1)
    def _():
        for h in range(HG_HEADS):
            st_ref[0, h] = st_ref[0, h].T


def _gla(z, lb_logits, layer, g_o, s0, s0_layer):
    b, t, zw = z.shape
    dh = zw // 4
    tc = _tile(t, 256)
    masks = jnp.asarray(_level_masks(), F32)
    zspec = lambda c: pl.BlockSpec((1, tc, dh), lambda i, j, c=c: (i, j, c))
    sshape = s0.shape[2:]
    assert sshape[-1] == sshape[-2]
    return pl.pallas_call(
        functools.partial(_gla_kernel, layer=layer),
        grid=(b, t // tc),
        in_specs=[zspec(0), zspec(1), zspec(2), zspec(3),
                  pl.BlockSpec(lb_logits.shape, lambda i, j: (0, 0)),
                  pl.BlockSpec((1, HG_DV), lambda i, j: (0, 0)),
                  pl.BlockSpec((None, 1) + sshape, lambda i, j: (s0_layer, i, 0, 0, 0)),
                  pl.BlockSpec(masks.shape, lambda i, j: (0, 0, 0))],
        out_specs=[pl.BlockSpec((1, tc, dh), lambda i, j: (i, j, 0)),
                   pl.BlockSpec((1,) + sshape, lambda i, j: (i, 0, 0, 0))],
        out_shape=[jax.ShapeDtypeStruct((b, t, dh), BF16), jax.ShapeDtypeStruct((b,) + sshape, F32)],
        compiler_params=_params("parallel", "arbitrary"),
        name="hgrn2_recurrence",
    )(z, z, z, z, lb_logits, g_o.reshape(1, HG_DV), s0, masks)


def _rope_tables(pos):
    half = ROPE_DIM // 2
    inv = ROPE_THETA ** (-jnp.arange(half, dtype=F32) / half)
    ang = pos.astype(F32)[:, None] * inv[None, :]
    cos, sin = jnp.cos(ang), jnp.sin(ang)
    c1 = jnp.concatenate([cos, cos], axis=1)
    s1 = jnp.concatenate([-sin, sin], axis=1)
    return c1, s1, jnp.tile(c1, (1, MLA_HEADS)), jnp.tile(s1, (1, MLA_HEADS))


def _swap_halves(w):
    k, c = w.shape
    w = w.reshape(k, c // ROPE_DIM, 2, ROPE_DIM // 2)
    return w[:, :, ::-1, :].reshape(k, c)


def _prep_even(w_in_a, w_uq, w_ukv, w_out_a, dc):
    d = w_in_a.shape[0]
    kpe = w_in_a[:, -ROPE_DIM:]
    w_in = jnp.concatenate([w_in_a, _swap_halves(kpe)], axis=1).astype(BF16)
    ql = w_uq.shape[0]
    wq = w_uq.reshape(ql, MLA_HEADS, NOPE_DIM + ROPE_DIM)
    wqn = wq[:, :, :NOPE_DIM].reshape(ql, MLA_HEADS * NOPE_DIM).astype(BF16)
    wqp = wq[:, :, NOPE_DIM:].reshape(ql, MLA_HEADS * ROPE_DIM)
    wqs = _swap_halves(wqp).astype(BF16)
    kl = w_ukv.shape[0]
    wkv = w_ukv.reshape(kl, MLA_HEADS, NOPE_DIM + V_DIM)
    wuk = jnp.transpose(wkv[:, :, :NOPE_DIM], (1, 2, 0)).astype(BF16)
    wuv = jnp.transpose(wkv[:, :, NOPE_DIM:], (1, 0, 2)).astype(BF16)
    wo = w_out_a.astype(BF16)
    return dict(w_in=w_in, wqn=wqn, wqp=wqp.astype(BF16), wqs=wqs, wuk=wuk, wuv=wuv, wo_a=wo[:dc], wo_b=wo[dc:])


def _run_group(x3, pos, conv_prev, ckv_prev, kpe_prev, hgrn_prev, p):
    b, t, d = x3.shape
    n = b * t
    depth = p["norm_ffn1"].shape[0]
    x = x3.reshape(n, d)
    c1, s1, c8, s8 = _rope_tables(pos)
    conv_new, ckv_new, kpe_new, hgrn_new = [], [], [], []
    for l in range(depth):
        x = _ffn(x, p["norm_ffn1"], p["wg1"], p["wu1"], p["wd1"], l)
        if l % 2 == 0:
            e = l // 2
            pe = p["even"][e]
            dc = p["w_conv"].shape[2]
            cprev = jnp.zeros((b, CONV_W - 1, dc), F32) if conv_prev is None else conv_prev[e]
            ya, cs, q, ckv, kpe, kv = _even_in(x.reshape(b, t, d), p["norm_mix"][l], pe["w_in"], cprev,
                                               p["w_conv"][e], p["g_q"][e], pe["wqn"], pe["wqp"], pe["wqs"],
                                               pe["wuk"], p["g_kv"][e], c1, s1, c8, s8)
            if ckv_prev is None:
                yb = _attention(q, kv, pe["wuv"])
            else:
                yb = _attention_cached(q, kv, ckv_prev, kpe_prev, e, pe["wuv"])
            x = _matmul_residual([ya.reshape(n, -1), yb.reshape(n, -1)], [pe["wo_a"], pe["wo_b"]], x)
            conv_new.append(cs)
            ckv_new.append(ckv)
            kpe_new.append(kpe)
        else:
            o = l // 2
            z = _norm_matmul(x, p["norm_mix"][l], p["w_in_c"], o).reshape(b, t, -1)
            if hgrn_prev is None:
                og, st = _gla(z, p["lb_logits"], o, p["g_o"][o], jnp.zeros((1, b, HG_HEADS, HG_DK, HG_DV), F32), 0)
            else:
                og, st = _gla(z, p["lb_logits"], o, p["g_o"][o], hgrn_prev, o)
            x = _matmul_residual([og.reshape(n, -1)], [p["w_out_c"][o]], x)
            hgrn_new.append(st)
        g_final = p["norm_final"] if l == depth - 1 else None
        x = _ffn(x, p["norm_ffn2"], p["wg2"], p["wu2"], p["wd2"], l, g_final)
    return x.reshape(b, t, d), jnp.stack(conv_new), jnp.stack(ckv_new), jnp.stack(kpe_new), jnp.stack(hgrn_new)


def kernel(x_prompt, x_sample, cache_conv, cache_ckv, cache_kpe, state_hgrn, norm_ffn1, w_ffn1_gate, w_ffn1_up, w_ffn1_down, norm_mix, w_in_a, w_conv, g_q, w_uq, g_kv, w_ukv, w_out_a, w_in_c, lb_logits, g_o, w_out_c, norm_ffn2, w_ffn2_gate, w_ffn2_up, w_ffn2_down, norm_final):
    dc = w_conv.shape[2]
    p = dict(
        norm_ffn1=norm_ffn1, wg1=w_ffn1_gate.astype(BF16), wu1=w_ffn1_up.astype(BF16), wd1=w_ffn1_down.astype(BF16),
        norm_ffn2=norm_ffn2, wg2=w_ffn2_gate.astype(BF16), wu2=w_ffn2_up.astype(BF16), wd2=w_ffn2_down.astype(BF16),
        norm_mix=norm_mix, w_conv=w_conv, g_q=g_q, g_kv=g_kv,
        even=[_prep_even(w_in_a[e], w_uq[e], w_ukv[e], w_out_a[e], dc) for e in range(w_in_a.shape[0])],
        w_in_c=w_in_c.astype(BF16), lb_logits=lb_logits, g_o=g_o, w_out_c=w_out_c.astype(BF16),
        norm_final=norm_final,
    )
    pos_p = jnp.arange(x_prompt.shape[1], dtype=jnp.int32)
    pos_s = cache_ckv.shape[2] + jnp.arange(x_sample.shape[1], dtype=jnp.int32)
    y_p, conv_p, ckv_p, kpe_p, hgrn_p = _run_group(x_prompt, pos_p, None, None, None, None, p)
    y_s, conv_s, ckv_s, kpe_s, hgrn_s = _run_group(x_sample, pos_s, cache_conv, cache_ckv, cache_kpe, state_hgrn, p)
    return (y_p, y_s, conv_p, ckv_p, kpe_p, hgrn_p, conv_s, ckv_s, kpe_s, hgrn_s)
```

```python
import functools

import numpy as np
import jax
import jax.numpy as jnp
from jax import lax
from jax.experimental import pallas as pl
from jax.experimental.pallas import tpu as pltpu

F32 = jnp.float32
BF16 = jnp.bfloat16

EPS = 1e-6
CHUNK = 64
CONV_W = 3
MLA_HEADS = 8
NOPE_DIM = 128
ROPE_DIM = 64
V_DIM = 128
ROPE_THETA = 10000.0
MLA_SCALE = (NOPE_DIM + ROPE_DIM) ** -0.5
HG_HEADS = 16
HG_DK = 128
HG_DV = 128
LANES = 128
SUBLANES = 8
NEG_BIG = -1e30
LOG2E = 1.4426950408889634
LEVELS = (32, 16, 8, 4, 2)
VMEM_LIMIT_BYTES = 60 * 1024 * 1024
FFN_TF = 512


def _tile(n, pref):
    if n <= pref:
        return n
    for t in range(pref, 7, -8):
        if n % t == 0:
            return t
    return n


def _col_tile(n, pref):
    for t in range(min(pref, n) // LANES * LANES, 0, -LANES):
        if n % t == 0:
            return t
    return n


def _params(*sem, **kw):
    return pltpu.CompilerParams(dimension_semantics=sem, vmem_limit_bytes=VMEM_LIMIT_BYTES, **kw)


def _sigmoid(x):
    return 1.0 / (1.0 + jnp.exp(-x))


def _silu(x):
    h = 0.5 * x
    return h + h * jnp.tanh(h)


def _rms(x, g):
    return x * lax.rsqrt(jnp.mean(x * x, axis=-1, keepdims=True) + EPS) * g


def _dot(a, b):
    return jnp.dot(a, b, preferred_element_type=F32)


def _dot_nt(a, b):
    return lax.dot_general(a, b, (((1,), (1,)), ((), ())), preferred_element_type=F32)


def _dot_tn(a, b):
    return lax.dot_general(a, b, (((0,), (0,)), ((), ())), preferred_element_type=F32)


def _ffn_kernel(x_ref, g_ref, wg_ref, wu_ref, wd_ref, *rest, final_norm):
    if final_norm:
        gf_ref, o_ref, n_ref = rest
    else:
        o_ref, n_ref = rest
    j = pl.program_id(1)
    tm = x_ref.shape[0]
    rc = min(tm, 128)

    def rows(i):
        return pl.ds(pl.multiple_of(i * rc, rc), rc)

    @pl.when(j == 0)
    def _():
        def norm_rows(i, carry):
            x = x_ref[rows(i), :]
            n_ref[rows(i), :] = _rms(x, g_ref[...]).astype(BF16)
            o_ref[rows(i), :] = x
            return carry

        lax.fori_loop(0, tm // rc, norm_rows, 0)

    n = n_ref[...]
    hg = _dot(n, wg_ref[...])
    hu = _dot(n, wu_ref[...])
    a = _silu(hg) * (0.5 * hu)
    o_ref[...] += _dot(a.astype(BF16), wd_ref[...])

    if final_norm:
        @pl.when(j == pl.num_programs(1) - 1)
        def _():
            def norm_out_rows(i, carry):
                o_ref[rows(i), :] = _rms(o_ref[rows(i), :], gf_ref[...])
                return carry

            lax.fori_loop(0, tm // rc, norm_out_rows, 0)


def _ffn(x, g, wg, wu, wd, layer, g_final=None):
    n, d = x.shape
    f = wg.shape[2]
    tm = _tile(n, 1024)
    tf = _col_tile(f, FFN_TF)
    in_specs = [
        pl.BlockSpec((tm, d), lambda i, j: (i, 0)),
        pl.BlockSpec((None, 1, d), lambda i, j: (layer, 0, 0)),
        pl.BlockSpec((None, d, tf), lambda i, j: (layer, 0, j)),
        pl.BlockSpec((None, d, tf), lambda i, j: (layer, 0, j)),
        pl.BlockSpec((None, tf, d), lambda i, j: (layer, j, 0)),
    ]
    args = [x, g.reshape(g.shape[0], 1, d), wg, wu, wd]
    if g_final is not None:
        in_specs.append(pl.BlockSpec((1, d), lambda i, j: (0, 0)))
        args.append(g_final.reshape(1, d))
    return pl.pallas_call(
        functools.partial(_ffn_kernel, final_norm=g_final is not None),
        grid=(n // tm, f // tf),
        in_specs=in_specs,
        out_specs=pl.BlockSpec((tm, d), lambda i, j: (i, 0)),
        out_shape=jax.ShapeDtypeStruct((n, d), F32),
        scratch_shapes=[pltpu.VMEM((tm, d), BF16)],
        compiler_params=_params("parallel", "arbitrary"),
        name="ffn",
    )(*args)


def _nmm_kernel(x_ref, g_ref, w_ref, o_ref, n_ref):
    tm = x_ref.shape[0]
    rc = min(tm, 128)

    @pl.when(pl.program_id(1) == 0)
    def _():
        def norm_rows(i, carry):
            rows = pl.ds(pl.multiple_of(i * rc, rc), rc)
            n_ref[rows, :] = _rms(x_ref[rows, :], g_ref[...]).astype(BF16)
            return carry

        lax.fori_loop(0, tm // rc, norm_rows, 0)

    o_ref[...] = _dot(n_ref[...], w_ref[...])


def _norm_matmul(x, g, w, layer):
    n, d = x.shape
    c = w.shape[2]
    tm = _tile(n, 1024)
    tn = _col_tile(c, 2048)
    return pl.pallas_call(
        _nmm_kernel,
        grid=(n // tm, c // tn),
        in_specs=[
            pl.BlockSpec((tm, d), lambda i, j: (i, 0)),
            pl.BlockSpec((1, d), lambda i, j: (0, 0)),
            pl.BlockSpec((None, d, tn), lambda i, j: (layer, 0, j)),
        ],
        out_specs=pl.BlockSpec((tm, tn), lambda i, j: (i, j)),
        out_shape=jax.ShapeDtypeStruct((n, c), F32),
        scratch_shapes=[pltpu.VMEM((tm, d), BF16)],
        compiler_params=_params("parallel", "arbitrary"),
        name="norm_matmul",
    )(x, g.reshape(1, d), w)


def _mmres_kernel(*refs, n_in):
    x_ref, o_ref = refs[2 * n_in], refs[2 * n_in + 1]
    acc = x_ref[...]
    for a_ref, w_ref in zip(refs[:n_in], refs[n_in:2 * n_in]):
        acc = acc + _dot(a_ref[...], w_ref[...])
    o_ref[...] = acc


def _matmul_residual(acts, weights, x):
    n, d = x.shape
    tm = _tile(n, 512)
    in_specs = [pl.BlockSpec((tm, a.shape[1]), lambda i: (i, 0)) for a in acts]
    in_specs += [pl.BlockSpec(w.shape, lambda i: (0, 0)) for w in weights]
    in_specs.append(pl.BlockSpec((tm, d), lambda i: (i, 0)))
    return pl.pallas_call(
        functools.partial(_mmres_kernel, n_in=len(acts)),
        grid=(n // tm,),
        in_specs=in_specs,
        out_specs=pl.BlockSpec((tm, d), lambda i: (i, 0)),
        out_shape=jax.ShapeDtypeStruct((n, d), F32),
        compiler_params=_params("parallel"),
        name="matmul_residual",
    )(*acts, *weights, x)


def _even_in_kernel(x_ref, gn_ref, win_ref, cp_ref, wc_ref, gq_ref, wqn_ref, wqp_ref, wqs_ref, wuk_ref, gkv_ref,
                    c1_ref, s1_ref, c8_ref, s8_ref,
                    ya_ref, cs_ref, q_ref, ckv_ref, kpe_ref, kv_ref, vp_ref, *, dc, ql, kl):
    t = pl.program_id(1)
    tt = x_ref.shape[1]
    o3 = 3 * dc
    o4 = o3 + ql
    o5 = o4 + kl
    n = _rms(x_ref[0], gn_ref[...]).astype(BF16)

    def proj(c0, c1):
        return _dot(n, win_ref[:, c0:c1])

    cqn = _rms(proj(o3, o4), gq_ref[...]).astype(BF16)
    qn = _dot(cqn, wqn_ref[...])
    qpr = _dot(cqn, wqp_ref[...]) * c8_ref[...] + _dot(cqn, wqs_ref[...]) * s8_ref[...]
    v = proj(dc, 2 * dc) * proj(2 * dc, o3)
    gb = proj(0, dc)
    cn = _rms(proj(o4, o5), gkv_ref[...])
    kp = proj(o5, o5 + 2 * ROPE_DIM)
    kr = kp[:, 0:ROPE_DIM] * c1_ref[...] + kp[:, ROPE_DIM:2 * ROPE_DIM] * s1_ref[...]
    for h in range(MLA_HEADS):
        qlat = _dot(qn[:, h * NOPE_DIM:(h + 1) * NOPE_DIM].astype(BF16), wuk_ref[h])
        q_ref[0, h, :, 0:kl] = qlat.astype(BF16)
        q_ref[0, h, :, kl:kl + ROPE_DIM] = qpr[:, h * ROPE_DIM:(h + 1) * ROPE_DIM].astype(BF16)
    cp = cp_ref[0]
    first = t == 0
    pm2 = jnp.where(first, cp[0:1], vp_ref[SUBLANES - 2:SUBLANES - 1, :])
    pm1 = jnp.where(first, cp[1:2], vp_ref[SUBLANES - 1:SUBLANES, :])
    rows = lax.broadcasted_iota(jnp.int32, v.shape, 0)
    v1 = jnp.where(rows == 0, pm1, pltpu.roll(v, 1, 0))
    v2 = jnp.where(rows == 0, pm2, jnp.where(rows == 1, pm1, pltpu.roll(v, 2, 0)))
    w = wc_ref[...]
    conv = v2 * w[0:1] + v1 * w[1:2] + v * w[2:3]
    ya_ref[0] = (gb * conv).astype(BF16)
    vp_ref[...] = v[tt - SUBLANES:tt]
    ckv_ref[0] = cn
    kpe_ref[0] = kr
    kv_ref[0, :, 0:kl] = cn.astype(BF16)
    kv_ref[0, :, kl:kl + ROPE_DIM] = kr.astype(BF16)

    @pl.when(t == pl.num_programs(1) - 1)
    def _():
        cs_ref[0] = v[tt - (CONV_W - 1):tt]


EVEN_IN_INPUTS = 15


def _even_in_entry(*refs, n_alias, **kw):
    return _even_in_kernel(*refs[:EVEN_IN_INPUTS], *refs[EVEN_IN_INPUTS + n_alias:], **kw)


def _even_in(x, g_norm, w_in, conv_prev, w_conv, g_q, wqn, wqp, wqs, wuk, g_kv, c1, s1, c8, s8, e, n_even, stacks):
    b, t, d = x.shape
    dc = w_conv.shape[1]
    ql = g_q.shape[0]
    kl = g_kv.shape[0]
    tt = _tile(t, 256)
    hr = MLA_HEADS * ROPE_DIM
    once = lambda a: pl.BlockSpec(a.shape, lambda i, j: (0,) * a.ndim, pipeline_mode=pl.Buffered(1))
    wts = [g_norm.reshape(1, d), w_in]
    wts2 = [w_conv, g_q.reshape(1, ql), wqn, wqp, wqs, wuk, g_kv.reshape(1, kl)]
    in_specs = [pl.BlockSpec((1, tt, d), lambda i, j: (i, j, 0))] + [once(a) for a in wts] + [
        pl.BlockSpec((1, CONV_W - 1, dc), lambda i, j: (i, 0, 0)),
    ] + [once(a) for a in wts2] + [
        pl.BlockSpec((tt, ROPE_DIM), lambda i, j: (j, 0)),
        pl.BlockSpec((tt, ROPE_DIM), lambda i, j: (j, 0)),
        pl.BlockSpec((tt, hr), lambda i, j: (j, 0)),
        pl.BlockSpec((tt, hr), lambda i, j: (j, 0)),
    ]
    out_shape = [
        jax.ShapeDtypeStruct((b, t, dc), BF16),
        jax.ShapeDtypeStruct((b, CONV_W - 1, dc), F32),
        jax.ShapeDtypeStruct((b, MLA_HEADS, t, kl + ROPE_DIM), BF16),
        jax.ShapeDtypeStruct((n_even, b, t, kl), F32),
        jax.ShapeDtypeStruct((n_even, b, t, ROPE_DIM), F32),
        jax.ShapeDtypeStruct((b, t, kl + ROPE_DIM), BF16),
    ]
    out_specs = [
        pl.BlockSpec((1, tt, dc), lambda i, j: (i, j, 0)),
        pl.BlockSpec((1, CONV_W - 1, dc), lambda i, j: (i, 0, 0)),
        pl.BlockSpec((1, MLA_HEADS, tt, kl + ROPE_DIM), lambda i, j: (i, 0, j, 0)),
        pl.BlockSpec((None, 1, tt, kl), lambda i, j: (e, i, j, 0)),
        pl.BlockSpec((None, 1, tt, ROPE_DIM), lambda i, j: (e, i, j, 0)),
        pl.BlockSpec((1, tt, kl + ROPE_DIM), lambda i, j: (i, j, 0)),
    ]
    args = [x, *wts, conv_prev, *wts2, c1, s1, c8, s8]
    assert len(args) == EVEN_IN_INPUTS
    aliases = {}
    if stacks is not None:
        in_specs += [pl.BlockSpec(memory_space=pl.ANY)] * len(stacks)
        aliases = {EVEN_IN_INPUTS: 3, EVEN_IN_INPUTS + 1: 4}
        args += list(stacks)
    return pl.pallas_call(
        functools.partial(_even_in_entry, n_alias=len(aliases), dc=dc, ql=ql, kl=kl),
        grid=(b, t // tt),
        in_specs=in_specs,
        out_specs=out_specs,
        out_shape=out_shape,
        input_output_aliases=aliases,
        scratch_shapes=[pltpu.VMEM((SUBLANES, dc), F32)],
        compiler_params=_params("parallel", "arbitrary"),
        name="even_in",
    )(*args)


def _online_softmax(score_fns, value_fns, mask_last):
    c = MLA_SCALE * LOG2E
    n = len(score_fns)
    s = score_fns[0]()
    m = l = acc = None
    for j in range(n):
        s_next = score_fns[j + 1]() if j + 1 < n else None
        if j == n - 1 and mask_last is not None:
            s = mask_last(s)
        m_new = jnp.max(s, axis=-1, keepdims=True)
        if m is not None:
            m_new = jnp.maximum(m, m_new)
        p = jnp.exp2((s - m_new) * c)
        p_sum = jnp.sum(p, axis=-1, keepdims=True)
        pv = _dot(p.astype(BF16), value_fns[j]())
        if m is None:
            l, acc = p_sum, pv
        else:
            alpha = jnp.exp2((m - m_new) * c)
            l = alpha * l + p_sum
            acc = alpha * acc + pv
        m, s = m_new, s_next
    return acc * (1.0 / l)


def _value_up(o, wuv_ref, o_ref, tq):
    o = o.astype(BF16)
    for h in range(MLA_HEADS):
        o_ref[0, :, h * V_DIM:(h + 1) * V_DIM] = _dot(o[h * tq:(h + 1) * tq], wuv_ref[h]).astype(BF16)


def _attn_kernel(q_ref, kv_ref, wuv_ref, o_ref, *, tq, tk, kl):
    i = pl.program_id(1)
    rows = MLA_HEADS * tq
    q = q_ref[0].reshape(rows, q_ref.shape[3])
    tok = i * tq + lax.rem(lax.broadcasted_iota(jnp.int32, (rows, 1), 0), tq)
    limit = (tok // CHUNK + 1) * CHUNK

    def run(n_full, tail):
        bounds = [(j * tk, (j + 1) * tk) for j in range(n_full)] + [(n_full * tk, n_full * tk + tail)]

        def mask_last(s):
            kpos = n_full * tk + lax.broadcasted_iota(jnp.int32, (1, tail), 1)
            return jnp.where(kpos < limit, s, NEG_BIG)

        score_fns = [functools.partial(lambda lo, hi: _dot_nt(q, kv_ref[0, lo:hi, :]), lo, hi) for lo, hi in bounds]
        value_fns = [functools.partial(lambda lo, hi: kv_ref[0, lo:hi, 0:kl], lo, hi) for lo, hi in bounds]
        _value_up(_online_softmax(score_fns, value_fns, mask_last), wuv_ref, o_ref, tq)

    start = i * tq
    n_full = start // tk
    in_second_half = lax.rem(start, tk) >= tk // 2
    for a in range(kv_ref.shape[1] // tk):
        pl.when((n_full == a) & jnp.logical_not(in_second_half))(functools.partial(run, a, tk // 2))
        pl.when((n_full == a) & in_second_half)(functools.partial(run, a, tk))


def _attn_cache_kernel(q_ref, kv_ref, cc_ref, ck_ref, wuv_ref, o_ref, *, tk, kl):
    tq = q_ref.shape[2]
    rows = MLA_HEADS * tq
    q = q_ref[0].reshape(rows, q_ref.shape[3])
    q_lat, q_pe = q[:, 0:kl], q[:, kl:]
    n_cache = cc_ref.shape[1] // tk

    def cache_scores(j):
        ks = slice(j * tk, (j + 1) * tk)
        return _dot_nt(q_lat, cc_ref[0, ks, :].astype(BF16)) + _dot_nt(q_pe, ck_ref[0, ks, :].astype(BF16))

    score_fns = [functools.partial(cache_scores, j) for j in range(n_cache)]
    value_fns = [functools.partial(lambda j: cc_ref[0, j * tk:(j + 1) * tk, :].astype(BF16), j) for j in range(n_cache)]
    score_fns.append(lambda: _dot_nt(q, kv_ref[0]))
    value_fns.append(lambda: kv_ref[0, :, 0:kl])
    _value_up(_online_softmax(score_fns, value_fns, None), wuv_ref, o_ref, tq)


def _attention_cached(q, kv_new, cache_ckv, cache_kpe, layer, wuv):
    b, h, t, dq = q.shape
    past = cache_ckv.shape[2]
    kl = wuv.shape[1]
    tk = _tile(past, 512)
    assert past % tk == 0
    return pl.pallas_call(
        functools.partial(_attn_cache_kernel, tk=tk, kl=kl),
        grid=(b,),
        in_specs=[
            pl.BlockSpec((1, h, t, dq), lambda i: (i, 0, 0, 0)),
            pl.BlockSpec((1, t, dq), lambda i: (i, 0, 0)),
            pl.BlockSpec((None, 1, past, kl), lambda i: (layer, i, 0, 0)),
            pl.BlockSpec((None, 1, past, dq - kl), lambda i: (layer, i, 0, 0)),
            pl.BlockSpec(wuv.shape, lambda i: (0, 0, 0)),
        ],
        out_specs=pl.BlockSpec((1, t, h * V_DIM), lambda i: (i, 0, 0)),
        out_shape=jax.ShapeDtypeStruct((b, t, h * V_DIM), BF16),
        compiler_params=_params("parallel"),
        name="mla_attention_cached",
    )(q, kv_new, cache_ckv, cache_kpe, wuv)


def _attention(q, kv, wuv):
    b, h, t, dq = q.shape
    tkv = kv.shape[1]
    kl = wuv.shape[1]
    tq = _tile(t, 128)
    tk = _tile(tkv, 512)
    assert (tk // 2) % tq == 0 and tkv % tk == 0 and tq % CHUNK == 0
    return pl.pallas_call(
        functools.partial(_attn_kernel, tq=tq, tk=tk, kl=kl),
        grid=(b, t // tq),
        in_specs=[
            pl.BlockSpec((1, h, tq, dq), lambda i, j: (i, 0, j, 0)),
            pl.BlockSpec((1, tkv, dq), lambda i, j: (i, 0, 0)),
            pl.BlockSpec(wuv.shape, lambda i, j: (0, 0, 0)),
        ],
        out_specs=pl.BlockSpec((1, tq, h * V_DIM), lambda i, j: (i, j, 0)),
        out_shape=jax.ShapeDtypeStruct((b, t, h * V_DIM), BF16),
        compiler_params=_params("parallel", "arbitrary"),
        name="mla_attention",
    )(q, kv, wuv)


def _level_masks():
    t = np.arange(CHUNK)[:, None]
    s = np.arange(CHUNK)[None, :]
    masks = [(t // (2 * m) == s // (2 * m)) & ((t & m) != 0) & ((s & m) == 0) for m in LEVELS]
    return np.stack(masks).astype(np.float32)


def _boundary_rows(a, m):
    w = a.shape[1]
    if m >= SUBLANES:
        parts = [jnp.broadcast_to(a[b0 + m - 1:b0 + m], (2 * m, w)) for b0 in range(0, CHUNK, 2 * m)]
        return parts[0] if len(parts) == 1 else jnp.concatenate(parts, axis=0)
    a3 = a.reshape(CHUNK // SUBLANES, SUBLANES, w)
    sub = lax.broadcasted_iota(jnp.int32, a3.shape, 1)
    out = None
    for b0 in range(SUBLANES - 2 * m, -1, -2 * m):
        piece = jnp.broadcast_to(a3[:, b0 + m - 1:b0 + m, :], a3.shape)
        out = piece if out is None else jnp.where(sub < b0 + 2 * m, piece, out)
    return out.reshape(CHUNK, w)


def _prev_row_in_group(x):
    w = x.shape[1]
    x3 = x.reshape(CHUNK // SUBLANES, SUBLANES, w)
    return pltpu.roll(x3, 1, 1).reshape(CHUNK, w)


def _level_exponent(a, m):
    if m < SUBLANES:
        return -jnp.abs(a - _boundary_rows(a, m))
    parts = []
    for b0 in range(0, CHUNK, 2 * m):
        r = a[b0 + m - 1:b0 + m]
        parts += [r - a[b0:b0 + m], a[b0 + m:b0 + 2 * m] - r]
    return jnp.concatenate(parts, axis=0)


def _pick_halves(first, second, m, is_second):
    if m < SUBLANES:
        return jnp.where(is_second, second, first)
    parts = []
    for b0 in range(0, CHUNK, 2 * m):
        parts += [first[b0:b0 + m], second[b0 + m:b0 + 2 * m]]
    return jnp.concatenate(parts, axis=0)


def _gla_kernel(zq_ref, zf_ref, zi_ref, zg_ref, lbl_ref, go_ref, s0_ref, mk_ref, o_ref, st_ref, *, layer):
    t = pl.program_id(1)
    tc = zq_ref.shape[1]
    n_lv = len(LEVELS)

    @pl.when(t == 0)
    def _():
        st_ref[...] = s0_ref[...]

    ll = lbl_ref[...]
    e = jnp.exp(ll - jnp.max(ll, axis=0, keepdims=True))
    sm = e / jnp.sum(e, axis=0, keepdims=True)
    lb = jnp.zeros((1, ll.shape[1]), F32)
    for r in range(1, layer + 1):
        lb = lb + sm[r:r + 1]
    one_m_lb = 1.0 - lb
    go = go_ref[...]
    row = lax.broadcasted_iota(jnp.int32, (CHUNK, 1), 0)
    second = [(row & m) != 0 for m in LEVELS]
    odd = (row & 1) != 0

    def wide(c):
        rs = pl.ds(pl.multiple_of(c * CHUNK, CHUNK), CHUNK)
        zq = zq_ref[0, rs, :]
        zf = zf_ref[0, rs, :]
        v = zi_ref[0, rs, :]
        zg = zg_ref[0, rs, :]
        sig = _sigmoid(zf)
        g = jnp.log(lb + one_m_lb * sig) * LOG2E
        k = one_m_lb * (1.0 - sig)
        q = _silu(zq)
        gate = _silu(zg)
        a = g
        for sh in (1, 2, 4):
            rolled = pltpu.roll(a, sh, 0)
            top = jnp.where(row[:SUBLANES] >= sh, rolled[:SUBLANES], 0.0)
            a = a + jnp.concatenate([top, rolled[SUBLANES:]], axis=0)
        for sh in (8, 16, 32):
            a = a + jnp.concatenate([jnp.zeros((sh, a.shape[1]), F32), a[:CHUNK - sh]], axis=0)
        e_q = jnp.exp2(a)
        e_k = jnp.exp2(a[CHUNK - 1:CHUNK] - a)
        e_last = e_q[CHUNK - 1:CHUNK]
        qt = (q * e_q).astype(BF16)
        kt = (k * e_k).astype(BF16)
        qk = q * k
        vb = v.astype(BF16)
        pair = q * jnp.exp2(g) * _prev_row_in_group(k)
        v_prev = _prev_row_in_group(v)
        ops = []
        for li, m in enumerate(LEVELS):
            ex = jnp.exp2(_level_exponent(a, m))
            ops.append((_pick_halves(k, q, m, second[li]) * ex).astype(BF16))
        return ops, qt, kt, vb, qk, v, gate, e_last, pair, v_prev

    def heads(c, w, states):
        ops, qt, kt, vb, qk, v, gate, e_last, pair, v_prev = w
        rs = pl.ds(pl.multiple_of(c * CHUNK, CHUNK), CHUNK)
        sls = [slice(h * HG_DK, (h + 1) * HG_DK) for h in range(HG_HEADS)]
        ps = []
        for sl in sls:
            groups = [None] * (CHUNK // SUBLANES)
            for li, m in enumerate(LEVELS):
                r = _dot_nt(ops[li][:, sl], ops[li][:, sl])
                for gi in range(len(groups)):
                    if m >= SUBLANES and not (gi * SUBLANES) & m:
                        continue
                    gs = slice(gi * SUBLANES, (gi + 1) * SUBLANES)
                    term = r[gs] * mk_ref[li, gs, :]
                    groups[gi] = term if groups[gi] is None else groups[gi] + term
            ps.append(jnp.concatenate(groups, axis=0).astype(BF16))
        new_states = [states[h] * e_last[:, sl] + _dot_tn(vb[:, sl], kt[:, sl]) for h, sl in enumerate(sls)]
        os_ = [_dot_nt(qt[:, sl], states[h].astype(BF16)) + _dot(ps[h], vb[:, sl]) for h, sl in enumerate(sls)]
        for h, sl in enumerate(sls):
            o = os_[h] + jnp.sum(qk[:, sl], axis=-1, keepdims=True) * v[:, sl]
            c1 = jnp.where(odd, jnp.sum(pair[:, sl], axis=-1, keepdims=True), 0.0)
            o = o + c1 * v_prev[:, sl]
            og = _rms(o, go) * gate[:, sl]
            o_ref[0, rs, sl] = og.astype(BF16)
        return new_states

    def chunk_body(c, carry):
        states = heads(c, wide(c), [st_ref[0, h] for h in range(HG_HEADS)])
        for h in range(HG_HEADS):
            st_ref[0, h] = states[h]
        return carry

    lax.fori_loop(0, tc // CHUNK, chunk_body, 0)


def _gla(z, lb_logits, layer, g_o, s0t):
    b, t, zw = z.shape
    dh = zw // 4
    tc = _tile(t, 256)
    masks = jnp.asarray(_level_masks(), F32)
    zspec = lambda c: pl.BlockSpec((1, tc, dh), lambda i, j, c=c: (i, j, c))
    sspec = pl.BlockSpec((1,) + s0t.shape[1:], lambda i, j: (i, 0, 0, 0))
    return pl.pallas_call(
        functools.partial(_gla_kernel, layer=layer),
        grid=(b, t // tc),
        in_specs=[zspec(0), zspec(1), zspec(2), zspec(3),
                  pl.BlockSpec(lb_logits.shape, lambda i, j: (0, 0)),
                  pl.BlockSpec((1, HG_DV), lambda i, j: (0, 0)),
                  sspec,
                  pl.BlockSpec(masks.shape, lambda i, j: (0, 0, 0))],
        out_specs=[pl.BlockSpec((1, tc, dh), lambda i, j: (i, j, 0)), sspec],
        out_shape=[jax.ShapeDtypeStruct((b, t, dh), BF16), jax.ShapeDtypeStruct(s0t.shape, F32)],
        compiler_params=_params("parallel", "arbitrary"),
        name="hgrn2_recurrence",
    )(z, z, z, z, lb_logits, g_o.reshape(1, HG_DV), s0t, masks)


def _rope_tables(pos):
    half = ROPE_DIM // 2
    inv = ROPE_THETA ** (-jnp.arange(half, dtype=F32) / half)
    ang = pos.astype(F32)[:, None] * inv[None, :]
    cos, sin = jnp.cos(ang), jnp.sin(ang)
    c1 = jnp.concatenate([cos, cos], axis=1)
    s1 = jnp.concatenate([-sin, sin], axis=1)
    return c1, s1, jnp.tile(c1, (1, MLA_HEADS)), jnp.tile(s1, (1, MLA_HEADS))


def _swap_halves(w):
    k, c = w.shape
    w = w.reshape(k, c // ROPE_DIM, 2, ROPE_DIM // 2)
    return w[:, :, ::-1, :].reshape(k, c)


def _prep_even(w_in_a, w_uq, w_ukv, w_out_a, dc):
    d = w_in_a.shape[0]
    kpe = w_in_a[:, -ROPE_DIM:]
    w_in = jnp.concatenate([w_in_a, _swap_halves(kpe)], axis=1).astype(BF16)
    ql = w_uq.shape[0]
    wq = w_uq.reshape(ql, MLA_HEADS, NOPE_DIM + ROPE_DIM)
    wqn = wq[:, :, :NOPE_DIM].reshape(ql, MLA_HEADS * NOPE_DIM).astype(BF16)
    wqp = wq[:, :, NOPE_DIM:].reshape(ql, MLA_HEADS * ROPE_DIM)
    wqs = _swap_halves(wqp).astype(BF16)
    kl = w_ukv.shape[0]
    wkv = w_ukv.reshape(kl, MLA_HEADS, NOPE_DIM + V_DIM)
    wuk = jnp.transpose(wkv[:, :, :NOPE_DIM], (1, 2, 0)).astype(BF16)
    wuv = jnp.transpose(wkv[:, :, NOPE_DIM:], (1, 0, 2)).astype(BF16)
    wo = w_out_a.astype(BF16)
    return dict(w_in=w_in, wqn=wqn, wqp=wqp.astype(BF16), wqs=wqs, wuk=wuk, wuv=wuv, wo_a=wo[:dc], wo_b=wo[dc:])


def _run_group(x3, pos, conv_prev, ckv_prev, kpe_prev, hgrn_prev, p):
    b, t, d = x3.shape
    n = b * t
    depth = p["norm_ffn1"].shape[0]
    x = x3.reshape(n, d)
    c1, s1, c8, s8 = _rope_tables(pos)
    conv_new, hgrn_new, stacks = [], [], None
    for l in range(depth):
        x = _ffn(x, p["norm_ffn1"], p["wg1"], p["wu1"], p["wd1"], l)
        if l % 2 == 0:
            e = l // 2
            pe = p["even"][e]
            dc = p["w_conv"].shape[2]
            cprev = jnp.zeros((b, CONV_W - 1, dc), F32) if conv_prev is None else conv_prev[e]
            ya, cs, q, ckv_stack, kpe_stack, kv = _even_in(
                x.reshape(b, t, d), p["norm_mix"][l], pe["w_in"], cprev, p["w_conv"][e], p["g_q"][e], pe["wqn"],
                pe["wqp"], pe["wqs"], pe["wuk"], p["g_kv"][e], c1, s1, c8, s8, e, (depth + 1) // 2, stacks)
            stacks = (ckv_stack, kpe_stack)
            if ckv_prev is None:
                yb = _attention(q, kv, pe["wuv"])
            else:
                yb = _attention_cached(q, kv, ckv_prev, kpe_prev, e, pe["wuv"])
            x = _matmul_residual([ya.reshape(n, -1), yb.reshape(n, -1)], [pe["wo_a"], pe["wo_b"]], x)
            conv_new.append(cs)
        else:
            o = l // 2
            z = _norm_matmul(x, p["norm_mix"][l], p["w_in_c"], o).reshape(b, t, -1)
            if hgrn_prev is None:
                s0t = jnp.zeros((b, HG_HEADS, HG_DV, HG_DK), F32)
            else:
                s0t = jnp.swapaxes(hgrn_prev[o], -1, -2)
            og, st = _gla(z, p["lb_logits"], o, p["g_o"][o], s0t)
            x = _matmul_residual([og.reshape(n, -1)], [p["w_out_c"][o]], x)
            hgrn_new.append(jnp.swapaxes(st, -1, -2))
        g_final = p["norm_final"] if l == depth - 1 else None
        x = _ffn(x, p["norm_ffn2"], p["wg2"], p["wu2"], p["wd2"], l, g_final)
    return x.reshape(b, t, d), jnp.stack(conv_new), stacks[0], stacks[1], jnp.stack(hgrn_new)


def kernel(x_prompt, x_sample, cache_conv, cache_ckv, cache_kpe, state_hgrn, norm_ffn1, w_ffn1_gate, w_ffn1_up, w_ffn1_down, norm_mix, w_in_a, w_conv, g_q, w_uq, g_kv, w_ukv, w_out_a, w_in_c, lb_logits, g_o, w_out_c, norm_ffn2, w_ffn2_gate, w_ffn2_up, w_ffn2_down, norm_final):
    dc = w_conv.shape[2]
    p = dict(
        norm_ffn1=norm_ffn1, wg1=w_ffn1_gate.astype(BF16), wu1=w_ffn1_up.astype(BF16), wd1=w_ffn1_down.astype(BF16),
        norm_ffn2=norm_ffn2, wg2=w_ffn2_gate.astype(BF16), wu2=w_ffn2_up.astype(BF16), wd2=w_ffn2_down.astype(BF16),
        norm_mix=norm_mix, w_conv=w_conv, g_q=g_q, g_kv=g_kv,
        even=[_prep_even(w_in_a[e], w_uq[e], w_ukv[e], w_out_a[e], dc) for e in range(w_in_a.shape[0])],
        w_in_c=w_in_c.astype(BF16), lb_logits=lb_logits, g_o=g_o, w_out_c=w_out_c.astype(BF16),
        norm_final=norm_final,
    )
    pos_p = jnp.arange(x_prompt.shape[1], dtype=jnp.int32)
    pos_s = cache_ckv.shape[2] + jnp.arange(x_sample.shape[1], dtype=jnp.int32)
    y_p, conv_p, ckv_p, kpe_p, hgrn_p = _run_group(x_prompt, pos_p, None, None, None, None, p)
    y_s, conv_s, ckv_s, kpe_s, hgrn_s = _run_group(x_sample, pos_s, cache_conv, cache_ckv, cache_kpe, state_hgrn, p)
    return (y_p, y_s, conv_p, ckv_p, kpe_p, hgrn_p, conv_s, ckv_s, kpe_s, hgrn_s)
```

```python
import functools

import numpy as np
import jax
import jax.numpy as jnp
from jax import lax
from jax.experimental import pallas as pl
from jax.experimental.pallas import tpu as pltpu

F32 = jnp.float32
BF16 = jnp.bfloat16

EPS = 1e-6
CHUNK = 64
CONV_W = 3
MLA_HEADS = 8
NOPE_DIM = 128
ROPE_DIM = 64
V_DIM = 128
ROPE_THETA = 10000.0
MLA_SCALE = (NOPE_DIM + ROPE_DIM) ** -0.5
HG_HEADS = 16
HG_DK = 128
HG_DV = 128
LANES = 128
SUBLANES = 8
NEG_BIG = -1e30
LOG2E = 1.4426950408889634
LEVELS = (32, 16, 8, 4, 2)
VMEM_LIMIT_BYTES = 60 * 1024 * 1024
FFN_TF = 512


def _tile(n, pref):
    if n <= pref:
        return n
    for t in range(pref, 7, -8):
        if n % t == 0:
            return t
    return n


def _col_tile(n, pref):
    for t in range(min(pref, n) // LANES * LANES, 0, -LANES):
        if n % t == 0:
            return t
    return n


def _params(*sem, **kw):
    return pltpu.CompilerParams(dimension_semantics=sem, vmem_limit_bytes=VMEM_LIMIT_BYTES, **kw)


def _sigmoid(x):
    return 1.0 / (1.0 + jnp.exp(-x))


def _silu(x):
    h = 0.5 * x
    return h + h * jnp.tanh(h)


def _rms(x, g):
    return x * lax.rsqrt(jnp.mean(x * x, axis=-1, keepdims=True) + EPS) * g


def _dot(a, b):
    return jnp.dot(a, b, preferred_element_type=F32)


def _dot_nt(a, b):
    return lax.dot_general(a, b, (((1,), (1,)), ((), ())), preferred_element_type=F32)


def _dot_tn(a, b):
    return lax.dot_general(a, b, (((0,), (0,)), ((), ())), preferred_element_type=F32)


def _ffn_kernel(x_ref, g_ref, wg_ref, wu_ref, wd_ref, *rest, final_norm):
    if final_norm:
        gf_ref, o_ref, n_ref = rest
    else:
        o_ref, n_ref = rest
    j = pl.program_id(1)
    tm = x_ref.shape[0]
    rc = min(tm, 128)

    def rows(i):
        return pl.ds(pl.multiple_of(i * rc, rc), rc)

    @pl.when(j == 0)
    def _():
        def norm_rows(i, carry):
            x = x_ref[rows(i), :]
            n_ref[rows(i), :] = _rms(x, g_ref[...]).astype(BF16)
            o_ref[rows(i), :] = x
            return carry

        lax.fori_loop(0, tm // rc, norm_rows, 0)

    n = n_ref[...]
    hg = _dot(n, wg_ref[...])
    hu = _dot(n, wu_ref[...])
    a = _silu(hg) * (0.5 * hu)
    o_ref[...] += _dot(a.astype(BF16), wd_ref[...])

    if final_norm:
        @pl.when(j == pl.num_programs(1) - 1)
        def _():
            def norm_out_rows(i, carry):
                o_ref[rows(i), :] = _rms(o_ref[rows(i), :], gf_ref[...])
                return carry

            lax.fori_loop(0, tm // rc, norm_out_rows, 0)


def _ffn(x, g, wg, wu, wd, layer, g_final=None):
    n, d = x.shape
    f = wg.shape[2]
    tm = _tile(n, 1024)
    tf = _col_tile(f, FFN_TF)
    in_specs = [
        pl.BlockSpec((tm, d), lambda i, j: (i, 0)),
        pl.BlockSpec((None, 1, d), lambda i, j: (layer, 0, 0)),
        pl.BlockSpec((None, d, tf), lambda i, j: (layer, 0, j)),
        pl.BlockSpec((None, d, tf), lambda i, j: (layer, 0, j)),
        pl.BlockSpec((None, tf, d), lambda i, j: (layer, j, 0)),
    ]
    args = [x, g.reshape(g.shape[0], 1, d), wg, wu, wd]
    if g_final is not None:
        in_specs.append(pl.BlockSpec((1, d), lambda i, j: (0, 0)))
        args.append(g_final.reshape(1, d))
    return pl.pallas_call(
        functools.partial(_ffn_kernel, final_norm=g_final is not None),
        grid=(n // tm, f // tf),
        in_specs=in_specs,
        out_specs=pl.BlockSpec((tm, d), lambda i, j: (i, 0)),
        out_shape=jax.ShapeDtypeStruct((n, d), F32),
        scratch_shapes=[pltpu.VMEM((tm, d), BF16)],
        compiler_params=_params("parallel", "arbitrary"),
        name="ffn",
    )(*args)


def _nmm_kernel(x_ref, g_ref, w_ref, o_ref, n_ref):
    tm = x_ref.shape[0]
    rc = min(tm, 128)

    @pl.when(pl.program_id(1) == 0)
    def _():
        def norm_rows(i, carry):
            rows = pl.ds(pl.multiple_of(i * rc, rc), rc)
            n_ref[rows, :] = _rms(x_ref[rows, :], g_ref[...]).astype(BF16)
            return carry

        lax.fori_loop(0, tm // rc, norm_rows, 0)

    o_ref[...] = _dot(n_ref[...], w_ref[...])


def _norm_matmul(x, g, w, layer):
    n, d = x.shape
    c = w.shape[2]
    tm = _tile(n, 1024)
    tn = _col_tile(c, 2048)
    return pl.pallas_call(
        _nmm_kernel,
        grid=(n // tm, c // tn),
        in_specs=[
            pl.BlockSpec((tm, d), lambda i, j: (i, 0)),
            pl.BlockSpec((1, d), lambda i, j: (0, 0)),
            pl.BlockSpec((None, d, tn), lambda i, j: (layer, 0, j)),
        ],
        out_specs=pl.BlockSpec((tm, tn), lambda i, j: (i, j)),
        out_shape=jax.ShapeDtypeStruct((n, c), F32),
        scratch_shapes=[pltpu.VMEM((tm, d), BF16)],
        compiler_params=_params("parallel", "arbitrary"),
        name="norm_matmul",
    )(x, g.reshape(1, d), w)


def _mmres_kernel(*refs, n_in):
    x_ref, o_ref = refs[2 * n_in], refs[2 * n_in + 1]
    acc = x_ref[...]
    for a_ref, w_ref in zip(refs[:n_in], refs[n_in:2 * n_in]):
        acc = acc + _dot(a_ref[...], w_ref[...])
    o_ref[...] = acc


def _matmul_residual(acts, weights, x):
    n, d = x.shape
    tm = _tile(n, 512)
    in_specs = [pl.BlockSpec((tm, a.shape[1]), lambda i: (i, 0)) for a in acts]
    in_specs += [pl.BlockSpec(w.shape, lambda i: (0, 0)) for w in weights]
    in_specs.append(pl.BlockSpec((tm, d), lambda i: (i, 0)))
    return pl.pallas_call(
        functools.partial(_mmres_kernel, n_in=len(acts)),
        grid=(n // tm,),
        in_specs=in_specs,
        out_specs=pl.BlockSpec((tm, d), lambda i: (i, 0)),
        out_shape=jax.ShapeDtypeStruct((n, d), F32),
        compiler_params=_params("parallel"),
        name="matmul_residual",
    )(*acts, *weights, x)


def _even_in_kernel(x_ref, gn_ref, win_ref, cp_ref, wc_ref, gq_ref, wqn_ref, wqp_ref, wqs_ref, wuk_ref, gkv_ref,
                    c1_ref, s1_ref, c8_ref, s8_ref,
                    ya_ref, cs_ref, q_ref, ckv_ref, kpe_ref, kv_ref, vp_ref, *, dc, ql, kl):
    t = pl.program_id(1)
    tt = x_ref.shape[1]
    o3 = 3 * dc
    o4 = o3 + ql
    o5 = o4 + kl
    n = _rms(x_ref[0], gn_ref[...]).astype(BF16)

    def proj(c0, c1):
        return _dot(n, win_ref[:, c0:c1])

    cqn = _rms(proj(o3, o4), gq_ref[...]).astype(BF16)
    qn = _dot(cqn, wqn_ref[...])
    qpr = _dot(cqn, wqp_ref[...]) * c8_ref[...] + _dot(cqn, wqs_ref[...]) * s8_ref[...]
    v = proj(dc, 2 * dc) * proj(2 * dc, o3)
    gb = proj(0, dc)
    cn = _rms(proj(o4, o5), gkv_ref[...])
    kp = proj(o5, o5 + 2 * ROPE_DIM)
    kr = kp[:, 0:ROPE_DIM] * c1_ref[...] + kp[:, ROPE_DIM:2 * ROPE_DIM] * s1_ref[...]
    for h in range(MLA_HEADS):
        qlat = _dot(qn[:, h * NOPE_DIM:(h + 1) * NOPE_DIM].astype(BF16), wuk_ref[h])
        q_ref[0, h, :, 0:kl] = qlat.astype(BF16)
        q_ref[0, h, :, kl:kl + ROPE_DIM] = qpr[:, h * ROPE_DIM:(h + 1) * ROPE_DIM].astype(BF16)
    cp = cp_ref[0]
    first = t == 0
    pm2 = jnp.where(first, cp[0:1], vp_ref[SUBLANES - 2:SUBLANES - 1, :])
    pm1 = jnp.where(first, cp[1:2], vp_ref[SUBLANES - 1:SUBLANES, :])
    rows = lax.broadcasted_iota(jnp.int32, v.shape, 0)
    v1 = jnp.where(rows == 0, pm1, pltpu.roll(v, 1, 0))
    v2 = jnp.where(rows == 0, pm2, jnp.where(rows == 1, pm1, pltpu.roll(v, 2, 0)))
    w = wc_ref[...]
    conv = v2 * w[0:1] + v1 * w[1:2] + v * w[2:3]
    ya_ref[0] = (gb * conv).astype(BF16)
    vp_ref[...] = v[tt - SUBLANES:tt]
    ckv_ref[0] = cn
    kpe_ref[0] = kr
    kv_ref[0, :, 0:kl] = cn.astype(BF16)
    kv_ref[0, :, kl:kl + ROPE_DIM] = kr.astype(BF16)

    @pl.when(t == pl.num_programs(1) - 1)
    def _():
        cs_ref[0] = v[tt - (CONV_W - 1):tt]


EVEN_IN_INPUTS = 15


def _even_in_entry(*refs, n_alias, **kw):
    return _even_in_kernel(*refs[:EVEN_IN_INPUTS], *refs[EVEN_IN_INPUTS + n_alias:], **kw)


def _even_in(x, g_norm, w_in, conv_prev, w_conv, g_q, wqn, wqp, wqs, wuk, g_kv, c1, s1, c8, s8, e, n_even, stacks):
    b, t, d = x.shape
    dc = w_conv.shape[1]
    ql = g_q.shape[0]
    kl = g_kv.shape[0]
    tt = _tile(t, 256)
    hr = MLA_HEADS * ROPE_DIM
    once = lambda a: pl.BlockSpec(a.shape, lambda i, j: (0,) * a.ndim, pipeline_mode=pl.Buffered(1))
    wts = [g_norm.reshape(1, d), w_in]
    wts2 = [w_conv, g_q.reshape(1, ql), wqn, wqp, wqs, wuk, g_kv.reshape(1, kl)]
    in_specs = [pl.BlockSpec((1, tt, d), lambda i, j: (i, j, 0))] + [once(a) for a in wts] + [
        pl.BlockSpec((1, CONV_W - 1, dc), lambda i, j: (i, 0, 0)),
    ] + [once(a) for a in wts2] + [
        pl.BlockSpec((tt, ROPE_DIM), lambda i, j: (j, 0)),
        pl.BlockSpec((tt, ROPE_DIM), lambda i, j: (j, 0)),
        pl.BlockSpec((tt, hr), lambda i, j: (j, 0)),
        pl.BlockSpec((tt, hr), lambda i, j: (j, 0)),
    ]
    out_shape = [
        jax.ShapeDtypeStruct((b, t, dc), BF16),
        jax.ShapeDtypeStruct((b, CONV_W - 1, dc), F32),
        jax.ShapeDtypeStruct((b, MLA_HEADS, t, kl + ROPE_DIM), BF16),
        jax.ShapeDtypeStruct((n_even, b, t, kl), F32),
        jax.ShapeDtypeStruct((n_even, b, t, ROPE_DIM), F32),
        jax.ShapeDtypeStruct((b, t, kl + ROPE_DIM), BF16),
    ]
    out_specs = [
        pl.BlockSpec((1, tt, dc), lambda i, j: (i, j, 0)),
        pl.BlockSpec((1, CONV_W - 1, dc), lambda i, j: (i, 0, 0)),
        pl.BlockSpec((1, MLA_HEADS, tt, kl + ROPE_DIM), lambda i, j: (i, 0, j, 0)),
        pl.BlockSpec((None, 1, tt, kl), lambda i, j: (e, i, j, 0)),
        pl.BlockSpec((None, 1, tt, ROPE_DIM), lambda i, j: (e, i, j, 0)),
        pl.BlockSpec((1, tt, kl + ROPE_DIM), lambda i, j: (i, j, 0)),
    ]
    args = [x, *wts, conv_prev, *wts2, c1, s1, c8, s8]
    assert len(args) == EVEN_IN_INPUTS
    aliases = {}
    if stacks is not None:
        in_specs += [pl.BlockSpec(memory_space=pl.ANY)] * len(stacks)
        aliases = {EVEN_IN_INPUTS: 3, EVEN_IN_INPUTS + 1: 4}
        args += list(stacks)
    return pl.pallas_call(
        functools.partial(_even_in_entry, n_alias=len(aliases), dc=dc, ql=ql, kl=kl),
        grid=(b, t // tt),
        in_specs=in_specs,
        out_specs=out_specs,
        out_shape=out_shape,
        input_output_aliases=aliases,
        scratch_shapes=[pltpu.VMEM((SUBLANES, dc), F32)],
        compiler_params=_params("parallel", "arbitrary"),
        name="even_in",
    )(*args)


def _online_softmax(score_fns, value_fns, mask_last):
    c = MLA_SCALE * LOG2E
    n = len(score_fns)
    s = score_fns[0]()
    m = l = acc = None
    for j in range(n):
        s_next = score_fns[j + 1]() if j + 1 < n else None
        if j == n - 1 and mask_last is not None:
            s = mask_last(s)
        m_new = jnp.max(s, axis=-1, keepdims=True)
        if m is not None:
            m_new = jnp.maximum(m, m_new)
        p = jnp.exp2((s - m_new) * c)
        p_sum = jnp.sum(p, axis=-1, keepdims=True)
        pv = _dot(p.astype(BF16), value_fns[j]())
        if m is None:
            l, acc = p_sum, pv
        else:
            alpha = jnp.exp2((m - m_new) * c)
            l = alpha * l + p_sum
            acc = alpha * acc + pv
        m, s = m_new, s_next
    return acc * (1.0 / l)


def _value_up(o, wuv_ref, o_ref, tq):
    o = o.astype(BF16)
    for h in range(MLA_HEADS):
        o_ref[0, :, h * V_DIM:(h + 1) * V_DIM] = _dot(o[h * tq:(h + 1) * tq], wuv_ref[h]).astype(BF16)


def _attn_kernel(q_ref, kv_ref, wuv_ref, o_ref, *, tq, tk, kl):
    i = pl.program_id(1)
    rows = MLA_HEADS * tq
    q = q_ref[0].reshape(rows, q_ref.shape[3])
    tok = i * tq + lax.rem(lax.broadcasted_iota(jnp.int32, (rows, 1), 0), tq)
    limit = (tok // CHUNK + 1) * CHUNK

    def run(n_full, tail):
        bounds = [(j * tk, (j + 1) * tk) for j in range(n_full)] + [(n_full * tk, n_full * tk + tail)]

        def mask_last(s):
            kpos = n_full * tk + lax.broadcasted_iota(jnp.int32, (1, tail), 1)
            return jnp.where(kpos < limit, s, NEG_BIG)

        score_fns = [functools.partial(lambda lo, hi: _dot_nt(q, kv_ref[0, lo:hi, :]), lo, hi) for lo, hi in bounds]
        value_fns = [functools.partial(lambda lo, hi: kv_ref[0, lo:hi, 0:kl], lo, hi) for lo, hi in bounds]
        _value_up(_online_softmax(score_fns, value_fns, mask_last), wuv_ref, o_ref, tq)

    start = i * tq
    n_full = start // tk
    in_second_half = lax.rem(start, tk) >= tk // 2
    for a in range(kv_ref.shape[1] // tk):
        pl.when((n_full == a) & jnp.logical_not(in_second_half))(functools.partial(run, a, tk // 2))
        pl.when((n_full == a) & in_second_half)(functools.partial(run, a, tk))


def _attn_cache_kernel(q_ref, kv_ref, cc_ref, ck_ref, wuv_ref, o_ref, *, tk, kl):
    tq = q_ref.shape[2]
    rows = MLA_HEADS * tq
    q = q_ref[0].reshape(rows, q_ref.shape[3])
    q_lat, q_pe = q[:, 0:kl], q[:, kl:]
    n_cache = cc_ref.shape[1] // tk

    def cache_scores(j):
        ks = slice(j * tk, (j + 1) * tk)
        return _dot_nt(q_lat, cc_ref[0, ks, :].astype(BF16)) + _dot_nt(q_pe, ck_ref[0, ks, :].astype(BF16))

    score_fns = [functools.partial(cache_scores, j) for j in range(n_cache)]
    value_fns = [functools.partial(lambda j: cc_ref[0, j * tk:(j + 1) * tk, :].astype(BF16), j) for j in range(n_cache)]
    score_fns.append(lambda: _dot_nt(q, kv_ref[0]))
    value_fns.append(lambda: kv_ref[0, :, 0:kl])
    _value_up(_online_softmax(score_fns, value_fns, None), wuv_ref, o_ref, tq)


def _attention_cached(q, kv_new, cache_ckv, cache_kpe, layer, wuv):
    b, h, t, dq = q.shape
    past = cache_ckv.shape[2]
    kl = wuv.shape[1]
    tk = _tile(past, 512)
    assert past % tk == 0
    return pl.pallas_call(
        functools.partial(_attn_cache_kernel, tk=tk, kl=kl),
        grid=(b,),
        in_specs=[
            pl.BlockSpec((1, h, t, dq), lambda i: (i, 0, 0, 0)),
            pl.BlockSpec((1, t, dq), lambda i: (i, 0, 0)),
            pl.BlockSpec((None, 1, past, kl), lambda i: (layer, i, 0, 0)),
            pl.BlockSpec((None, 1, past, dq - kl), lambda i: (layer, i, 0, 0)),
            pl.BlockSpec(wuv.shape, lambda i: (0, 0, 0)),
        ],
        out_specs=pl.BlockSpec((1, t, h * V_DIM), lambda i: (i, 0, 0)),
        out_shape=jax.ShapeDtypeStruct((b, t, h * V_DIM), BF16),
        compiler_params=_params("parallel"),
        name="mla_attention_cached",
    )(q, kv_new, cache_ckv, cache_kpe, wuv)


def _attention(q, kv, wuv):
    b, h, t, dq = q.shape
    tkv = kv.shape[1]
    kl = wuv.shape[1]
    tq = _tile(t, 128)
    tk = _tile(tkv, 512)
    assert (tk // 2) % tq == 0 and tkv % tk == 0 and tq % CHUNK == 0
    return pl.pallas_call(
        functools.partial(_attn_kernel, tq=tq, tk=tk, kl=kl),
        grid=(b, t // tq),
        in_specs=[
            pl.BlockSpec((1, h, tq, dq), lambda i, j: (i, 0, j, 0)),
            pl.BlockSpec((1, tkv, dq), lambda i, j: (i, 0, 0)),
            pl.BlockSpec(wuv.shape, lambda i, j: (0, 0, 0)),
        ],
        out_specs=pl.BlockSpec((1, tq, h * V_DIM), lambda i, j: (i, j, 0)),
        out_shape=jax.ShapeDtypeStruct((b, t, h * V_DIM), BF16),
        compiler_params=_params("parallel", "arbitrary"),
        name="mla_attention",
    )(q, kv, wuv)


def _level_masks():
    t = np.arange(CHUNK)[:, None]
    s = np.arange(CHUNK)[None, :]
    masks = [(t // (2 * m) == s // (2 * m)) & ((t & m) != 0) & ((s & m) == 0) for m in LEVELS]
    return np.stack(masks).astype(np.float32)


def _boundary_rows(a, m):
    w = a.shape[1]
    if m >= SUBLANES:
        parts = [jnp.broadcast_to(a[b0 + m - 1:b0 + m], (2 * m, w)) for b0 in range(0, CHUNK, 2 * m)]
        return parts[0] if len(parts) == 1 else jnp.concatenate(parts, axis=0)
    a3 = a.reshape(CHUNK // SUBLANES, SUBLANES, w)
    sub = lax.broadcasted_iota(jnp.int32, a3.shape, 1)
    out = None
    for b0 in range(SUBLANES - 2 * m, -1, -2 * m):
        piece = jnp.broadcast_to(a3[:, b0 + m - 1:b0 + m, :], a3.shape)
        out = piece if out is None else jnp.where(sub < b0 + 2 * m, piece, out)
    return out.reshape(CHUNK, w)


def _prev_row_in_group(x):
    w = x.shape[1]
    x3 = x.reshape(CHUNK // SUBLANES, SUBLANES, w)
    return pltpu.roll(x3, 1, 1).reshape(CHUNK, w)


def _level_exponent(a, m):
    if m < SUBLANES:
        return -jnp.abs(a - _boundary_rows(a, m))
    parts = []
    for b0 in range(0, CHUNK, 2 * m):
        r = a[b0 + m - 1:b0 + m]
        parts += [r - a[b0:b0 + m], a[b0 + m:b0 + 2 * m] - r]
    return jnp.concatenate(parts, axis=0)


def _pick_halves(first, second, m, is_second):
    if m < SUBLANES:
        return jnp.where(is_second, second, first)
    parts = []
    for b0 in range(0, CHUNK, 2 * m):
        parts += [first[b0:b0 + m], second[b0 + m:b0 + 2 * m]]
    return jnp.concatenate(parts, axis=0)


def _gla_kernel(zq_ref, zf_ref, zi_ref, zg_ref, lbl_ref, go_ref, s0_ref, mk_ref, o_ref, st_ref, *, layer):
    t = pl.program_id(1)
    tc = zq_ref.shape[1]
    n_lv = len(LEVELS)

    @pl.when(t == 0)
    def _():
        st_ref[...] = s0_ref[...]

    ll = lbl_ref[...]
    e = jnp.exp(ll - jnp.max(ll, axis=0, keepdims=True))
    sm = e / jnp.sum(e, axis=0, keepdims=True)
    lb = jnp.zeros((1, ll.shape[1]), F32)
    for r in range(1, layer + 1):
        lb = lb + sm[r:r + 1]
    one_m_lb = 1.0 - lb
    go = go_ref[...]
    row = lax.broadcasted_iota(jnp.int32, (CHUNK, 1), 0)
    second = [(row & m) != 0 for m in LEVELS]
    odd = (row & 1) != 0

    def wide(c):
        rs = pl.ds(pl.multiple_of(c * CHUNK, CHUNK), CHUNK)
        zq = zq_ref[0, rs, :]
        zf = zf_ref[0, rs, :]
        v = zi_ref[0, rs, :]
        zg = zg_ref[0, rs, :]
        sig = _sigmoid(zf)
        g = jnp.log(lb + one_m_lb * sig) * LOG2E
        k = one_m_lb * (1.0 - sig)
        q = _silu(zq)
        gate = _silu(zg)
        a = g
        for sh in (1, 2, 4):
            rolled = pltpu.roll(a, sh, 0)
            top = jnp.where(row[:SUBLANES] >= sh, rolled[:SUBLANES], 0.0)
            a = a + jnp.concatenate([top, rolled[SUBLANES:]], axis=0)
        for sh in (8, 16, 32):
            a = a + jnp.concatenate([jnp.zeros((sh, a.shape[1]), F32), a[:CHUNK - sh]], axis=0)
        e_q = jnp.exp2(a)
        e_k = jnp.exp2(a[CHUNK - 1:CHUNK] - a)
        e_last = e_q[CHUNK - 1:CHUNK]
        qt = (q * e_q).astype(BF16)
        kt = (k * e_k).astype(BF16)
        qk = q * k
        vb = v.astype(BF16)
        pair = q * jnp.exp2(g) * _prev_row_in_group(k)
        v_prev = _prev_row_in_group(v)
        ops = []
        for li, m in enumerate(LEVELS):
            ex = jnp.exp2(_level_exponent(a, m))
            ops.append((_pick_halves(k, q, m, second[li]) * ex).astype(BF16))
        return ops, qt, kt, vb, qk, v, gate, e_last, pair, v_prev

    def heads(c, w, states):
        ops, qt, kt, vb, qk, v, gate, e_last, pair, v_prev = w
        rs = pl.ds(pl.multiple_of(c * CHUNK, CHUNK), CHUNK)
        sls = [slice(h * HG_DK, (h + 1) * HG_DK) for h in range(HG_HEADS)]
        ps = []
        for sl in sls:
            groups = [None] * (CHUNK // SUBLANES)
            for li, m in enumerate(LEVELS):
                r = _dot_nt(ops[li][:, sl], ops[li][:, sl])
                for gi in range(len(groups)):
                    if m >= SUBLANES and not (gi * SUBLANES) & m:
                        continue
                    gs = slice(gi * SUBLANES, (gi + 1) * SUBLANES)
                    term = r[gs] * mk_ref[li, gs, :]
                    groups[gi] = term if groups[gi] is None else groups[gi] + term
            ps.append(jnp.concatenate(groups, axis=0).astype(BF16))
        new_states = [states[h] * e_last[:, sl] + _dot_tn(vb[:, sl], kt[:, sl]) for h, sl in enumerate(sls)]
        os_ = [_dot_nt(qt[:, sl], states[h].astype(BF16)) + _dot(ps[h], vb[:, sl]) for h, sl in enumerate(sls)]
        for h, sl in enumerate(sls):
            o = os_[h] + jnp.sum(qk[:, sl], axis=-1, keepdims=True) * v[:, sl]
            c1 = jnp.where(odd, jnp.sum(pair[:, sl], axis=-1, keepdims=True), 0.0)
            o = o + c1 * v_prev[:, sl]
            og = _rms(o, go) * gate[:, sl]
            o_ref[0, rs, sl] = og.astype(BF16)
        return new_states

    n_ch = tc // CHUNK
    states = [st_ref[0, h] for h in range(HG_HEADS)]
    w = wide(0)
    for c in range(n_ch):
        w_next = wide(c + 1) if c + 1 < n_ch else None
        states = heads(c, w, states)
        w = w_next
    for h in range(HG_HEADS):
        st_ref[0, h] = states[h]


def _gla(z, lb_logits, layer, g_o, s0t):
    b, t, zw = z.shape
    dh = zw // 4
    tc = _tile(t, 256)
    masks = jnp.asarray(_level_masks(), F32)
    zspec = lambda c: pl.BlockSpec((1, tc, dh), lambda i, j, c=c: (i, j, c))
    sspec = pl.BlockSpec((1,) + s0t.shape[1:], lambda i, j: (i, 0, 0, 0))
    return pl.pallas_call(
        functools.partial(_gla_kernel, layer=layer),
        grid=(b, t // tc),
        in_specs=[zspec(0), zspec(1), zspec(2), zspec(3),
                  pl.BlockSpec(lb_logits.shape, lambda i, j: (0, 0)),
                  pl.BlockSpec((1, HG_DV), lambda i, j: (0, 0)),
                  sspec,
                  pl.BlockSpec(masks.shape, lambda i, j: (0, 0, 0))],
        out_specs=[pl.BlockSpec((1, tc, dh), lambda i, j: (i, j, 0)), sspec],
        out_shape=[jax.ShapeDtypeStruct((b, t, dh), BF16), jax.ShapeDtypeStruct(s0t.shape, F32)],
        compiler_params=_params("parallel", "arbitrary"),
        name="hgrn2_recurrence",
    )(z, z, z, z, lb_logits, g_o.reshape(1, HG_DV), s0t, masks)


def _rope_tables(pos):
    half = ROPE_DIM // 2
    inv = ROPE_THETA ** (-jnp.arange(half, dtype=F32) / half)
    ang = pos.astype(F32)[:, None] * inv[None, :]
    cos, sin = jnp.cos(ang), jnp.sin(ang)
    c1 = jnp.concatenate([cos, cos], axis=1)
    s1 = jnp.concatenate([-sin, sin], axis=1)
    return c1, s1, jnp.tile(c1, (1, MLA_HEADS)), jnp.tile(s1, (1, MLA_HEADS))


def _swap_halves(w):
    k, c = w.shape
    w = w.reshape(k, c // ROPE_DIM, 2, ROPE_DIM // 2)
    return w[:, :, ::-1, :].reshape(k, c)


def _prep_even(w_in_a, w_uq, w_ukv, w_out_a, dc):
    d = w_in_a.shape[0]
    kpe = w_in_a[:, -ROPE_DIM:]
    w_in = jnp.concatenate([w_in_a, _swap_halves(kpe)], axis=1).astype(BF16)
    ql = w_uq.shape[0]
    wq = w_uq.reshape(ql, MLA_HEADS, NOPE_DIM + ROPE_DIM)
    wqn = wq[:, :, :NOPE_DIM].reshape(ql, MLA_HEADS * NOPE_DIM).astype(BF16)
    wqp = wq[:, :, NOPE_DIM:].reshape(ql, MLA_HEADS * ROPE_DIM)
    wqs = _swap_halves(wqp).astype(BF16)
    kl = w_ukv.shape[0]
    wkv = w_ukv.reshape(kl, MLA_HEADS, NOPE_DIM + V_DIM)
    wuk = jnp.transpose(wkv[:, :, :NOPE_DIM], (1, 2, 0)).astype(BF16)
    wuv = jnp.transpose(wkv[:, :, NOPE_DIM:], (1, 0, 2)).astype(BF16)
    wo = w_out_a.astype(BF16)
    return dict(w_in=w_in, wqn=wqn, wqp=wqp.astype(BF16), wqs=wqs, wuk=wuk, wuv=wuv, wo_a=wo[:dc], wo_b=wo[dc:])


def _run_group(x3, pos, conv_prev, ckv_prev, kpe_prev, hgrn_prev, p):
    b, t, d = x3.shape
    n = b * t
    depth = p["norm_ffn1"].shape[0]
    x = x3.reshape(n, d)
    c1, s1, c8, s8 = _rope_tables(pos)
    conv_new, hgrn_new, stacks = [], [], None
    for l in range(depth):
        x = _ffn(x, p["norm_ffn1"], p["wg1"], p["wu1"], p["wd1"], l)
        if l % 2 == 0:
            e = l // 2
            pe = p["even"][e]
            dc = p["w_conv"].shape[2]
            cprev = jnp.zeros((b, CONV_W - 1, dc), F32) if conv_prev is None else conv_prev[e]
            ya, cs, q, ckv_stack, kpe_stack, kv = _even_in(
                x.reshape(b, t, d), p["norm_mix"][l], pe["w_in"], cprev, p["w_conv"][e], p["g_q"][e], pe["wqn"],
                pe["wqp"], pe["wqs"], pe["wuk"], p["g_kv"][e], c1, s1, c8, s8, e, (depth + 1) // 2, stacks)
            stacks = (ckv_stack, kpe_stack)
            if ckv_prev is None:
                yb = _attention(q, kv, pe["wuv"])
            else:
                yb = _attention_cached(q, kv, ckv_prev, kpe_prev, e, pe["wuv"])
            x = _matmul_residual([ya.reshape(n, -1), yb.reshape(n, -1)], [pe["wo_a"], pe["wo_b"]], x)
            conv_new.append(cs)
        else:
            o = l // 2
            z = _norm_matmul(x, p["norm_mix"][l], p["w_in_c"], o).reshape(b, t, -1)
            if hgrn_prev is None:
                s0t = jnp.zeros((b, HG_HEADS, HG_DV, HG_DK), F32)
            else:
                s0t = jnp.swapaxes(hgrn_prev[o], -1, -2)
            og, st = _gla(z, p["lb_logits"], o, p["g_o"][o], s0t)
            x = _matmul_residual([og.reshape(n, -1)], [p["w_out_c"][o]], x)
            hgrn_new.append(jnp.swapaxes(st, -1, -2))
        g_final = p["norm_final"] if l == depth - 1 else None
        x = _ffn(x, p["norm_ffn2"], p["wg2"], p["wu2"], p["wd2"], l, g_final)
    return x.reshape(b, t, d), jnp.stack(conv_new), stacks[0], stacks[1], jnp.stack(hgrn_new)


def kernel(x_prompt, x_sample, cache_conv, cache_ckv, cache_kpe, state_hgrn, norm_ffn1, w_ffn1_gate, w_ffn1_up, w_ffn1_down, norm_mix, w_in_a, w_conv, g_q, w_uq, g_kv, w_ukv, w_out_a, w_in_c, lb_logits, g_o, w_out_c, norm_ffn2, w_ffn2_gate, w_ffn2_up, w_ffn2_down, norm_final):
    dc = w_conv.shape[2]
    p = dict(
        norm_ffn1=norm_ffn1, wg1=w_ffn1_gate.astype(BF16), wu1=w_ffn1_up.astype(BF16), wd1=w_ffn1_down.astype(BF16),
        norm_ffn2=norm_ffn2, wg2=w_ffn2_gate.astype(BF16), wu2=w_ffn2_up.astype(BF16), wd2=w_ffn2_down.astype(BF16),
        norm_mix=norm_mix, w_conv=w_conv, g_q=g_q, g_kv=g_kv,
        even=[_prep_even(w_in_a[e], w_uq[e], w_ukv[e], w_out_a[e], dc) for e in range(w_in_a.shape[0])],
        w_in_c=w_in_c.astype(BF16), lb_logits=lb_logits, g_o=g_o, w_out_c=w_out_c.astype(BF16),
        norm_final=norm_final,
    )
    pos_p = jnp.arange(x_prompt.shape[1], dtype=jnp.int32)
    pos_s = cache_ckv.shape[2] + jnp.arange(x_sample.shape[1], dtype=jnp.int32)
    y_p, conv_p, ckv_p, kpe_p, hgrn_p = _run_group(x_prompt, pos_p, None, None, None, None, p)
    y_s, conv_s, ckv_s, kpe_s, hgrn_s = _run_group(x_sample, pos_s, cache_conv, cache_ckv, cache_kpe, state_hgrn, p)
    return (y_p, y_s, conv_p, ckv_p, kpe_p, hgrn_p, conv_s, ckv_s, kpe_s, hgrn_s)
```
